```python
import math
import jax, jax.numpy as jnp
from jax import lax
import numpy as np

D_MODEL = 1024
BATCH = 16
SEQ = 2048
DEPTH = 1

CHUNK = 64
EPS = 1e-6
S5_WIDTH = 512
S5_GROUP = 16
S5_GROUPS = S5_WIDTH // S5_GROUP
S5_STATE = 64
HG_WIDTH = 512
HG_HEAD_DIM = 128
HG_HEADS = HG_WIDTH // HG_HEAD_DIM
D_FF = 2816
CONV_W = 3
N_IN = S5_WIDTH + 4 * HG_WIDTH + 2 * D_MODEL

kernel_name = "hybrid_s5_hgrn2_convffn_block"


def rmsnorm(x, gain):
    x32 = x.astype(jnp.float32)
    y = x32 * lax.rsqrt(jnp.mean(x32 * x32, axis=-1, keepdims=True) + EPS)
    return (y * gain.astype(jnp.float32)).astype(x.dtype)


def s5_mixer(u, a_re, a_im, log_dt, b_re, b_im, c_re, c_im, d_skip):
    bsz, seq, _ = u.shape
    f32 = jnp.float32
    ug = u.astype(f32).reshape(bsz, seq, S5_GROUPS, S5_GROUP)
    a_re = a_re.astype(f32); a_im = a_im.astype(f32)
    dt = jnp.exp(log_dt.astype(f32))[:, None]
    mag = jnp.exp(a_re * dt)
    ang = a_im * dt
    lb_re = mag * jnp.cos(ang)
    lb_im = mag * jnp.sin(ang)
    den = a_re * a_re + a_im * a_im
    n_re = lb_re - 1.0
    n_im = lb_im
    co_re = ((n_re * a_re + n_im * a_im) / den)[..., None]
    co_im = ((n_im * a_re - n_re * a_im) / den)[..., None]
    b_re = b_re.astype(f32); b_im = b_im.astype(f32)
    bb_re = co_re * b_re - co_im * b_im
    bb_im = co_re * b_im + co_im * b_re
    bu_re = jnp.einsum('bsgh,gph->bsgp', ug, bb_re)
    bu_im = jnp.einsum('bsgh,gph->bsgp', ug, bb_im)
    al_re = jnp.broadcast_to(lb_re, (1, seq, S5_GROUPS, S5_STATE))
    al_im = jnp.broadcast_to(lb_im, (1, seq, S5_GROUPS, S5_STATE))

    def combine(left, right):
        ar1, ai1, br1, bi1 = left
        ar2, ai2, br2, bi2 = right
        return (ar2 * ar1 - ai2 * ai1,
                ar2 * ai1 + ai2 * ar1,
                ar2 * br1 - ai2 * bi1 + br2,
                ar2 * bi1 + ai2 * br1 + bi2)

    _, _, x_re, x_im = lax.associative_scan(combine, (al_re, al_im, bu_re, bu_im), axis=1)
    y = (jnp.einsum('bsgp,ghp->bsgh', x_re, c_re.astype(f32))
         - jnp.einsum('bsgp,ghp->bsgh', x_im, c_im.astype(f32))
         + d_skip.astype(f32).reshape(S5_GROUPS, S5_GROUP) * ug)
    return y.reshape(bsz, seq, S5_WIDTH).astype(u.dtype)


def hgrn2_mixer(q, f_raw, i, g, lb, norm_gain):
    bsz, seq, _ = q.shape
    nc = seq // CHUNK
    f32 = jnp.float32
    lb = lb.astype(f32).reshape(HG_HEADS, HG_HEAD_DIM)
    f = lb + (1.0 - lb) * jax.nn.sigmoid(f_raw.astype(f32).reshape(bsz, seq, HG_HEADS, HG_HEAD_DIM))
    log_f = jnp.log(f)
    k = 1.0 - f
    qa = jax.nn.silu(q.astype(f32)) * (HG_HEAD_DIM ** -0.5)

    def blocks(t):
        return t.astype(f32).reshape(bsz, nc, CHUNK, HG_HEADS, HG_HEAD_DIM).transpose(1, 0, 2, 3, 4)

    causal = jnp.tril(jnp.ones((CHUNK, CHUNK), dtype=bool))[None, :, :, None, None]

    def step(state, xs):
        qc, kc, ic, lfc = xs
        bcum = jnp.cumsum(lfc, axis=1)
        o_inter = jnp.einsum('bthk,bhkv->bthv', qc * jnp.exp(bcum), state)
        decay = jnp.exp(jnp.where(causal, bcum[:, :, None] - bcum[:, None, :], -jnp.inf))
        scores = jnp.sum(qc[:, :, None] * kc[:, None, :] * decay, axis=-1)
        o_intra = jnp.einsum('btsh,bshv->bthv', scores, ic)
        b_last = bcum[:, -1]
        state = (jnp.exp(b_last)[..., None] * state
                 + jnp.einsum('bshk,bshv->bhkv', kc * jnp.exp(b_last[:, None] - bcum), ic))
        return state, o_inter + o_intra

    state0 = jnp.zeros((bsz, HG_HEADS, HG_HEAD_DIM, HG_HEAD_DIM), f32)
    _, o = lax.scan(step, state0, (blocks(qa), blocks(k), blocks(i), blocks(log_f)))
    o = o.transpose(1, 0, 2, 3, 4).reshape(bsz, seq, HG_HEADS, HG_HEAD_DIM)
    o = rmsnorm(o, norm_gain.reshape(HG_HEADS, HG_HEAD_DIM))
    o = o.reshape(bsz, seq, HG_WIDTH) * jax.nn.silu(g.astype(f32))
    return o.astype(q.dtype)


def conv_ffn(u, w_up, w_conv, b_conv, w_down):
    seq = u.shape[1]
    h = u @ w_up
    hp = jnp.pad(h, ((0, 0), (CONV_W - 1, 0), (0, 0)))
    hc = hp[:, 0:seq] * w_conv[0]
    for j in range(1, CONV_W):
        hc = hc + hp[:, j:j + seq] * w_conv[j]
    hc = hc + b_conv
    gate, val = jnp.split(hc, 2, axis=-1)
    return (jax.nn.silu(gate) * val) @ w_down


def setup_inputs(seed: int = 0) -> dict:
    key = jax.random.key(seed)
    ks = jax.random.split(key, 24)
    f32 = jnp.float32
    L = DEPTH
    nrm = lambda k, shp, s: jax.random.normal(k, shp, f32) * s
    x = jax.random.normal(ks[0], (BATCH, SEQ, D_MODEL), f32)
    g_mix = 1.0 + nrm(ks[1], (L, D_MODEL), 0.01)
    w_in = nrm(ks[2], (L, D_MODEL, N_IN), D_MODEL ** -0.5)
    s5_a_re = -0.5 * (1.0 + nrm(ks[3], (L, S5_GROUPS, S5_STATE), 0.02))
    s5_a_im = jnp.broadcast_to(jnp.pi * jnp.arange(S5_STATE, dtype=f32), (L, S5_GROUPS, S5_STATE)) + nrm(ks[4], (L, S5_GROUPS, S5_STATE), 0.01)
    s5_log_dt = jax.random.uniform(ks[5], (L, S5_GROUPS), f32, math.log(1e-3), math.log(1e-1))
    s5_b_re = nrm(ks[6], (L, S5_GROUPS, S5_STATE, S5_GROUP), (2 * S5_GROUP) ** -0.5)
    s5_b_im = nrm(ks[7], (L, S5_GROUPS, S5_STATE, S5_GROUP), (2 * S5_GROUP) ** -0.5)
    s5_c_re = nrm(ks[8], (L, S5_GROUPS, S5_GROUP, S5_STATE), (2 * S5_STATE) ** -0.5)
    s5_c_im = nrm(ks[9], (L, S5_GROUPS, S5_GROUP, S5_STATE), (2 * S5_STATE) ** -0.5)
    s5_d = nrm(ks[10], (L, S5_WIDTH), 1.0)
    w_glu = nrm(ks[11], (L, S5_WIDTH, S5_WIDTH), S5_WIDTH ** -0.5)
    b_glu = nrm(ks[12], (L, S5_WIDTH), 0.01)
    hg_lb_logits = nrm(ks[13], (L + 1, HG_WIDTH), 0.1)
    hg_norm_gain = 1.0 + nrm(ks[14], (L, HG_WIDTH), 0.01)
    w_pa = nrm(ks[15], (L, S5_WIDTH, D_MODEL), S5_WIDTH ** -0.5)
    w_pb = nrm(ks[16], (L, HG_WIDTH, D_MODEL), HG_WIDTH ** -0.5)
    w_out = nrm(ks[17], (L, D_MODEL, D_MODEL), D_MODEL ** -0.5)
    g_ffn = 1.0 + nrm(ks[18], (L, D_MODEL), 0.01)
    w_up = nrm(ks[19], (L, D_MODEL, 2 * D_FF), D_MODEL ** -0.5)
    w_conv = nrm(ks[20], (L, CONV_W, 2 * D_FF), CONV_W ** -0.5)
    b_conv = nrm(ks[21], (L, 2 * D_FF), 0.01)
    w_down = nrm(ks[22], (L, D_FF, D_MODEL), D_FF ** -0.5)
    g_final = 1.0 + nrm(ks[23], (D_MODEL,), 0.01)
    return {"x": x, "g_mix": g_mix, "w_in": w_in, "s5_a_re": s5_a_re, "s5_a_im": s5_a_im,
            "s5_log_dt": s5_log_dt, "s5_b_re": s5_b_re, "s5_b_im": s5_b_im, "s5_c_re": s5_c_re,
            "s5_c_im": s5_c_im, "s5_d": s5_d, "w_glu": w_glu, "b_glu": b_glu,
            "hg_lb_logits": hg_lb_logits, "hg_norm_gain": hg_norm_gain, "w_pa": w_pa, "w_pb": w_pb,
            "w_out": w_out, "g_ffn": g_ffn, "w_up": w_up, "w_conv": w_conv, "b_conv": b_conv,
            "w_down": w_down, "g_final": g_final}


def reference(x, g_mix, w_in, s5_a_re, s5_a_im, s5_log_dt, s5_b_re, s5_b_im, s5_c_re, s5_c_im,
              s5_d, w_glu, b_glu, hg_lb_logits, hg_norm_gain, w_pa, w_pb, w_out, g_ffn, w_up,
              w_conv, b_conv, w_down, g_final):
    lb_all = jnp.cumsum(jax.nn.softmax(hg_lb_logits.astype(jnp.float32), axis=0), axis=0)
    splits = [S5_WIDTH + j * HG_WIDTH for j in range(5)] + [S5_WIDTH + 4 * HG_WIDTH + D_MODEL]
    for l in range(DEPTH):
        u = rmsnorm(x, g_mix[l])
        z = u @ w_in[l]
        za, zq, zf, zi, zg, zga, zgb = jnp.split(z, splits, axis=-1)
        ya = s5_mixer(za, s5_a_re[l], s5_a_im[l], s5_log_dt[l], s5_b_re[l], s5_b_im[l],
                      s5_c_re[l], s5_c_im[l], s5_d[l])
        ya = jax.nn.gelu(ya)
        ya = ya * jax.nn.sigmoid(ya @ w_glu[l] + b_glu[l])
        yb = hgrn2_mixer(zq, zf, zi, zg, lb_all[l], hg_norm_gain[l])
        m = jax.nn.sigmoid(zga) * (ya @ w_pa[l]) + jax.nn.sigmoid(zgb) * (yb @ w_pb[l])
        x = x + m @ w_out[l]
        x = x + conv_ffn(rmsnorm(x, g_ffn[l]), w_up[l], w_conv[l], b_conv[l], w_down[l])
    return rmsnorm(x, g_final)
```

```python
import functools
import math

import jax
import jax.numpy as jnp
from jax import lax
from jax.experimental import pallas as pl
from jax.experimental.pallas import tpu as pltpu

EPS = 1e-6
S5_GROUP = 16
S5_STATE = 64
HG_HEAD_DIM = 128
CONV_W = 3

LANES = 128
SUBLANES = 8
GROUPS_PER_TILE = LANES // S5_GROUP
STATE_TILE = GROUPS_PER_TILE * S5_STATE
FFN_CHUNK = 256
VMEM_LIMIT = 56 * 1024 * 1024

F32 = jnp.float32
BF16 = jnp.bfloat16


def _dot(a, b):
    return jnp.dot(a, b, preferred_element_type=F32)


def _dot_t0(a, b):
    return lax.dot_general(a, b, (((0,), (0,)), ((), ())), preferred_element_type=F32)


def _dot_t1(a, b):
    return lax.dot_general(a, b, (((1,), (1,)), ((), ())), preferred_element_type=F32)


def _rmsnorm(x, gain):
    return x * lax.rsqrt(jnp.mean(x * x, axis=-1, keepdims=True) + EPS) * gain


def _sigmoid(x):
    return jax.nn.sigmoid(x)


def _const_spec(shape):
    nd = len(shape)
    return pl.BlockSpec(shape, lambda *_: (0,) * nd, pipeline_mode=pl.Buffered(1))


def _prep_kernel(a_re_ref, a_im_ref, log_dt_ref, bt_re_ref, bt_im_ref, lbl_ref,
                 lam_re_ref, lam_im_ref, bbt_re_ref, bbt_im_ref, lb_ref, *, layer):
    a_re = a_re_ref[...]
    a_im = a_im_ref[...]
    dt = jnp.exp(log_dt_ref[...])
    mag = jnp.exp(a_re * dt)
    ang = a_im * dt
    lb_re = mag * jnp.cos(ang)
    lb_im = mag * jnp.sin(ang)
    den = a_re * a_re + a_im * a_im
    n_re = lb_re - 1.0
    n_im = lb_im
    co_re = (n_re * a_re + n_im * a_im) / den
    co_im = (n_im * a_re - n_re * a_im) / den
    lam_re_ref[...] = lb_re
    lam_im_ref[...] = lb_im
    bt_re = bt_re_ref[...]
    bt_im = bt_im_ref[...]
    bbt_re_ref[...] = co_re[:, None, :] * bt_re - co_im[:, None, :] * bt_im
    bbt_im_ref[...] = co_re[:, None, :] * bt_im + co_im[:, None, :] * bt_re
    logits = lbl_ref[...]
    m = jnp.max(logits, axis=0, keepdims=True)
    e = jnp.exp(logits - m)
    tot = jnp.sum(e, axis=0, keepdims=True)
    lb_ref[...] = jnp.sum(e[: layer + 1], axis=0, keepdims=True) / tot


def _inproj_kernel(x_ref, g_ref, wa_ref, wh_ref, wg_ref, za_ref, zh_ref, zg_ref):
    u = _rmsnorm(x_ref[...], g_ref[...]).astype(BF16)
    za_ref[...] = _dot(u, wa_ref[...])
    zh_ref[...] = _dot(u, wh_ref[...])
    zg_ref[...] = _dot(u, wg_ref[...])


def _s5_kernel(za_ref, perm_ref, permt_ref, lam_re_ref, lam_im_ref, bk_ref, ck_ref, d_ref,
               wglu_ref, bglu_ref, out_ref, bu_scr, st_scr, *, tt):
    rows = SUBLANES * tt
    n_tiles = bk_ref.shape[0]

    @pl.when(pl.program_id(1) == 0)
    def _():
        st_scr[...] = jnp.zeros_like(st_scr)

    za = za_ref[...].reshape(rows, n_tiles * LANES)
    hi = za.astype(BF16)
    lo = (za - hi.astype(F32)).astype(BF16)
    perm = perm_ref[...]
    zp_hi = _dot(perm, hi)
    zp = zp_hi + _dot(perm, lo)
    zp_bf = zp_hi.astype(BF16)

    for k in range(n_tiles):
        bu_scr[:, k * 2 * STATE_TILE:(k + 1) * 2 * STATE_TILE] = _dot(
            zp_bf[:, k * LANES:(k + 1) * LANES], bk_ref[k])

    for k in range(n_tiles):
        lr = lam_re_ref[k]
        li = lam_im_ref[k]
        c_re = k * 2 * STATE_TILE
        c_im = c_re + STATE_TILE

        def step(t, carry, lr=lr, li=li, c_re=c_re, c_im=c_im):
            xr, xi = carry
            r = pl.ds(pl.multiple_of(t * SUBLANES, SUBLANES), SUBLANES)
            nr = lr * xr - li * xi + bu_scr[r, c_re:c_re + STATE_TILE]
            ni = lr * xi + li * xr + bu_scr[r, c_im:c_im + STATE_TILE]
            bu_scr[r, c_re:c_re + STATE_TILE] = nr
            bu_scr[r, c_im:c_im + STATE_TILE] = ni
            return nr, ni

        xr, xi = lax.fori_loop(0, tt, step, (st_scr[k, 0], st_scr[k, 1]), unroll=4)
        st_scr[k, 0] = xr
        st_scr[k, 1] = xi

    ys = []
    for k in range(n_tiles):
        xk = bu_scr[:, k * 2 * STATE_TILE:(k + 1) * 2 * STATE_TILE].astype(BF16)
        ys.append(_dot(xk, ck_ref[k]))
    y = jnp.concatenate(ys, axis=-1) + d_ref[...] * zp
    ya = jax.nn.gelu(y)
    glu = ya * _sigmoid(_dot(ya.astype(BF16), wglu_ref[...]) + bglu_ref[...])
    out = _dot(permt_ref[...], glu.astype(BF16))
    out_ref[...] = out.astype(BF16).reshape(out_ref.shape)


def _hgrn_kernel(q_ref, f_ref, i_ref, g_ref, lb_ref, gain_ref, tri_ref, out_ref, st_scr, *, chunk):
    n_heads = st_scr.shape[0]

    @pl.when(pl.program_id(1) == 0)
    def _():
        st_scr[...] = jnp.zeros_like(st_scr)

    lb = lb_ref[...]
    f = lb + (1.0 - lb) * _sigmoid(f_ref[0])
    lf = jnp.log(f)
    kk = 1.0 - f
    hi = lf.astype(BF16)
    lo = (lf - hi.astype(F32)).astype(BF16)
    tri = tri_ref[...]
    bcum = _dot(tri, hi) + _dot(tri, lo)
    q = q_ref[0]
    qa = q * _sigmoid(q) * (HG_HEAD_DIM ** -0.5)
    qh = (qa * jnp.exp(bcum)).astype(BF16)
    kh = (kk * jnp.exp(-bcum)).astype(BF16)
    b_last = bcum[chunk - 1:chunk, :]
    ke = (kk * jnp.exp(b_last - bcum)).astype(BF16)
    dec = jnp.exp(b_last)
    iv = i_ref[0].astype(BF16)
    row = lax.broadcasted_iota(jnp.int32, (chunk, chunk), 0)
    col = lax.broadcasted_iota(jnp.int32, (chunk, chunk), 1)
    causal = row >= col
    gain = gain_ref[...]
    outs = []
    for h in range(n_heads):
        sl = slice(h * HG_HEAD_DIM, (h + 1) * HG_HEAD_DIM)
        st = st_scr[h]
        o = _dot_t1(qh[:, sl], st.astype(BF16))
        sc = jnp.where(causal, _dot_t1(qh[:, sl], kh[:, sl]), 0.0)
        o = o + _dot(sc.astype(BF16), iv[:, sl])
        st_scr[h] = st * dec[:, sl] + _dot_t0(iv[:, sl], ke[:, sl])
        outs.append(_rmsnorm(o, gain[:, sl]))
    o = jnp.concatenate(outs, axis=-1)
    g = g_ref[0]
    out_ref[0] = (o * (g * _sigmoid(g))).astype(BF16)


def _merge_kernel(ya_ref, yb_ref, zga_ref, zgb_ref, x_ref, wpa_ref, wpb_ref, wout_ref, out_ref):
    m = (_sigmoid(zga_ref[...]) * _dot(ya_ref[...], wpa_ref[...])
         + _sigmoid(zgb_ref[...]) * _dot(yb_ref[...], wpb_ref[...]))
    out_ref[...] = x_ref[...] + _dot(m.astype(BF16), wout_ref[...])


def _ffn_kernel(x_ref, gf_ref, wup_ref, wconv_ref, bconv_ref, wdown_ref, gfin_ref, out_ref,
                hs_scr, halo_scr, acc_scr, *, tf):
    n_chunks = wup_ref.shape[0]

    @pl.when(pl.program_id(1) == 0)
    def _():
        halo_scr[...] = jnp.zeros_like(halo_scr)

    x1 = x_ref[0]
    u = _rmsnorm(x1, gf_ref[...]).astype(BF16)
    for c in range(n_chunks):
        h = _dot(u, wup_ref[c])
        hs_scr[0:SUBLANES, :] = halo_scr[c]
        hs_scr[SUBLANES:SUBLANES + tf, :] = h
        wc = wconv_ref[c]
        hc = (hs_scr[SUBLANES - 2:SUBLANES - 2 + tf, :] * wc[0:1]
              + hs_scr[SUBLANES - 1:SUBLANES - 1 + tf, :] * wc[1:2]
              + h * wc[2:3] + bconv_ref[c])
        halo_scr[c] = h[tf - SUBLANES:tf, :]
        gate = hc[:, :FFN_CHUNK]
        act = (gate * _sigmoid(gate) * hc[:, FFN_CHUNK:]).astype(BF16)
        contrib = _dot(act, wdown_ref[c])
        if c == 0:
            acc_scr[...] = contrib
        else:
            acc_scr[...] += contrib
    out_ref[0] = _rmsnorm(x1 + acc_scr[...], gfin_ref[...])


def _block_diag(vals):
    nt, g, a, b = vals.shape
    eye = jnp.eye(g, dtype=vals.dtype)
    return jnp.einsum('kgab,gj->kgajb', vals, eye).reshape(nt, g * a, g * b)


def kernel(x, g_mix, w_in, s5_a_re, s5_a_im, s5_log_dt, s5_b_re, s5_b_im, s5_c_re, s5_c_im, s5_d,
           w_glu, b_glu, hg_lb_logits, hg_norm_gain, w_pa, w_pb, w_out, g_ffn, w_up, w_conv,
           b_conv, w_down, g_final):
    depth = w_in.shape[0]
    assert depth == 1, "kernel is written for a single layer"
    bsz, seq, d_model = x.shape
    s5_width = s5_d.shape[-1]
    hg_width = hg_norm_gain.shape[-1]
    d_ff = w_down.shape[1]
    n_groups = s5_a_re.shape[1]
    n_tiles = s5_width // LANES
    n_heads = hg_width // HG_HEAD_DIM
    n_tok = bsz * seq
    assert bsz % SUBLANES == 0 and s5_width % LANES == 0 and d_ff % FFN_CHUNK == 0
    cparams = functools.partial(pltpu.CompilerParams, vmem_limit_bytes=VMEM_LIMIT)

    lam_re, lam_im, bbt_re, bbt_im, lb = pl.pallas_call(
        functools.partial(_prep_kernel, layer=0),
        out_shape=[jax.ShapeDtypeStruct((n_groups, S5_STATE), F32)] * 2
        + [jax.ShapeDtypeStruct((n_groups, S5_GROUP, S5_STATE), F32)] * 2
        + [jax.ShapeDtypeStruct((1, hg_width), F32)],
        name="prep",
    )(s5_a_re[0], s5_a_im[0], s5_log_dt[0][:, None],
      jnp.swapaxes(s5_b_re[0], 1, 2), jnp.swapaxes(s5_b_im[0], 1, 2), hg_lb_logits)

    def tiles(v):
        return v.reshape((n_tiles, GROUPS_PER_TILE) + v.shape[1:])
    lam_re_t = jnp.broadcast_to(lam_re.reshape(n_tiles, 1, STATE_TILE), (n_tiles, SUBLANES, STATE_TILE))
    lam_im_t = jnp.broadcast_to(lam_im.reshape(n_tiles, 1, STATE_TILE), (n_tiles, SUBLANES, STATE_TILE))
    bk = jnp.concatenate([_block_diag(tiles(bbt_re)), _block_diag(tiles(bbt_im))], axis=-1).astype(BF16)
    c_re_t = jnp.swapaxes(s5_c_re[0], 1, 2)
    c_im_t = jnp.swapaxes(s5_c_im[0], 1, 2)
    ck = jnp.concatenate([_block_diag(tiles(c_re_t)), _block_diag(tiles(-c_im_t))], axis=1).astype(BF16)

    tm = 512
    w_in_b = w_in[0].astype(BF16)
    n_h = 4 * hg_width
    za, zh, zg = pl.pallas_call(
        _inproj_kernel,
        grid=(n_tok // tm,),
        in_specs=[pl.BlockSpec((tm, d_model), lambda i: (i, 0)),
                  _const_spec((1, d_model)),
                  _const_spec((d_model, s5_width)),
                  _const_spec((d_model, n_h)),
                  _const_spec((d_model, 2 * d_model))],
        out_specs=[pl.BlockSpec((tm, s5_width), lambda i: (i, 0)),
                   pl.BlockSpec((tm, n_h), lambda i: (i, 0)),
                   pl.BlockSpec((tm, 2 * d_model), lambda i: (i, 0))],
        out_shape=[jax.ShapeDtypeStruct((n_tok, s5_width), F32),
                   jax.ShapeDtypeStruct((n_tok, n_h), F32),
                   jax.ShapeDtypeStruct((n_tok, 2 * d_model), F32)],
        compiler_params=cparams(dimension_semantics=("arbitrary",)),
        name="in_proj",
    )(x.reshape(n_tok, d_model), g_mix[0][None], w_in_b[:, :s5_width],
      w_in_b[:, s5_width:s5_width + n_h], w_in_b[:, s5_width + n_h:])

    tt = 64
    rows = SUBLANES * tt
    r_out = jnp.arange(rows)
    perm =((r_out[:, None] // SUBLANES == r_out[None, :] % tt)
            & (r_out[:, None] % SUBLANES == r_out[None, :] // tt)).astype(BF16)
    ya = pl.pallas_call(
        functools.partial(_s5_kernel, tt=tt),
        grid=(bsz // SUBLANES, seq // tt),
        in_specs=[pl.BlockSpec((SUBLANES, tt, s5_width), lambda b, t: (b, t, 0)),
                  _const_spec((rows, rows)), _const_spec((rows, rows)),
                  _const_spec((n_tiles, SUBLANES, STATE_TILE)),
                  _const_spec((n_tiles, SUBLANES, STATE_TILE)),
                  _const_spec((n_tiles, LANES, 2 * STATE_TILE)),
                  _const_spec((n_tiles, 2 * STATE_TILE, LANES)),
                  _const_spec((1, s5_width)),
                  _const_spec((s5_width, s5_width)),
                  _const_spec((1, s5_width))],
        out_specs=pl.BlockSpec((SUBLANES, tt, s5_width), lambda b, t: (b, t, 0)),
        out_shape=jax.ShapeDtypeStruct((bsz, seq, s5_width), BF16),
        scratch_shapes=[pltpu.VMEM((rows, n_tiles * 2 * STATE_TILE), F32),
                        pltpu.VMEM((n_tiles, 2, SUBLANES, STATE_TILE), F32)],
        compiler_params=cparams(dimension_semantics=("arbitrary", "arbitrary")),
        name="s5",
    )(za.reshape(bsz, seq, s5_width), perm, perm.T, lam_re_t, lam_im_t, bk, ck,
      s5_d[0][None], w_glu[0].astype(BF16), b_glu[0][None])

    chunk = 64
    tri = (jnp.arange(chunk)[:, None] >= jnp.arange(chunk)[None, :]).astype(BF16)
    zh3 = zh.reshape(bsz, seq, n_h)
    yb = pl.pallas_call(
        functools.partial(_hgrn_kernel, chunk=chunk),
        grid=(bsz, seq // chunk),
        in_specs=[pl.BlockSpec((1, chunk, hg_width), lambda b, t: (b, t, 0)),
                  pl.BlockSpec((1, chunk, hg_width), lambda b, t: (b, t, 1)),
                  pl.BlockSpec((1, chunk, hg_width), lambda b, t: (b, t, 2)),
                  pl.BlockSpec((1, chunk, hg_width), lambda b, t: (b, t, 3)),
                  _const_spec((1, hg_width)), _const_spec((1, hg_width)),
                  _const_spec((chunk, chunk))],
        out_specs=pl.BlockSpec((1, chunk, hg_width), lambda b, t: (b, t, 0)),
        out_shape=jax.ShapeDtypeStruct((bsz, seq, hg_width), BF16),
        scratch_shapes=[pltpu.VMEM((n_heads, HG_HEAD_DIM, HG_HEAD_DIM), F32)],
        compiler_params=cparams(dimension_semantics=("arbitrary", "arbitrary")),
        name="hgrn2",
    )(zh3, zh3, zh3, zh3, lb, hg_norm_gain[0][None], tri)

    x1 = pl.pallas_call(
        _merge_kernel,
        grid=(n_tok // tm,),
        in_specs=[pl.BlockSpec((tm, s5_width), lambda i: (i, 0)),
                  pl.BlockSpec((tm, hg_width), lambda i: (i, 0)),
                  pl.BlockSpec((tm, d_model), lambda i: (i, 0)),
                  pl.BlockSpec((tm, d_model), lambda i: (i, 1)),
                  pl.BlockSpec((tm, d_model), lambda i: (i, 0)),
                  _const_spec((s5_width, d_model)), _const_spec((hg_width, d_model)),
                  _const_spec((d_model, d_model))],
        out_specs=pl.BlockSpec((tm, d_model), lambda i: (i, 0)),
        out_shape=jax.ShapeDtypeStruct((n_tok, d_model), F32),
        compiler_params=cparams(dimension_semantics=("arbitrary",)),
        name="merge",
    )(ya.reshape(n_tok, s5_width), yb.reshape(n_tok, hg_width), zg, zg, x.reshape(n_tok, d_model),
      w_pa[0].astype(BF16), w_pb[0].astype(BF16), w_out[0].astype(BF16))

    tf = 512
    n_chunks = d_ff // FFN_CHUNK

    def pair(v):
        lead = v.shape[:-1]
        g, u = v[..., :d_ff], v[..., d_ff:]
        g = jnp.moveaxis(g.reshape(lead + (n_chunks, FFN_CHUNK)), -2, 0)
        u = jnp.moveaxis(u.reshape(lead + (n_chunks, FFN_CHUNK)), -2, 0)
        return jnp.concatenate([g, u], axis=-1)
    out = pl.pallas_call(
        functools.partial(_ffn_kernel, tf=tf),
        grid=(bsz, seq // tf),
        in_specs=[pl.BlockSpec((1, tf, d_model), lambda b, t: (b, t, 0)),
                  _const_spec((1, d_model)),
                  _const_spec((n_chunks, d_model, 2 * FFN_CHUNK)),
                  _const_spec((n_chunks, CONV_W, 2 * FFN_CHUNK)),
                  _const_spec((n_chunks, 1, 2 * FFN_CHUNK)),
                  _const_spec((n_chunks, FFN_CHUNK, d_model)),
                  _const_spec((1, d_model))],
        out_specs=pl.BlockSpec((1, tf, d_model), lambda b, t: (b, t, 0)),
        out_shape=jax.ShapeDtypeStruct((bsz, seq, d_model), F32),
        scratch_shapes=[pltpu.VMEM((SUBLANES + tf, 2 * FFN_CHUNK), F32),
                        pltpu.VMEM((n_chunks, SUBLANES, 2 * FFN_CHUNK), F32),
                        pltpu.VMEM((tf, d_model), F32)],
        compiler_params=cparams(dimension_semantics=("arbitrary", "arbitrary")),
        name="ffn",
    )(x1.reshape(bsz, seq, d_model), g_ffn[0][None], pair(w_up[0]).astype(BF16), pair(w_conv[0]),
      pair(b_conv[0][None]), w_down[0].reshape(n_chunks, FFN_CHUNK, d_model).astype(BF16),
      g_final[None])
    return out
```

```python
import functools

import jax
import jax.numpy as jnp
from jax import lax
from jax.experimental import pallas as pl
from jax.experimental.pallas import tpu as pltpu

EPS = 1e-6
S5_GROUP = 16
S5_STATE = 64
HG_HEAD_DIM = 128
CONV_W = 3

LANES = 128
SEQS = 8
TT = 64
ROWS = SEQS * TT
GROUPS_PER_TILE = LANES // S5_GROUP
STATE_TILE = GROUPS_PER_TILE * S5_STATE
FFN_CHUNK = 256
VMEM_LIMIT = 56 * 1024 * 1024

F32 = jnp.float32
BF16 = jnp.bfloat16


def _dot(a, b):
    return jnp.dot(a, b, preferred_element_type=F32)


def _dot_t0(a, b):
    return lax.dot_general(a, b, (((0,), (0,)), ((), ())), preferred_element_type=F32)


def _dot_t1(a, b):
    return lax.dot_general(a, b, (((1,), (1,)), ((), ())), preferred_element_type=F32)


def _rmsnorm(x, gain):
    return x * lax.rsqrt(jnp.mean(x * x, axis=-1, keepdims=True) + EPS) * gain


def _sigmoid(x):
    return jax.nn.sigmoid(x)


def _const_spec(shape):
    nd = len(shape)
    return pl.BlockSpec(shape, lambda *_: (0,) * nd, pipeline_mode=pl.Buffered(1))


def _seq_spec(width):
    return pl.BlockSpec((SEQS, TT, width), lambda g, t: (g, t, 0))


def _interleave(seq_ref, slab_scr):
    n_slabs = slab_scr.shape[0]
    for b in range(SEQS):
        for s in range(n_slabs):
            slab_scr[s, pl.ds(b, TT, stride=SEQS), :] = seq_ref[b, :, s * LANES:(s + 1) * LANES]
    return jnp.concatenate([slab_scr[s] for s in range(n_slabs)], axis=-1)


def _deinterleave(val, slab_scr, seq_ref):
    n_slabs = slab_scr.shape[0]
    for s in range(n_slabs):
        slab_scr[s] = val[:, s * LANES:(s + 1) * LANES]
    for b in range(SEQS):
        seq_ref[b] = jnp.concatenate(
            [slab_scr[s, pl.ds(b, TT, stride=SEQS), :] for s in range(n_slabs)], axis=-1)


def _prep_kernel(a_re_ref, a_im_ref, log_dt_ref, bt_re_ref, bt_im_ref, lbl_ref,
                 lam_re_ref, lam_im_ref, bbt_re_ref, bbt_im_ref, lb_ref, *, layer):
    a_re = a_re_ref[...]
    a_im = a_im_ref[...]
    dt = jnp.exp(log_dt_ref[...])
    mag = jnp.exp(a_re * dt)
    ang = a_im * dt
    lb_re = mag * jnp.cos(ang)
    lb_im = mag * jnp.sin(ang)
    den = a_re * a_re + a_im * a_im
    n_re = lb_re - 1.0
    n_im = lb_im
    co_re = (n_re * a_re + n_im * a_im) / den
    co_im = (n_im * a_re - n_re * a_im) / den
    lam_re_ref[...] = lb_re
    lam_im_ref[...] = lb_im
    bt_re = bt_re_ref[...]
    bt_im = bt_im_ref[...]
    bbt_re_ref[...] = co_re[:, None, :] * bt_re - co_im[:, None, :] * bt_im
    bbt_im_ref[...] = co_re[:, None, :] * bt_im + co_im[:, None, :] * bt_re
    logits = lbl_ref[...]
    m = jnp.max(logits, axis=0, keepdims=True)
    e = jnp.exp(logits - m)
    tot = jnp.sum(e, axis=0, keepdims=True)
    lb_ref[...] = jnp.sum(e[: layer + 1], axis=0, keepdims=True) / tot


def _s5_branch(za, lam_re_ref, lam_im_ref, bk_ref, ck_ref, d_ref, wglu_ref, bglu_ref, bu_scr, st_scr):
    n_tiles = bk_ref.shape[0]
    za_bf = za.astype(BF16)
    for k in range(n_tiles):
        bu_scr[:, k * 2 * STATE_TILE:(k + 1) * 2 * STATE_TILE] = _dot(
            za_bf[:, k * LANES:(k + 1) * LANES], bk_ref[k])
    for k in range(n_tiles):
        lr = lam_re_ref[k]
        li = lam_im_ref[k]
        c_re = k * 2 * STATE_TILE
        c_im = c_re + STATE_TILE

        def step(t, carry, lr=lr, li=li, c_re=c_re, c_im=c_im):
            xr, xi = carry
            r = pl.ds(pl.multiple_of(t * SEQS, SEQS), SEQS)
            nr = lr * xr - li * xi + bu_scr[r, c_re:c_re + STATE_TILE]
            ni = lr * xi + li * xr + bu_scr[r, c_im:c_im + STATE_TILE]
            bu_scr[r, c_re:c_re + STATE_TILE] = nr
            bu_scr[r, c_im:c_im + STATE_TILE] = ni
            return nr, ni

        xr, xi = lax.fori_loop(0, TT, step, (st_scr[k, 0], st_scr[k, 1]), unroll=4)
        st_scr[k, 0] = xr
        st_scr[k, 1] = xi
    ys = []
    for k in range(n_tiles):
        xk = bu_scr[:, k * 2 * STATE_TILE:(k + 1) * 2 * STATE_TILE].astype(BF16)
        ys.append(_dot(xk, ck_ref[k]))
    y = jnp.concatenate(ys, axis=-1) + d_ref[...] * za
    ya = jax.nn.gelu(y)
    return ya * _sigmoid(_dot(ya.astype(BF16), wglu_ref[...]) + bglu_ref[...])


def _hgrn_branch(zq, zf, zi, zg, lb_ref, gain_ref, hst_scr):
    n_heads = hst_scr.shape[0]
    width = zq.shape[-1]
    lb = lb_ref[...]
    f = lb + (1.0 - lb) * _sigmoid(zf)
    lf = jnp.log(f)
    kk = 1.0 - f
    acc = lf[0:SEQS]
    parts = [acc]
    for t in range(1, TT):
        acc = acc + lf[t * SEQS:(t + 1) * SEQS]
        parts.append(acc)
    bcum = jnp.concatenate(parts, axis=0)
    b_last = parts[-1]
    qa = zq * _sigmoid(zq) * (HG_HEAD_DIM ** -0.5)
    qh = qa * jnp.exp(bcum)
    kh = (kk * jnp.exp(-bcum)).astype(BF16)
    ke = (kk.reshape(TT, SEQS, width) * jnp.exp(b_last[None] - bcum.reshape(TT, SEQS, width))
          ).reshape(ROWS, width)
    dec = jnp.exp(b_last)
    iv = zi.astype(BF16)
    dlt = (lax.broadcasted_iota(jnp.int32, (ROWS, ROWS), 0)
           - lax.broadcasted_iota(jnp.int32, (ROWS, ROWS), 1))
    causal = (dlt >= 0) & ((dlt & (SEQS - 1)) == 0)
    seq_of_row = lax.broadcasted_iota(jnp.int32, (ROWS, HG_HEAD_DIM), 0) & (SEQS - 1)
    gain = gain_ref[...]
    outs = []
    for h in range(n_heads):
        sl = slice(h * HG_HEAD_DIM, (h + 1) * HG_HEAD_DIM)
        q_h = qh[:, sl]
        k_h = ke[:, sl]
        q_aug = jnp.concatenate([jnp.where(seq_of_row == b, q_h, 0.0) for b in range(SEQS)],
                                axis=-1).astype(BF16)
        k_aug = jnp.concatenate([jnp.where(seq_of_row == b, k_h, 0.0) for b in range(SEQS)],
                                axis=-1).astype(BF16)
        st = hst_scr[h]
        sc = jnp.where(causal, _dot_t1(q_h.astype(BF16), kh[:, sl]), 0.0)
        o = _dot(sc.astype(BF16), iv[:, sl]) + _dot_t1(q_aug, st.astype(BF16))
        dec_row = jnp.concatenate([dec[b:b + 1, sl] for b in range(SEQS)], axis=-1)
        hst_scr[h] = st * dec_row + _dot_t0(iv[:, sl], k_aug)
        outs.append(_rmsnorm(o, gain[:, sl]))
    o = jnp.concatenate(outs, axis=-1)
    return o * (zg * _sigmoid(zg))


def _mixer_kernel(x_ref, gmix_ref, win_ref, lam_re_ref, lam_im_ref, bk_ref, ck_ref, d_ref, wglu_ref,
                  bglu_ref, lb_ref, gain_ref, wpa_ref, wpb_ref, wout_ref, out_ref,
                  slab_scr, bu_scr, sst_scr, hst_scr, *, widths):
    s5_w, hg_w, d_model = widths

    @pl.when(pl.program_id(1) == 0)
    def _():
        sst_scr[...] = jnp.zeros_like(sst_scr)
        hst_scr[...] = jnp.zeros_like(hst_scr)

    x = _interleave(x_ref, slab_scr)
    u = _rmsnorm(x, gmix_ref[...]).astype(BF16)

    def proj(lo, width):
        return _dot(u, win_ref[:, lo:lo + width])

    ya = _s5_branch(proj(0, s5_w), lam_re_ref, lam_im_ref, bk_ref, ck_ref, d_ref, wglu_ref,
                    bglu_ref, bu_scr, sst_scr)
    o = s5_w
    yb = _hgrn_branch(proj(o, hg_w), proj(o + hg_w, hg_w), proj(o + 2 * hg_w, hg_w),
                      proj(o + 3 * hg_w, hg_w), lb_ref, gain_ref, hst_scr)
    o += 4 * hg_w
    m = (_sigmoid(proj(o, d_model)) * _dot(ya.astype(BF16), wpa_ref[...])
         + _sigmoid(proj(o + d_model, d_model)) * _dot(yb.astype(BF16), wpb_ref[...]))
    out_ref[0] = x + _dot(m.astype(BF16), wout_ref[...])


def _ffn_kernel(x_ref, gf_ref, wup_ref, wconv_ref, bconv_ref, wdown_ref, gfin_ref, out_ref,
                halo_scr, slab_scr):
    n_chunks = wup_ref.shape[0]

    @pl.when(pl.program_id(1) == 0)
    def _():
        halo_scr[...] = jnp.zeros_like(halo_scr)

    x1 = x_ref[0]
    u = _rmsnorm(x1, gf_ref[...]).astype(BF16)
    acc = None
    for c in range(n_chunks):
        h = _dot(u, wup_ref[c])
        halo = halo_scr[c]
        h1 = jnp.concatenate([halo[SEQS:], h[:ROWS - SEQS]], axis=0)
        h2 = jnp.concatenate([halo, h[:ROWS - 2 * SEQS]], axis=0)
        halo_scr[c] = h[ROWS - 2 * SEQS:]
        wc = wconv_ref[c]
        hc = h2 * wc[0:1] + h1 * wc[1:2] + h * wc[2:3] + bconv_ref[c]
        gate = hc[:, :FFN_CHUNK]
        act = (gate * _sigmoid(gate) * hc[:, FFN_CHUNK:]).astype(BF16)
        contrib = _dot(act, wdown_ref[c])
        acc = contrib if acc is None else acc + contrib
    _deinterleave(_rmsnorm(x1 + acc, gfin_ref[...]), slab_scr, out_ref)


def _block_diag(vals):
    nt, g, a, b = vals.shape
    eye = jnp.eye(g, dtype=vals.dtype)
    return jnp.einsum('kgab,gj->kgajb', vals, eye).reshape(nt, g * a, g * b)


def kernel(x, g_mix, w_in, s5_a_re, s5_a_im, s5_log_dt, s5_b_re, s5_b_im, s5_c_re, s5_c_im, s5_d,
           w_glu, b_glu, hg_lb_logits, hg_norm_gain, w_pa, w_pb, w_out, g_ffn, w_up, w_conv,
           b_conv, w_down, g_final):
    depth = w_in.shape[0]
    assert depth == 1, "kernel is written for a single layer"
    bsz, seq, d_model = x.shape
    s5_w = s5_d.shape[-1]
    hg_w = hg_norm_gain.shape[-1]
    d_ff = w_down.shape[1]
    n_in = w_in.shape[-1]
    n_groups = s5_a_re.shape[1]
    n_tiles = s5_w // LANES
    n_heads = hg_w // HG_HEAD_DIM
    n_chunks = d_ff // FFN_CHUNK
    assert bsz % SEQS == 0 and seq % TT == 0 and s5_w % LANES == 0 and d_ff % FFN_CHUNK == 0
    assert d_model % LANES == 0 and n_in == s5_w + 4 * hg_w + 2 * d_model
    grid = (bsz // SEQS, seq // TT)
    cparams = pltpu.CompilerParams(vmem_limit_bytes=VMEM_LIMIT,
                                   dimension_semantics=("arbitrary", "arbitrary"))

    lam_re, lam_im, bbt_re, bbt_im, lb = pl.pallas_call(
        functools.partial(_prep_kernel, layer=0),
        out_shape=[jax.ShapeDtypeStruct((n_groups, S5_STATE), F32)] * 2
        + [jax.ShapeDtypeStruct((n_groups, S5_GROUP, S5_STATE), F32)] * 2
        + [jax.ShapeDtypeStruct((1, hg_w), F32)],
        name="prep",
    )(s5_a_re[0], s5_a_im[0], s5_log_dt[0][:, None],
      jnp.swapaxes(s5_b_re[0], 1, 2), jnp.swapaxes(s5_b_im[0], 1, 2), hg_lb_logits)

    def tiles(v):
        return v.reshape((n_tiles, GROUPS_PER_TILE) + v.shape[1:])
    lam_re_t = jnp.broadcast_to(lam_re.reshape(n_tiles, 1, STATE_TILE), (n_tiles, SEQS, STATE_TILE))
    lam_im_t = jnp.broadcast_to(lam_im.reshape(n_tiles, 1, STATE_TILE), (n_tiles, SEQS, STATE_TILE))
    bk = jnp.concatenate([_block_diag(tiles(bbt_re)), _block_diag(tiles(bbt_im))], axis=-1).astype(BF16)
    c_re_t = jnp.swapaxes(s5_c_re[0], 1, 2)
    c_im_t = jnp.swapaxes(s5_c_im[0], 1, 2)
    ck = jnp.concatenate([_block_diag(tiles(c_re_t)), _block_diag(tiles(-c_im_t))], axis=1).astype(BF16)

    x1 = pl.pallas_call(
        functools.partial(_mixer_kernel, widths=(s5_w, hg_w, d_model)),
        grid=grid,
        in_specs=[
            _seq_spec(d_model),
            _const_spec((1, d_model)),
            _const_spec((d_model, n_in)),
            _const_spec((n_tiles, SEQS, STATE_TILE)),
            _const_spec((n_tiles, SEQS, STATE_TILE)),
            _const_spec((n_tiles, LANES, 2 * STATE_TILE)),
            _const_spec((n_tiles, 2 * STATE_TILE, LANES)),
            _const_spec((1, s5_w)),
            _const_spec((s5_w, s5_w)),
            _const_spec((1, s5_w)),
            _const_spec((1, hg_w)),
            _const_spec((1, hg_w)),
            _const_spec((s5_w, d_model)),
            _const_spec((hg_w, d_model)),
            _const_spec((d_model, d_model))],
        out_specs=pl.BlockSpec((1, ROWS, d_model), lambda g, t: (g, t, 0)),
        out_shape=jax.ShapeDtypeStruct((bsz // SEQS, seq * SEQS, d_model), F32),
        scratch_shapes=[pltpu.VMEM((d_model // LANES, ROWS, LANES), F32),
                        pltpu.VMEM((ROWS, n_tiles * 2 * STATE_TILE), F32),
                        pltpu.VMEM((n_tiles, 2, SEQS, STATE_TILE), F32),
                        pltpu.VMEM((n_heads, HG_HEAD_DIM, SEQS * HG_HEAD_DIM), F32)],
        compiler_params=cparams,
        name="mixer",
    )(x, g_mix[0][None], w_in[0].astype(BF16), lam_re_t, lam_im_t, bk, ck,
      s5_d[0][None], w_glu[0].astype(BF16), b_glu[0][None], lb, hg_norm_gain[0][None],
      w_pa[0].astype(BF16), w_pb[0].astype(BF16), w_out[0].astype(BF16))

    def pair(v):
        lead = v.shape[:-1]
        g, u = v[..., :d_ff], v[..., d_ff:]
        g = jnp.moveaxis(g.reshape(lead + (n_chunks, FFN_CHUNK)), -2, 0)
        u = jnp.moveaxis(u.reshape(lead + (n_chunks, FFN_CHUNK)), -2, 0)
        return jnp.concatenate([g, u], axis=-1)
    return pl.pallas_call(
        _ffn_kernel,
        grid=grid,
        in_specs=[pl.BlockSpec((1, ROWS, d_model), lambda g, t: (g, t, 0)),
                  _const_spec((1, d_model)),
                  _const_spec((n_chunks, d_model, 2 * FFN_CHUNK)),
                  _const_spec((n_chunks, CONV_W, 2 * FFN_CHUNK)),
                  _const_spec((n_chunks, 1, 2 * FFN_CHUNK)),
                  _const_spec((n_chunks, FFN_CHUNK, d_model)),
                  _const_spec((1, d_model))],
        out_specs=_seq_spec(d_model),
        out_shape=jax.ShapeDtypeStruct((bsz, seq, d_model), F32),
        scratch_shapes=[pltpu.VMEM((n_chunks, 2 * SEQS, 2 * FFN_CHUNK), F32),
                        pltpu.VMEM((d_model // LANES, ROWS, LANES), F32)],
        compiler_params=cparams,
        name="ffn",
    )(x1, g_ffn[0][None], pair(w_up[0]).astype(BF16), pair(w_conv[0]), pair(b_conv[0][None]),
      w_down[0].reshape(n_chunks, FFN_CHUNK, d_model).astype(BF16), g_final[None])
```

```python
import functools

import jax
import jax.numpy as jnp
from jax import lax
from jax.experimental import pallas as pl
from jax.experimental.pallas import tpu as pltpu

EPS = 1e-6
S5_GROUP = 16
S5_STATE = 64
HG_HEAD_DIM = 128
CONV_W = 3

LANES = 128
SEQS = 8
TT = 64
ROWS = SEQS * TT
GROUPS_PER_TILE = LANES // S5_GROUP
STATE_TILE = GROUPS_PER_TILE * S5_STATE
FFN_CHUNK = 256
VMEM_LIMIT = 56 * 1024 * 1024

F32 = jnp.float32
BF16 = jnp.bfloat16


def _dot(a, b):
    return jnp.dot(a, b, preferred_element_type=F32)


def _dot_t0(a, b):
    return lax.dot_general(a, b, (((0,), (0,)), ((), ())), preferred_element_type=F32)


def _dot_t1(a, b):
    return lax.dot_general(a, b, (((1,), (1,)), ((), ())), preferred_element_type=F32)


def _rmsnorm(x, gain):
    return x * lax.rsqrt(jnp.mean(x * x, axis=-1, keepdims=True) + EPS) * gain


def _sigmoid(x):
    return jax.nn.sigmoid(x)


def _const_spec(shape):
    nd = len(shape)
    return pl.BlockSpec(shape, lambda *_: (0,) * nd, pipeline_mode=pl.Buffered(1))


def _seq_spec(width):
    return pl.BlockSpec((SEQS, TT, width), lambda g, t: (g, t, 0))


def _interleave(seq_ref, slab_scr):
    n_slabs = slab_scr.shape[0]
    for b in range(SEQS):
        for s in range(n_slabs):
            slab_scr[s, pl.ds(b, TT, stride=SEQS), :] = seq_ref[b, :, s * LANES:(s + 1) * LANES]
    return jnp.concatenate([slab_scr[s] for s in range(n_slabs)], axis=-1)


def _deinterleave(val, slab_scr, seq_ref):
    n_slabs = slab_scr.shape[0]
    for s in range(n_slabs):
        slab_scr[s] = val[:, s * LANES:(s + 1) * LANES]
    for b in range(SEQS):
        seq_ref[b] = jnp.concatenate(
            [slab_scr[s, pl.ds(b, TT, stride=SEQS), :] for s in range(n_slabs)], axis=-1)


def _prep_kernel(a_re_ref, a_im_ref, log_dt_ref, bt_re_ref, bt_im_ref, lbl_ref,
                 lam_re_ref, lam_im_ref, bbt_re_ref, bbt_im_ref, lb_ref, *, layer):
    a_re = a_re_ref[...]
    a_im = a_im_ref[...]
    dt = jnp.exp(log_dt_ref[...])
    mag = jnp.exp(a_re * dt)
    ang = a_im * dt
    lb_re = mag * jnp.cos(ang)
    lb_im = mag * jnp.sin(ang)
    den = a_re * a_re + a_im * a_im
    n_re = lb_re - 1.0
    n_im = lb_im
    co_re = (n_re * a_re + n_im * a_im) / den
    co_im = (n_im * a_re - n_re * a_im) / den
    lam_re_ref[...] = lb_re
    lam_im_ref[...] = lb_im
    bt_re = bt_re_ref[...]
    bt_im = bt_im_ref[...]
    bbt_re_ref[...] = co_re[:, None, :] * bt_re - co_im[:, None, :] * bt_im
    bbt_im_ref[...] = co_re[:, None, :] * bt_im + co_im[:, None, :] * bt_re
    logits = lbl_ref[...]
    m = jnp.max(logits, axis=0, keepdims=True)
    e = jnp.exp(logits - m)
    tot = jnp.sum(e, axis=0, keepdims=True)
    lb_ref[...] = jnp.sum(e[: layer + 1], axis=0, keepdims=True) / tot


def _s5_branch(za, lam_re_ref, lam_im_ref, bk_ref, ck_ref, d_ref, wglu_ref, bglu_ref, bu_scr, st_scr):
    n_tiles = bk_ref.shape[0]
    za_bf = za.astype(BF16)
    for k in range(n_tiles):
        bu_scr[:, k * 2 * STATE_TILE:(k + 1) * 2 * STATE_TILE] = _dot(
            za_bf[:, k * LANES:(k + 1) * LANES], bk_ref[k])
    for k in range(n_tiles):
        lr = lam_re_ref[k]
        li = lam_im_ref[k]
        c_re = k * 2 * STATE_TILE
        c_im = c_re + STATE_TILE
        xr = st_scr[k, 0]
        xi = st_scr[k, 1]
        for t in range(TT):
            r = slice(t * SEQS, (t + 1) * SEQS)
            nr = lr * xr - li * xi + bu_scr[r, c_re:c_re + STATE_TILE]
            ni = lr * xi + li * xr + bu_scr[r, c_im:c_im + STATE_TILE]
            bu_scr[r, c_re:c_re + STATE_TILE] = nr
            bu_scr[r, c_im:c_im + STATE_TILE] = ni
            xr, xi = nr, ni
        st_scr[k, 0] = xr
        st_scr[k, 1] = xi
    ys = []
    for k in range(n_tiles):
        xk = bu_scr[:, k * 2 * STATE_TILE:(k + 1) * 2 * STATE_TILE].astype(BF16)
        ys.append(_dot(xk, ck_ref[k]))
    y = jnp.concatenate(ys, axis=-1) + d_ref[...] * za
    ya = jax.nn.gelu(y)
    return ya * _sigmoid(_dot(ya.astype(BF16), wglu_ref[...]) + bglu_ref[...])


def _hgrn_branch(zq, zf, zi, zg, lb_ref, gain_ref, hst_scr):
    n_heads = hst_scr.shape[0]
    width = zq.shape[-1]
    lb = lb_ref[...]
    f = lb + (1.0 - lb) * _sigmoid(zf)
    lf = jnp.log(f)
    kk = 1.0 - f
    acc = lf[0:SEQS]
    parts = [acc]
    for t in range(1, TT):
        acc = acc + lf[t * SEQS:(t + 1) * SEQS]
        parts.append(acc)
    bcum = jnp.concatenate(parts, axis=0)
    b_last = parts[-1]
    qa = zq * _sigmoid(zq) * (HG_HEAD_DIM ** -0.5)
    qh = qa * jnp.exp(bcum)
    kh = (kk * jnp.exp(-bcum)).astype(BF16)
    ke = (kk.reshape(TT, SEQS, width) * jnp.exp(b_last[None] - bcum.reshape(TT, SEQS, width))
          ).reshape(ROWS, width)
    dec = jnp.exp(b_last)
    iv = zi.astype(BF16)
    dlt = (lax.broadcasted_iota(jnp.int32, (ROWS, ROWS), 0)
           - lax.broadcasted_iota(jnp.int32, (ROWS, ROWS), 1))
    causal = (dlt >= 0) & ((dlt & (SEQS - 1)) == 0)
    seq_of_row = lax.broadcasted_iota(jnp.int32, (ROWS, HG_HEAD_DIM), 0) & (SEQS - 1)
    gain = gain_ref[...]
    outs = []
    for h in range(n_heads):
        sl = slice(h * HG_HEAD_DIM, (h + 1) * HG_HEAD_DIM)
        q_h = qh[:, sl]
        k_h = ke[:, sl]
        q_aug = jnp.concatenate([jnp.where(seq_of_row == b, q_h, 0.0) for b in range(SEQS)],
                                axis=-1).astype(BF16)
        k_aug = jnp.concatenate([jnp.where(seq_of_row == b, k_h, 0.0) for b in range(SEQS)],
                                axis=-1).astype(BF16)
        st = hst_scr[h]
        sc = jnp.where(causal, _dot_t1(q_h.astype(BF16), kh[:, sl]), 0.0)
        o = _dot(sc.astype(BF16), iv[:, sl]) + _dot_t1(q_aug, st.astype(BF16))
        dec_row = jnp.concatenate([dec[b:b + 1, sl] for b in range(SEQS)], axis=-1)
        hst_scr[h] = st * dec_row + _dot_t0(iv[:, sl], k_aug)
        outs.append(_rmsnorm(o, gain[:, sl]))
    o = jnp.concatenate(outs, axis=-1)
    return o * (zg * _sigmoid(zg))


def _mixer_kernel(x_ref, gmix_ref, win_ref, lam_re_ref, lam_im_ref, bk_ref, ck_ref, d_ref, wglu_ref,
                  bglu_ref, lb_ref, gain_ref, wpa_ref, wpb_ref, wout_ref, out_ref,
                  slab_scr, bu_scr, sst_scr, hst_scr, *, widths):
    s5_w, hg_w, d_model = widths

    @pl.when(pl.program_id(1) == 0)
    def _():
        sst_scr[...] = jnp.zeros_like(sst_scr)
        hst_scr[...] = jnp.zeros_like(hst_scr)

    x = _interleave(x_ref, slab_scr)
    u = _rmsnorm(x, gmix_ref[...]).astype(BF16)

    def proj(lo, width):
        return _dot(u, win_ref[:, lo:lo + width])

    ya = _s5_branch(proj(0, s5_w), lam_re_ref, lam_im_ref, bk_ref, ck_ref, d_ref, wglu_ref,
                    bglu_ref, bu_scr, sst_scr)
    o = s5_w
    yb = _hgrn_branch(proj(o, hg_w), proj(o + hg_w, hg_w), proj(o + 2 * hg_w, hg_w),
                      proj(o + 3 * hg_w, hg_w), lb_ref, gain_ref, hst_scr)
    o += 4 * hg_w
    m = (_sigmoid(proj(o, d_model)) * _dot(ya.astype(BF16), wpa_ref[...])
         + _sigmoid(proj(o + d_model, d_model)) * _dot(yb.astype(BF16), wpb_ref[...]))
    out_ref[0] = x + _dot(m.astype(BF16), wout_ref[...])


def _ffn_kernel(x_ref, gf_ref, wup_ref, wconv_ref, bconv_ref, wdown_ref, gfin_ref, out_ref,
                halo_scr, slab_scr, act_scr):
    n_chunks = wup_ref.shape[0]

    @pl.when(pl.program_id(1) == 0)
    def _():
        halo_scr[...] = jnp.zeros_like(halo_scr)

    x1 = x_ref[0]
    u = _rmsnorm(x1, gf_ref[...]).astype(BF16)
    for c in range(n_chunks):
        h = _dot(u, wup_ref[c])
        halo = halo_scr[c]
        h1 = jnp.concatenate([halo[SEQS:], h[:ROWS - SEQS]], axis=0)
        h2 = jnp.concatenate([halo, h[:ROWS - 2 * SEQS]], axis=0)
        halo_scr[c] = h[ROWS - 2 * SEQS:]
        wc = wconv_ref[c]
        hc = h2 * wc[0:1] + h1 * wc[1:2] + h * wc[2:3] + bconv_ref[c]
        gate = hc[:, :FFN_CHUNK]
        act_scr[:, c * FFN_CHUNK:(c + 1) * FFN_CHUNK] = (
            gate * _sigmoid(gate) * hc[:, FFN_CHUNK:]).astype(BF16)
    y = x1 + _dot(act_scr[...], wdown_ref[...])
    _deinterleave(_rmsnorm(y, gfin_ref[...]), slab_scr, out_ref)


def _block_diag(vals):
    nt, g, a, b = vals.shape
    eye = jnp.eye(g, dtype=vals.dtype)
    return jnp.einsum('kgab,gj->kgajb', vals, eye).reshape(nt, g * a, g * b)


def kernel(x, g_mix, w_in, s5_a_re, s5_a_im, s5_log_dt, s5_b_re, s5_b_im, s5_c_re, s5_c_im, s5_d,
           w_glu, b_glu, hg_lb_logits, hg_norm_gain, w_pa, w_pb, w_out, g_ffn, w_up, w_conv,
           b_conv, w_down, g_final):
    depth = w_in.shape[0]
    assert depth == 1, "kernel is written for a single layer"
    bsz, seq, d_model = x.shape
    s5_w = s5_d.shape[-1]
    hg_w = hg_norm_gain.shape[-1]
    d_ff = w_down.shape[1]
    n_in = w_in.shape[-1]
    n_groups = s5_a_re.shape[1]
    n_tiles = s5_w // LANES
    n_heads = hg_w // HG_HEAD_DIM
    n_chunks = d_ff // FFN_CHUNK
    assert bsz % SEQS == 0 and seq % TT == 0 and s5_w % LANES == 0 and d_ff % FFN_CHUNK == 0
    assert d_model % LANES == 0 and n_in == s5_w + 4 * hg_w + 2 * d_model
    grid = (bsz // SEQS, seq // TT)
    cparams = pltpu.CompilerParams(vmem_limit_bytes=VMEM_LIMIT,
                                   dimension_semantics=("arbitrary", "arbitrary"))

    lam_re, lam_im, bbt_re, bbt_im, lb = pl.pallas_call(
        functools.partial(_prep_kernel, layer=0),
        out_shape=[jax.ShapeDtypeStruct((n_groups, S5_STATE), F32)] * 2
        + [jax.ShapeDtypeStruct((n_groups, S5_GROUP, S5_STATE), F32)] * 2
        + [jax.ShapeDtypeStruct((1, hg_w), F32)],
        name="prep",
    )(s5_a_re[0], s5_a_im[0], s5_log_dt[0][:, None],
      jnp.swapaxes(s5_b_re[0], 1, 2), jnp.swapaxes(s5_b_im[0], 1, 2), hg_lb_logits)

    def tiles(v):
        return v.reshape((n_tiles, GROUPS_PER_TILE) + v.shape[1:])
    lam_re_t = jnp.broadcast_to(lam_re.reshape(n_tiles, 1, STATE_TILE), (n_tiles, SEQS, STATE_TILE))
    lam_im_t = jnp.broadcast_to(lam_im.reshape(n_tiles, 1, STATE_TILE), (n_tiles, SEQS, STATE_TILE))
    bk = jnp.concatenate([_block_diag(tiles(bbt_re)), _block_diag(tiles(bbt_im))], axis=-1).astype(BF16)
    c_re_t = jnp.swapaxes(s5_c_re[0], 1, 2)
    c_im_t = jnp.swapaxes(s5_c_im[0], 1, 2)
    ck = jnp.concatenate([_block_diag(tiles(c_re_t)), _block_diag(tiles(-c_im_t))], axis=1).astype(BF16)

    x1 = pl.pallas_call(
        functools.partial(_mixer_kernel, widths=(s5_w, hg_w, d_model)),
        grid=grid,
        in_specs=[
            _seq_spec(d_model),
            _const_spec((1, d_model)),
            _const_spec((d_model, n_in)),
            _const_spec((n_tiles, SEQS, STATE_TILE)),
            _const_spec((n_tiles, SEQS, STATE_TILE)),
            _const_spec((n_tiles, LANES, 2 * STATE_TILE)),
            _const_spec((n_tiles, 2 * STATE_TILE, LANES)),
            _const_spec((1, s5_w)),
            _const_spec((s5_w, s5_w)),
            _const_spec((1, s5_w)),
            _const_spec((1, hg_w)),
            _const_spec((1, hg_w)),
            _const_spec((s5_w, d_model)),
            _const_spec((hg_w, d_model)),
            _const_spec((d_model, d_model))],
        out_specs=pl.BlockSpec((1, ROWS, d_model), lambda g, t: (g, t, 0)),
        out_shape=jax.ShapeDtypeStruct((bsz // SEQS, seq * SEQS, d_model), F32),
        scratch_shapes=[pltpu.VMEM((d_model // LANES, ROWS, LANES), F32),
                        pltpu.VMEM((ROWS, n_tiles * 2 * STATE_TILE), F32),
                        pltpu.VMEM((n_tiles, 2, SEQS, STATE_TILE), F32),
                        pltpu.VMEM((n_heads, HG_HEAD_DIM, SEQS * HG_HEAD_DIM), F32)],
        compiler_params=cparams,
        name="mixer",
    )(x, g_mix[0][None], w_in[0].astype(BF16), lam_re_t, lam_im_t, bk, ck,
      s5_d[0][None], w_glu[0].astype(BF16), b_glu[0][None], lb, hg_norm_gain[0][None],
      w_pa[0].astype(BF16), w_pb[0].astype(BF16), w_out[0].astype(BF16))

    def pair(v):
        lead = v.shape[:-1]
        g, u = v[..., :d_ff], v[..., d_ff:]
        g = jnp.moveaxis(g.reshape(lead + (n_chunks, FFN_CHUNK)), -2, 0)
        u = jnp.moveaxis(u.reshape(lead + (n_chunks, FFN_CHUNK)), -2, 0)
        return jnp.concatenate([g, u], axis=-1)
    return pl.pallas_call(
        _ffn_kernel,
        grid=grid,
        in_specs=[pl.BlockSpec((1, ROWS, d_model), lambda g, t: (g, t, 0)),
                  _const_spec((1, d_model)),
                  _const_spec((n_chunks, d_model, 2 * FFN_CHUNK)),
                  _const_spec((n_chunks, CONV_W, 2 * FFN_CHUNK)),
                  _const_spec((n_chunks, 1, 2 * FFN_CHUNK)),
                  _const_spec((d_ff, d_model)),
                  _const_spec((1, d_model))],
        out_specs=_seq_spec(d_model),
        out_shape=jax.ShapeDtypeStruct((bsz, seq, d_model), F32),
        scratch_shapes=[pltpu.VMEM((n_chunks, 2 * SEQS, 2 * FFN_CHUNK), F32),
                        pltpu.VMEM((d_model // LANES, ROWS, LANES), F32),
                        pltpu.VMEM((ROWS, d_ff), BF16)],
        compiler_params=cparams,
        name="ffn",
    )(x1, g_ffn[0][None], pair(w_up[0]).astype(BF16), pair(w_conv[0]), pair(b_conv[0][None]),
      w_down[0].astype(BF16), g_final[None])
```

```python
import functools

import jax
import jax.numpy as jnp
from jax import lax
from jax.experimental import pallas as pl
from jax.experimental.pallas import tpu as pltpu

EPS = 1e-6
S5_GROUP = 16
S5_STATE = 64
HG_HEAD_DIM = 128
CONV_W = 3

LANES = 128
SEQS = 8
TT = 64
ROWS = SEQS * TT
GROUPS_PER_TILE = LANES // S5_GROUP
STATE_TILE = GROUPS_PER_TILE * S5_STATE
FFN_CHUNK = 256
FFN_ROW_SPLIT = 2
VMEM_LIMIT = 56 * 1024 * 1024

F32 = jnp.float32
BF16 = jnp.bfloat16


def _dot(a, b):
    return jnp.dot(a, b, preferred_element_type=F32)


def _dot_t0(a, b):
    return lax.dot_general(a, b, (((0,), (0,)), ((), ())), preferred_element_type=F32)


def _dot_t1(a, b):
    return lax.dot_general(a, b, (((1,), (1,)), ((), ())), preferred_element_type=F32)


def _rmsnorm(x, gain):
    return x * lax.rsqrt(jnp.mean(x * x, axis=-1, keepdims=True) + EPS) * gain


def _sigmoid(x):
    return jax.nn.sigmoid(x)


def _const_spec(shape):
    nd = len(shape)
    return pl.BlockSpec(shape, lambda *_: (0,) * nd, pipeline_mode=pl.Buffered(1))


def _seq_spec(width):
    return pl.BlockSpec((SEQS, TT, width), lambda g, t: (g, t, 0))


def _interleave(seq_ref, slab_scr):
    n_slabs = slab_scr.shape[0]
    for b in range(SEQS):
        for s in range(n_slabs):
            slab_scr[s, pl.ds(b, TT, stride=SEQS), :] = seq_ref[b, :, s * LANES:(s + 1) * LANES]
    return jnp.concatenate([slab_scr[s] for s in range(n_slabs)], axis=-1)


def _deinterleave(val, slab_scr, seq_ref, row0=0):
    n_slabs = slab_scr.shape[0]
    rows = val.shape[0]
    for s in range(n_slabs):
        slab_scr[s, row0:row0 + rows, :] = val[:, s * LANES:(s + 1) * LANES]
    for b in range(SEQS):
        seq_ref[b, row0 // SEQS:(row0 + rows) // SEQS, :] = jnp.concatenate(
            [slab_scr[s, pl.ds(row0 + b, rows // SEQS, stride=SEQS), :] for s in range(n_slabs)],
            axis=-1)


def _prep_kernel(a_re_ref, a_im_ref, log_dt_ref, bt_re_ref, bt_im_ref, lbl_ref,
                 lam_re_ref, lam_im_ref, bbt_re_ref, bbt_im_ref, lb_ref, *, layer):
    a_re = a_re_ref[...]
    a_im = a_im_ref[...]
    dt = jnp.exp(log_dt_ref[...])
    mag = jnp.exp(a_re * dt)
    ang = a_im * dt
    lb_re = mag * jnp.cos(ang)
    lb_im = mag * jnp.sin(ang)
    den = a_re * a_re + a_im * a_im
    n_re = lb_re - 1.0
    n_im = lb_im
    co_re = (n_re * a_re + n_im * a_im) / den
    co_im = (n_im * a_re - n_re * a_im) / den
    lam_re_ref[...] = lb_re
    lam_im_ref[...] = lb_im
    bt_re = bt_re_ref[...]
    bt_im = bt_im_ref[...]
    bbt_re_ref[...] = co_re[:, None, :] * bt_re - co_im[:, None, :] * bt_im
    bbt_im_ref[...] = co_re[:, None, :] * bt_im + co_im[:, None, :] * bt_re
    logits = lbl_ref[...]
    m = jnp.max(logits, axis=0, keepdims=True)
    e = jnp.exp(logits - m)
    tot = jnp.sum(e, axis=0, keepdims=True)
    lb_ref[...] = jnp.sum(e[: layer + 1], axis=0, keepdims=True) / tot


def _s5_branch(za, lam_re_ref, lam_im_ref, bk_ref, ck_ref, d_ref, wglu_ref, bglu_ref, bu_scr, st_scr):
    n_tiles = bk_ref.shape[0]
    za_bf = za.astype(BF16)
    for k in range(n_tiles):
        bu_scr[:, k * 2 * STATE_TILE:(k + 1) * 2 * STATE_TILE] = _dot(
            za_bf[:, k * LANES:(k + 1) * LANES], bk_ref[k])
    for k in range(n_tiles):
        lr = lam_re_ref[k]
        li = lam_im_ref[k]
        c_re = k * 2 * STATE_TILE
        c_im = c_re + STATE_TILE
        xr = st_scr[k, 0]
        xi = st_scr[k, 1]
        for t in range(TT):
            r = slice(t * SEQS, (t + 1) * SEQS)
            nr = lr * xr - li * xi + bu_scr[r, c_re:c_re + STATE_TILE]
            ni = lr * xi + li * xr + bu_scr[r, c_im:c_im + STATE_TILE]
            bu_scr[r, c_re:c_re + STATE_TILE] = nr
            bu_scr[r, c_im:c_im + STATE_TILE] = ni
            xr, xi = nr, ni
        st_scr[k, 0] = xr
        st_scr[k, 1] = xi
    ys = []
    for k in range(n_tiles):
        xk = bu_scr[:, k * 2 * STATE_TILE:(k + 1) * 2 * STATE_TILE].astype(BF16)
        ys.append(_dot(xk, ck_ref[k]))
    y = jnp.concatenate(ys, axis=-1) + d_ref[...] * za
    ya = jax.nn.gelu(y)
    return ya * _sigmoid(_dot(ya.astype(BF16), wglu_ref[...]) + bglu_ref[...])


def _hgrn_branch(zq, zf, zi, zg, lb_ref, gain_ref, hst_scr):
    n_heads = hst_scr.shape[0]
    width = zq.shape[-1]
    lb = lb_ref[...]
    f = lb + (1.0 - lb) * _sigmoid(zf)
    lf = jnp.log(f)
    kk = 1.0 - f
    acc = lf[0:SEQS]
    parts = [acc]
    for t in range(1, TT):
        acc = acc + lf[t * SEQS:(t + 1) * SEQS]
        parts.append(acc)
    bcum = jnp.concatenate(parts, axis=0)
    b_last = parts[-1]
    qa = zq * _sigmoid(zq) * (HG_HEAD_DIM ** -0.5)
    qh = qa * jnp.exp(bcum)
    kh = (kk * jnp.exp(-bcum)).astype(BF16)
    ke = (kk.reshape(TT, SEQS, width) * jnp.exp(b_last[None] - bcum.reshape(TT, SEQS, width))
          ).reshape(ROWS, width)
    dec = jnp.exp(b_last)
    iv = zi.astype(BF16)
    dlt = (lax.broadcasted_iota(jnp.int32, (ROWS, ROWS), 0)
           - lax.broadcasted_iota(jnp.int32, (ROWS, ROWS), 1))
    causal = (dlt >= 0) & ((dlt & (SEQS - 1)) == 0)
    seq_of_row = lax.broadcasted_iota(jnp.int32, (ROWS, HG_HEAD_DIM), 0) & (SEQS - 1)
    gain = gain_ref[...]
    outs = []
    for h in range(n_heads):
        sl = slice(h * HG_HEAD_DIM, (h + 1) * HG_HEAD_DIM)
        q_h = qh[:, sl]
        k_h = ke[:, sl]
        q_aug = jnp.concatenate([jnp.where(seq_of_row == b, q_h, 0.0) for b in range(SEQS)],
                                axis=-1).astype(BF16)
        k_aug = jnp.concatenate([jnp.where(seq_of_row == b, k_h, 0.0) for b in range(SEQS)],
                                axis=-1).astype(BF16)
        st = hst_scr[h]
        sc = jnp.where(causal, _dot_t1(q_h.astype(BF16), kh[:, sl]), 0.0)
        o = _dot(sc.astype(BF16), iv[:, sl]) + _dot_t1(q_aug, st.astype(BF16))
        dec_row = jnp.concatenate([dec[b:b + 1, sl] for b in range(SEQS)], axis=-1)
        hst_scr[h] = st * dec_row + _dot_t0(iv[:, sl], k_aug)
        outs.append(_rmsnorm(o, gain[:, sl]))
    o = jnp.concatenate(outs, axis=-1)
    return o * (zg * _sigmoid(zg))


def _mixer_kernel(x_ref, gmix_ref, win_ref, lam_re_ref, lam_im_ref, bk_ref, ck_ref, d_ref, wglu_ref,
                  bglu_ref, lb_ref, gain_ref, wpa_ref, wpb_ref, wout_ref, out_ref,
                  slab_scr, bu_scr, sst_scr, hst_scr, *, widths):
    s5_w, hg_w, d_model = widths

    @pl.when(pl.program_id(1) == 0)
    def _():
        sst_scr[...] = jnp.zeros_like(sst_scr)
        hst_scr[...] = jnp.zeros_like(hst_scr)

    x = _interleave(x_ref, slab_scr)
    u = _rmsnorm(x, gmix_ref[...]).astype(BF16)

    def proj(lo, width):
        return _dot(u, win_ref[:, lo:lo + width])

    ya = _s5_branch(proj(0, s5_w), lam_re_ref, lam_im_ref, bk_ref, ck_ref, d_ref, wglu_ref,
                    bglu_ref, bu_scr, sst_scr)
    o = s5_w
    yb = _hgrn_branch(proj(o, hg_w), proj(o + hg_w, hg_w), proj(o + 2 * hg_w, hg_w),
                      proj(o + 3 * hg_w, hg_w), lb_ref, gain_ref, hst_scr)
    o += 4 * hg_w
    m = (_sigmoid(proj(o, d_model)) * _dot(ya.astype(BF16), wpa_ref[...])
         + _sigmoid(proj(o + d_model, d_model)) * _dot(yb.astype(BF16), wpb_ref[...]))
    out_ref[0] = x + _dot(m.astype(BF16), wout_ref[...])


def _conv_taps(h, prev, w, b):
    rows = h.shape[0]
    h1 = jnp.concatenate([prev[SEQS:], h[:rows - SEQS]], axis=0)
    h2 = jnp.concatenate([prev, h[:rows - 2 * SEQS]], axis=0)
    return h2 * w[0:1] + h1 * w[1:2] + h * w[2:3] + b


def _ffn_kernel(x_ref, gf_ref, wup_ref, wconv_ref, bconv_ref, wdown_ref, gfin_ref, out_ref,
                halo_scr, slab_scr, act_scr):
    d_ff = wdown_ref.shape[0]
    n_chunks = d_ff // FFN_CHUNK
    blk_rows = ROWS // FFN_ROW_SPLIT

    @pl.when(pl.program_id(1) == 0)
    def _():
        halo_scr[...] = jnp.zeros_like(halo_scr)

    prev = [halo_scr[:, j * FFN_CHUNK:(j + 1) * FFN_CHUNK] for j in range(2 * n_chunks)]
    for blk in range(FFN_ROW_SPLIT):
        rs = slice(blk * blk_rows, (blk + 1) * blk_rows)
        x1 = x_ref[0, rs, :]
        u = _rmsnorm(x1, gf_ref[...]).astype(BF16)
        for c in range(n_chunks):
            cols = []
            for part in range(2):
                j = part * n_chunks + c
                cs = slice(j * FFN_CHUNK, (j + 1) * FFN_CHUNK)
                h = _dot(u, wup_ref[:, cs])
                cols.append(_conv_taps(h, prev[j], wconv_ref[:, cs], bconv_ref[:, cs]))
                prev[j] = h[blk_rows - 2 * SEQS:]
            gate, val = cols
            act_scr[rs, c * FFN_CHUNK:(c + 1) * FFN_CHUNK] = (
                gate * _sigmoid(gate) * val).astype(BF16)
        y = x1 + _dot(act_scr[rs, :], wdown_ref[...])
        _deinterleave(_rmsnorm(y, gfin_ref[...]), slab_scr, out_ref, blk * blk_rows)
    for j in range(2 * n_chunks):
        halo_scr[:, j * FFN_CHUNK:(j + 1) * FFN_CHUNK] = prev[j]


def _block_diag(vals):
    nt, g, a, b = vals.shape
    same = jnp.arange(g)[:, None, None, None] == jnp.arange(g)[None, None, :, None]
    return jnp.where(same, vals[:, :, :, None, :], 0).reshape(nt, g * a, g * b)


def kernel(x, g_mix, w_in, s5_a_re, s5_a_im, s5_log_dt, s5_b_re, s5_b_im, s5_c_re, s5_c_im, s5_d,
           w_glu, b_glu, hg_lb_logits, hg_norm_gain, w_pa, w_pb, w_out, g_ffn, w_up, w_conv,
           b_conv, w_down, g_final):
    depth = w_in.shape[0]
    assert depth == 1, "kernel is written for a single layer"
    bsz, seq, d_model = x.shape
    s5_w = s5_d.shape[-1]
    hg_w = hg_norm_gain.shape[-1]
    d_ff = w_down.shape[1]
    n_in = w_in.shape[-1]
    n_groups = s5_a_re.shape[1]
    n_tiles = s5_w // LANES
    n_heads = hg_w // HG_HEAD_DIM
    n_chunks = d_ff // FFN_CHUNK
    assert bsz % SEQS == 0 and seq % TT == 0 and s5_w % LANES == 0 and d_ff % FFN_CHUNK == 0
    assert d_model % LANES == 0 and n_in == s5_w + 4 * hg_w + 2 * d_model
    grid = (bsz // SEQS, seq // TT)
    cparams = pltpu.CompilerParams(vmem_limit_bytes=VMEM_LIMIT,
                                   dimension_semantics=("arbitrary", "arbitrary"))

    lam_re, lam_im, bbt_re, bbt_im, lb = pl.pallas_call(
        functools.partial(_prep_kernel, layer=0),
        out_shape=[jax.ShapeDtypeStruct((n_groups, S5_STATE), F32)] * 2
        + [jax.ShapeDtypeStruct((n_groups, S5_GROUP, S5_STATE), F32)] * 2
        + [jax.ShapeDtypeStruct((1, hg_w), F32)],
        name="prep",
    )(s5_a_re[0], s5_a_im[0], s5_log_dt[0][:, None],
      jnp.swapaxes(s5_b_re[0], 1, 2), jnp.swapaxes(s5_b_im[0], 1, 2), hg_lb_logits)

    def tiles(v):
        return v.reshape((n_tiles, GROUPS_PER_TILE) + v.shape[1:])
    lam_re_t = jnp.broadcast_to(lam_re.reshape(n_tiles, 1, STATE_TILE), (n_tiles, SEQS, STATE_TILE))
    lam_im_t = jnp.broadcast_to(lam_im.reshape(n_tiles, 1, STATE_TILE), (n_tiles, SEQS, STATE_TILE))
    bk = jnp.concatenate([_block_diag(tiles(bbt_re)), _block_diag(tiles(bbt_im))], axis=-1).astype(BF16)
    c_re_t = jnp.swapaxes(s5_c_re[0], 1, 2)
    c_im_t = jnp.swapaxes(s5_c_im[0], 1, 2)
    ck = jnp.concatenate([_block_diag(tiles(c_re_t)), _block_diag(tiles(-c_im_t))], axis=1).astype(BF16)

    x1 = pl.pallas_call(
        functools.partial(_mixer_kernel, widths=(s5_w, hg_w, d_model)),
        grid=grid,
        in_specs=[
            _seq_spec(d_model),
            _const_spec((1, d_model)),
            _const_spec((d_model, n_in)),
            _const_spec((n_tiles, SEQS, STATE_TILE)),
            _const_spec((n_tiles, SEQS, STATE_TILE)),
            _const_spec((n_tiles, LANES, 2 * STATE_TILE)),
            _const_spec((n_tiles, 2 * STATE_TILE, LANES)),
            _const_spec((1, s5_w)),
            _const_spec((s5_w, s5_w)),
            _const_spec((1, s5_w)),
            _const_spec((1, hg_w)),
            _const_spec((1, hg_w)),
            _const_spec((s5_w, d_model)),
            _const_spec((hg_w, d_model)),
            _const_spec((d_model, d_model))],
        out_specs=pl.BlockSpec((1, ROWS, d_model), lambda g, t: (g, t, 0)),
        out_shape=jax.ShapeDtypeStruct((bsz // SEQS, seq * SEQS, d_model), F32),
        scratch_shapes=[pltpu.VMEM((d_model // LANES, ROWS, LANES), F32),
                        pltpu.VMEM((ROWS, n_tiles * 2 * STATE_TILE), F32),
                        pltpu.VMEM((n_tiles, 2, SEQS, STATE_TILE), F32),
                        pltpu.VMEM((n_heads, HG_HEAD_DIM, SEQS * HG_HEAD_DIM), F32)],
        compiler_params=cparams,
        name="mixer",
    )(x, g_mix[0][None], w_in[0].astype(BF16), lam_re_t, lam_im_t, bk, ck,
      s5_d[0][None], w_glu[0].astype(BF16), b_glu[0][None], lb, hg_norm_gain[0][None],
      w_pa[0].astype(BF16), w_pb[0].astype(BF16), w_out[0].astype(BF16))

    return pl.pallas_call(
        _ffn_kernel,
        grid=grid,
        in_specs=[pl.BlockSpec((1, ROWS, d_model), lambda g, t: (g, t, 0)),
                  _const_spec((1, d_model)),
                  _const_spec((d_model, 2 * d_ff)),
                  _const_spec((CONV_W, 2 * d_ff)),
                  _const_spec((1, 2 * d_ff)),
                  _const_spec((d_ff, d_model)),
                  _const_spec((1, d_model))],
        out_specs=_seq_spec(d_model),
        out_shape=jax.ShapeDtypeStruct((bsz, seq, d_model), F32),
        scratch_shapes=[pltpu.VMEM((2 * SEQS, 2 * d_ff), F32),
                        pltpu.VMEM((d_model // LANES, ROWS, LANES), F32),
                        pltpu.VMEM((ROWS, d_ff), BF16)],
        compiler_params=cparams,
        name="ffn",
    )(x1, g_ffn[0][None], w_up[0].astype(BF16), w_conv[0], b_conv[0][None],
      w_down[0].astype(BF16), g_final[None])
```

```python
import functools

import jax
import jax.numpy as jnp
from jax import lax
from jax.experimental import pallas as pl
from jax.experimental.pallas import tpu as pltpu

EPS = 1e-6
S5_GROUP = 16
S5_STATE = 64
HG_HEAD_DIM = 128
CONV_W = 3

LANES = 128
SEQS = 8
TT = 64
ROWS = SEQS * TT
GROUPS_PER_TILE = LANES // S5_GROUP
STATE_TILE = GROUPS_PER_TILE * S5_STATE
FFN_CHUNK = 256
FFN_ROW_SPLIT = 1
VMEM_LIMIT = 56 * 1024 * 1024

F32 = jnp.float32
BF16 = jnp.bfloat16


def _dot(a, b):
    return jnp.dot(a, b, preferred_element_type=F32)


def _dot_t0(a, b):
    return lax.dot_general(a, b, (((0,), (0,)), ((), ())), preferred_element_type=F32)


def _dot_t1(a, b):
    return lax.dot_general(a, b, (((1,), (1,)), ((), ())), preferred_element_type=F32)


def _rmsnorm(x, gain):
    return x * lax.rsqrt(jnp.mean(x * x, axis=-1, keepdims=True) + EPS) * gain


def _sigmoid(x):
    return jax.nn.sigmoid(x)


def _const_spec(shape):
    nd = len(shape)
    return pl.BlockSpec(shape, lambda *_: (0,) * nd, pipeline_mode=pl.Buffered(1))


def _seq_spec(width):
    return pl.BlockSpec((SEQS, TT, width), lambda g, t: (g, t, 0))


def _interleave(seq_ref, slab_scr):
    n_slabs = slab_scr.shape[0]
    for b in range(SEQS):
        for s in range(n_slabs):
            slab_scr[s, pl.ds(b, TT, stride=SEQS), :] = seq_ref[b, :, s * LANES:(s + 1) * LANES]
    return jnp.concatenate([slab_scr[s] for s in range(n_slabs)], axis=-1)


def _deinterleave(val, slab_scr, seq_ref, row0=0):
    n_slabs = slab_scr.shape[0]
    rows = val.shape[0]
    for s in range(n_slabs):
        slab_scr[s, row0:row0 + rows, :] = val[:, s * LANES:(s + 1) * LANES]
    for b in range(SEQS):
        seq_ref[b, row0 // SEQS:(row0 + rows) // SEQS, :] = jnp.concatenate(
            [slab_scr[s, pl.ds(row0 + b, rows // SEQS, stride=SEQS), :] for s in range(n_slabs)],
            axis=-1)


def _prep_kernel(a_re_ref, a_im_ref, log_dt_ref, bt_re_ref, bt_im_ref, lbl_ref,
                 lam_re_ref, lam_im_ref, bbt_re_ref, bbt_im_ref, lb_ref, *, layer):
    a_re = a_re_ref[...]
    a_im = a_im_ref[...]
    dt = jnp.exp(log_dt_ref[...])
    mag = jnp.exp(a_re * dt)
    ang = a_im * dt
    lb_re = mag * jnp.cos(ang)
    lb_im = mag * jnp.sin(ang)
    den = a_re * a_re + a_im * a_im
    n_re = lb_re - 1.0
    n_im = lb_im
    co_re = (n_re * a_re + n_im * a_im) / den
    co_im = (n_im * a_re - n_re * a_im) / den
    lam_re_ref[...] = lb_re
    lam_im_ref[...] = lb_im
    bt_re = bt_re_ref[...]
    bt_im = bt_im_ref[...]
    bbt_re_ref[...] = co_re[:, None, :] * bt_re - co_im[:, None, :] * bt_im
    bbt_im_ref[...] = co_re[:, None, :] * bt_im + co_im[:, None, :] * bt_re
    logits = lbl_ref[...]
    m = jnp.max(logits, axis=0, keepdims=True)
    e = jnp.exp(logits - m)
    tot = jnp.sum(e, axis=0, keepdims=True)
    lb_ref[...] = jnp.sum(e[: layer + 1], axis=0, keepdims=True) / tot


def _s5_branch(za, lam_re_ref, lam_im_ref, bk_ref, ck_ref, d_ref, wglu_ref, bglu_ref, bu_scr, st_scr):
    n_tiles = bk_ref.shape[0]
    za_bf = za.astype(BF16)
    for k in range(n_tiles):
        bu_scr[:, k * 2 * STATE_TILE:(k + 1) * 2 * STATE_TILE] = _dot(
            za_bf[:, k * LANES:(k + 1) * LANES], bk_ref[k])
    for k in range(n_tiles):
        lr = lam_re_ref[k]
        li = lam_im_ref[k]
        c_re = k * 2 * STATE_TILE
        c_im = c_re + STATE_TILE
        xr = st_scr[k, 0]
        xi = st_scr[k, 1]
        for t in range(TT):
            r = slice(t * SEQS, (t + 1) * SEQS)
            nr = lr * xr - li * xi + bu_scr[r, c_re:c_re + STATE_TILE]
            ni = lr * xi + li * xr + bu_scr[r, c_im:c_im + STATE_TILE]
            bu_scr[r, c_re:c_re + STATE_TILE] = nr
            bu_scr[r, c_im:c_im + STATE_TILE] = ni
            xr, xi = nr, ni
        st_scr[k, 0] = xr
        st_scr[k, 1] = xi
    ys = []
    for k in range(n_tiles):
        xk = bu_scr[:, k * 2 * STATE_TILE:(k + 1) * 2 * STATE_TILE].astype(BF16)
        ys.append(_dot(xk, ck_ref[k]))
    y = jnp.concatenate(ys, axis=-1) + d_ref[...] * za
    ya = jax.nn.gelu(y)
    return ya * _sigmoid(_dot(ya.astype(BF16), wglu_ref[...]) + bglu_ref[...])


def _hgrn_branch(zq, zf, zi, zg, lb_ref, gain_ref, hst_scr):
    n_heads = hst_scr.shape[0]
    width = zq.shape[-1]
    lb = lb_ref[...]
    f = lb + (1.0 - lb) * _sigmoid(zf)
    lf = jnp.log(f)
    kk = 1.0 - f
    acc = lf[0:SEQS]
    parts = [acc]
    for t in range(1, TT):
        acc = acc + lf[t * SEQS:(t + 1) * SEQS]
        parts.append(acc)
    bcum = jnp.concatenate(parts, axis=0)
    b_last = parts[-1]
    qa = zq * _sigmoid(zq) * (HG_HEAD_DIM ** -0.5)
    qh = qa * jnp.exp(bcum)
    kh = (kk * jnp.exp(-bcum)).astype(BF16)
    ke = (kk.reshape(TT, SEQS, width) * jnp.exp(b_last[None] - bcum.reshape(TT, SEQS, width))
          ).reshape(ROWS, width)
    dec = jnp.exp(b_last)
    iv = zi.astype(BF16)
    dlt = (lax.broadcasted_iota(jnp.int32, (ROWS, ROWS), 0)
           - lax.broadcasted_iota(jnp.int32, (ROWS, ROWS), 1))
    causal = (dlt >= 0) & ((dlt & (SEQS - 1)) == 0)
    seq_of_row = lax.broadcasted_iota(jnp.int32, (ROWS, HG_HEAD_DIM), 0) & (SEQS - 1)
    gain = gain_ref[...]
    outs = []
    for h in range(n_heads):
        sl = slice(h * HG_HEAD_DIM, (h + 1) * HG_HEAD_DIM)
        q_h = qh[:, sl]
        k_h = ke[:, sl]
        q_aug = jnp.concatenate([jnp.where(seq_of_row == b, q_h, 0.0) for b in range(SEQS)],
                                axis=-1).astype(BF16)
        k_aug = jnp.concatenate([jnp.where(seq_of_row == b, k_h, 0.0) for b in range(SEQS)],
                                axis=-1).astype(BF16)
        st = hst_scr[h]
        sc = jnp.where(causal, _dot_t1(q_h.astype(BF16), kh[:, sl]), 0.0)
        o = _dot(sc.astype(BF16), iv[:, sl]) + _dot_t1(q_aug, st.astype(BF16))
        dec_row = jnp.concatenate([dec[b:b + 1, sl] for b in range(SEQS)], axis=-1)
        hst_scr[h] = st * dec_row + _dot_t0(iv[:, sl], k_aug)
        outs.append(_rmsnorm(o, gain[:, sl]))
    o = jnp.concatenate(outs, axis=-1)
    return o * (zg * _sigmoid(zg))


def _mixer_kernel(x_ref, gmix_ref, win_ref, lam_re_ref, lam_im_ref, bk_ref, ck_ref, d_ref, wglu_ref,
                  bglu_ref, lb_ref, gain_ref, wpa_ref, wpb_ref, wout_ref, out_ref,
                  slab_scr, bu_scr, sst_scr, hst_scr, *, widths):
    s5_w, hg_w, d_model = widths

    @pl.when(pl.program_id(1) == 0)
    def _():
        sst_scr[...] = jnp.zeros_like(sst_scr)
        hst_scr[...] = jnp.zeros_like(hst_scr)

    x = _interleave(x_ref, slab_scr)
    u = _rmsnorm(x, gmix_ref[...]).astype(BF16)

    def proj(lo, width):
        return _dot(u, win_ref[:, lo:lo + width])

    ya = _s5_branch(proj(0, s5_w), lam_re_ref, lam_im_ref, bk_ref, ck_ref, d_ref, wglu_ref,
                    bglu_ref, bu_scr, sst_scr)
    o = s5_w
    yb = _hgrn_branch(proj(o, hg_w), proj(o + hg_w, hg_w), proj(o + 2 * hg_w, hg_w),
                      proj(o + 3 * hg_w, hg_w), lb_ref, gain_ref, hst_scr)
    o += 4 * hg_w
    m = (_sigmoid(proj(o, d_model)) * _dot(ya.astype(BF16), wpa_ref[...])
         + _sigmoid(proj(o + d_model, d_model)) * _dot(yb.astype(BF16), wpb_ref[...]))
    out_ref[0] = x + _dot(m.astype(BF16), wout_ref[...])


def _conv_taps(h, prev, w, b):
    rows = h.shape[0]
    h1 = jnp.concatenate([prev[SEQS:], h[:rows - SEQS]], axis=0)
    h2 = jnp.concatenate([prev, h[:rows - 2 * SEQS]], axis=0)
    return h2 * w[0:1] + h1 * w[1:2] + h * w[2:3] + b


def _ffn_kernel(x_ref, gf_ref, wup_ref, wconv_ref, bconv_ref, wdown_ref, gfin_ref, out_ref,
                halo_scr, slab_scr, act_scr):
    d_ff = wdown_ref.shape[0]
    n_chunks = d_ff // FFN_CHUNK
    blk_rows = ROWS // FFN_ROW_SPLIT

    @pl.when(pl.program_id(1) == 0)
    def _():
        halo_scr[...] = jnp.zeros_like(halo_scr)

    prev = [halo_scr[:, j * FFN_CHUNK:(j + 1) * FFN_CHUNK] for j in range(2 * n_chunks)]
    for blk in range(FFN_ROW_SPLIT):
        rs = slice(blk * blk_rows, (blk + 1) * blk_rows)
        x1 = x_ref[0, rs, :]
        u = _rmsnorm(x1, gf_ref[...]).astype(BF16)
        for c in range(n_chunks):
            cols = []
            for part in range(2):
                j = part * n_chunks + c
                cs = slice(j * FFN_CHUNK, (j + 1) * FFN_CHUNK)
                h = _dot(u, wup_ref[:, cs])
                cols.append(_conv_taps(h, prev[j], wconv_ref[:, cs], bconv_ref[:, cs]))
                prev[j] = h[blk_rows - 2 * SEQS:]
            gate, val = cols
            act_scr[rs, c * FFN_CHUNK:(c + 1) * FFN_CHUNK] = (
                gate * _sigmoid(gate) * val).astype(BF16)
        y = x1 + _dot(act_scr[rs, :], wdown_ref[...])
        _deinterleave(_rmsnorm(y, gfin_ref[...]), slab_scr, out_ref, blk * blk_rows)
    for j in range(2 * n_chunks):
        halo_scr[:, j * FFN_CHUNK:(j + 1) * FFN_CHUNK] = prev[j]


def _block_diag(vals):
    nt, g, a, b = vals.shape
    same = jnp.arange(g)[:, None, None, None] == jnp.arange(g)[None, None, :, None]
    return jnp.where(same, vals[:, :, :, None, :], 0).reshape(nt, g * a, g * b)


def kernel(x, g_mix, w_in, s5_a_re, s5_a_im, s5_log_dt, s5_b_re, s5_b_im, s5_c_re, s5_c_im, s5_d,
           w_glu, b_glu, hg_lb_logits, hg_norm_gain, w_pa, w_pb, w_out, g_ffn, w_up, w_conv,
           b_conv, w_down, g_final):
    depth = w_in.shape[0]
    assert depth == 1, "kernel is written for a single layer"
    bsz, seq, d_model = x.shape
    s5_w = s5_d.shape[-1]
    hg_w = hg_norm_gain.shape[-1]
    d_ff = w_down.shape[1]
    n_in = w_in.shape[-1]
    n_groups = s5_a_re.shape[1]
    n_tiles = s5_w // LANES
    n_heads = hg_w // HG_HEAD_DIM
    n_chunks = d_ff // FFN_CHUNK
    assert bsz % SEQS == 0 and seq % TT == 0 and s5_w % LANES == 0 and d_ff % FFN_CHUNK == 0
    assert d_model % LANES == 0 and n_in == s5_w + 4 * hg_w + 2 * d_model
    grid = (bsz // SEQS, seq // TT)
    cparams = pltpu.CompilerParams(vmem_limit_bytes=VMEM_LIMIT,
                                   dimension_semantics=("arbitrary", "arbitrary"))

    lam_re, lam_im, bbt_re, bbt_im, lb = pl.pallas_call(
        functools.partial(_prep_kernel, layer=0),
        out_shape=[jax.ShapeDtypeStruct((n_groups, S5_STATE), F32)] * 2
        + [jax.ShapeDtypeStruct((n_groups, S5_GROUP, S5_STATE), F32)] * 2
        + [jax.ShapeDtypeStruct((1, hg_w), F32)],
        name="prep",
    )(s5_a_re[0], s5_a_im[0], s5_log_dt[0][:, None],
      jnp.swapaxes(s5_b_re[0], 1, 2), jnp.swapaxes(s5_b_im[0], 1, 2), hg_lb_logits)

    def tiles(v):
        return v.reshape((n_tiles, GROUPS_PER_TILE) + v.shape[1:])
    lam_re_t = jnp.broadcast_to(lam_re.reshape(n_tiles, 1, STATE_TILE), (n_tiles, SEQS, STATE_TILE))
    lam_im_t = jnp.broadcast_to(lam_im.reshape(n_tiles, 1, STATE_TILE), (n_tiles, SEQS, STATE_TILE))
    bk = jnp.concatenate([_block_diag(tiles(bbt_re)), _block_diag(tiles(bbt_im))], axis=-1).astype(BF16)
    c_re_t = jnp.swapaxes(s5_c_re[0], 1, 2)
    c_im_t = jnp.swapaxes(s5_c_im[0], 1, 2)
    ck = jnp.concatenate([_block_diag(tiles(c_re_t)), _block_diag(tiles(-c_im_t))], axis=1).astype(BF16)

    x1 = pl.pallas_call(
        functools.partial(_mixer_kernel, widths=(s5_w, hg_w, d_model)),
        grid=grid,
        in_specs=[
            _seq_spec(d_model),
            _const_spec((1, d_model)),
            _const_spec((d_model, n_in)),
            _const_spec((n_tiles, SEQS, STATE_TILE)),
            _const_spec((n_tiles, SEQS, STATE_TILE)),
            _const_spec((n_tiles, LANES, 2 * STATE_TILE)),
            _const_spec((n_tiles, 2 * STATE_TILE, LANES)),
            _const_spec((1, s5_w)),
            _const_spec((s5_w, s5_w)),
            _const_spec((1, s5_w)),
            _const_spec((1, hg_w)),
            _const_spec((1, hg_w)),
            _const_spec((s5_w, d_model)),
            _const_spec((hg_w, d_model)),
            _const_spec((d_model, d_model))],
        out_specs=pl.BlockSpec((1, ROWS, d_model), lambda g, t: (g, t, 0)),
        out_shape=jax.ShapeDtypeStruct((bsz // SEQS, seq * SEQS, d_model), F32),
        scratch_shapes=[pltpu.VMEM((d_model // LANES, ROWS, LANES), F32),
                        pltpu.VMEM((ROWS, n_tiles * 2 * STATE_TILE), F32),
                        pltpu.VMEM((n_tiles, 2, SEQS, STATE_TILE), F32),
                        pltpu.VMEM((n_heads, HG_HEAD_DIM, SEQS * HG_HEAD_DIM), F32)],
        compiler_params=cparams,
        name="mixer",
    )(x, g_mix[0][None], w_in[0].astype(BF16), lam_re_t, lam_im_t, bk, ck,
      s5_d[0][None], w_glu[0].astype(BF16), b_glu[0][None], lb, hg_norm_gain[0][None],
      w_pa[0].astype(BF16), w_pb[0].astype(BF16), w_out[0].astype(BF16))

    return pl.pallas_call(
        _ffn_kernel,
        grid=grid,
        in_specs=[pl.BlockSpec((1, ROWS, d_model), lambda g, t: (g, t, 0)),
                  _const_spec((1, d_model)),
                  _const_spec((d_model, 2 * d_ff)),
                  _const_spec((CONV_W, 2 * d_ff)),
                  _const_spec((1, 2 * d_ff)),
                  _const_spec((d_ff, d_model)),
                  _const_spec((1, d_model))],
        out_specs=_seq_spec(d_model),
        out_shape=jax.ShapeDtypeStruct((bsz, seq, d_model), F32),
        scratch_shapes=[pltpu.VMEM((2 * SEQS, 2 * d_ff), F32),
                        pltpu.VMEM((d_model // LANES, ROWS, LANES), F32),
                        pltpu.VMEM((ROWS, d_ff), BF16)],
        compiler_params=cparams,
        name="ffn",
    )(x1, g_ffn[0][None], w_up[0].astype(BF16), w_conv[0], b_conv[0][None],
      w_down[0].astype(BF16), g_final[None])
```

```python
import functools

import jax
import jax.numpy as jnp
from jax import lax
from jax.experimental import pallas as pl
from jax.experimental.pallas import tpu as pltpu

EPS = 1e-6
S5_GROUP = 16
S5_STATE = 64
HG_HEAD_DIM = 128
CONV_W = 3

LANES = 128
SEQS = 8
TT = 64
ROWS = SEQS * TT
GROUPS_PER_TILE = LANES // S5_GROUP
STATE_TILE = GROUPS_PER_TILE * S5_STATE
HG_FAST_DECAY_LIMIT = 60.0
FFN_CHUNK = 256
FFN_ROW_SPLIT = 1
VMEM_LIMIT = 56 * 1024 * 1024

F32 = jnp.float32
BF16 = jnp.bfloat16


def _dot(a, b):
    return jnp.dot(a, b, preferred_element_type=F32)


def _dot_t0(a, b):
    return lax.dot_general(a, b, (((0,), (0,)), ((), ())), preferred_element_type=F32)


def _dot_t1(a, b):
    return lax.dot_general(a, b, (((1,), (1,)), ((), ())), preferred_element_type=F32)


def _rmsnorm(x, gain):
    return x * lax.rsqrt(jnp.mean(x * x, axis=-1, keepdims=True) + EPS) * gain


def _sigmoid(x):
    return jax.nn.sigmoid(x)


def _const_spec(shape):
    nd = len(shape)
    return pl.BlockSpec(shape, lambda *_: (0,) * nd, pipeline_mode=pl.Buffered(1))


def _seq_spec(width):
    return pl.BlockSpec((SEQS, TT, width), lambda g, t: (g, t, 0))


def _interleave(seq_ref, slab_scr):
    n_slabs = slab_scr.shape[0]
    for b in range(SEQS):
        for s in range(n_slabs):
            slab_scr[s, pl.ds(b, TT, stride=SEQS), :] = seq_ref[b, :, s * LANES:(s + 1) * LANES]
    return jnp.concatenate([slab_scr[s] for s in range(n_slabs)], axis=-1)


def _deinterleave(val, slab_scr, seq_ref, row0=0):
    n_slabs = slab_scr.shape[0]
    rows = val.shape[0]
    for s in range(n_slabs):
        slab_scr[s, row0:row0 + rows, :] = val[:, s * LANES:(s + 1) * LANES]
    for b in range(SEQS):
        seq_ref[b, row0 // SEQS:(row0 + rows) // SEQS, :] = jnp.concatenate(
            [slab_scr[s, pl.ds(row0 + b, rows // SEQS, stride=SEQS), :] for s in range(n_slabs)],
            axis=-1)


def _prep_kernel(a_re_ref, a_im_ref, log_dt_ref, bt_re_ref, bt_im_ref, lbl_ref,
                 lam_re_ref, lam_im_ref, bbt_re_ref, bbt_im_ref, lb_ref, *, layer):
    a_re = a_re_ref[...]
    a_im = a_im_ref[...]
    dt = jnp.exp(log_dt_ref[...])
    mag = jnp.exp(a_re * dt)
    ang = a_im * dt
    lb_re = mag * jnp.cos(ang)
    lb_im = mag * jnp.sin(ang)
    den = a_re * a_re + a_im * a_im
    n_re = lb_re - 1.0
    n_im = lb_im
    co_re = (n_re * a_re + n_im * a_im) / den
    co_im = (n_im * a_re - n_re * a_im) / den
    lam_re_ref[...] = lb_re
    lam_im_ref[...] = lb_im
    bt_re = bt_re_ref[...]
    bt_im = bt_im_ref[...]
    bbt_re_ref[...] = co_re[:, None, :] * bt_re - co_im[:, None, :] * bt_im
    bbt_im_ref[...] = co_re[:, None, :] * bt_im + co_im[:, None, :] * bt_re
    logits = lbl_ref[...]
    m = jnp.max(logits, axis=0, keepdims=True)
    e = jnp.exp(logits - m)
    tot = jnp.sum(e, axis=0, keepdims=True)
    lb_ref[...] = jnp.sum(e[: layer + 1], axis=0, keepdims=True) / tot


def _s5_input(za, bk_ref, bu_scr):
    n_tiles = bk_ref.shape[0]
    za_bf = za.astype(BF16)
    for k in range(n_tiles):
        bu_scr[:, k * 2 * STATE_TILE:(k + 1) * 2 * STATE_TILE] = _dot(
            za_bf[:, k * LANES:(k + 1) * LANES], bk_ref[k])


def _s5_scan_output(za, lam_re_ref, lam_im_ref, ck_ref, d_ref, wglu_ref, bglu_ref, bu_scr, st_scr):
    n_tiles = ck_ref.shape[0]
    for k in range(n_tiles):
        lr = lam_re_ref[k]
        li = lam_im_ref[k]
        c_re = k * 2 * STATE_TILE
        c_im = c_re + STATE_TILE
        xr = st_scr[k, 0]
        xi = st_scr[k, 1]
        for t in range(TT):
            r = slice(t * SEQS, (t + 1) * SEQS)
            nr = lr * xr - li * xi + bu_scr[r, c_re:c_re + STATE_TILE]
            ni = lr * xi + li * xr + bu_scr[r, c_im:c_im + STATE_TILE]
            bu_scr[r, c_re:c_re + STATE_TILE] = nr
            bu_scr[r, c_im:c_im + STATE_TILE] = ni
            xr, xi = nr, ni
        st_scr[k, 0] = xr
        st_scr[k, 1] = xi
    ys = []
    for k in range(n_tiles):
        xk = bu_scr[:, k * 2 * STATE_TILE:(k + 1) * 2 * STATE_TILE].astype(BF16)
        ys.append(_dot(xk, ck_ref[k]))
    y = jnp.concatenate(ys, axis=-1) + d_ref[...] * za
    ya = jax.nn.gelu(y)
    return ya * _sigmoid(_dot(ya.astype(BF16), wglu_ref[...]) + bglu_ref[...])


def _hgrn_branch(zq, zf, zi, zg, lb_ref, gain_ref, hst_scr):
    n_heads = hst_scr.shape[0]
    width = zq.shape[-1]
    lb = lb_ref[...]
    f = lb + (1.0 - lb) * _sigmoid(zf)
    lf = jnp.log(f)
    kk = 1.0 - f
    acc = lf[0:SEQS]
    parts = [acc]
    for t in range(1, TT):
        acc = acc + lf[t * SEQS:(t + 1) * SEQS]
        parts.append(acc)
    bcum = jnp.concatenate(parts, axis=0)
    b_last = parts[-1]
    qa = zq * _sigmoid(zq) * (HG_HEAD_DIM ** -0.5)
    qh = qa * jnp.exp(bcum)
    kh = (kk * jnp.exp(-bcum)).astype(BF16)
    ke = (kk.reshape(TT, SEQS, width) * jnp.exp(b_last[None] - bcum.reshape(TT, SEQS, width))
          ).reshape(ROWS, width)
    dec = jnp.exp(b_last)
    iv = zi.astype(BF16)
    dlt = (lax.broadcasted_iota(jnp.int32, (ROWS, ROWS), 0)
           - lax.broadcasted_iota(jnp.int32, (ROWS, ROWS), 1))
    causal = (dlt >= 0) & ((dlt & (SEQS - 1)) == 0)
    seq_of_row = lax.broadcasted_iota(jnp.int32, (ROWS, HG_HEAD_DIM), 0) & (SEQS - 1)
    intra, inter = [], []
    for h in range(n_heads):
        sl = slice(h * HG_HEAD_DIM, (h + 1) * HG_HEAD_DIM)
        q_h = qh[:, sl]
        k_h = ke[:, sl]
        q_aug = jnp.concatenate([jnp.where(seq_of_row == b, q_h, 0.0) for b in range(SEQS)],
                                axis=-1).astype(BF16)
        k_aug = jnp.concatenate([jnp.where(seq_of_row == b, k_h, 0.0) for b in range(SEQS)],
                                axis=-1).astype(BF16)
        st = hst_scr[h]
        sc = jnp.where(causal, _dot_t1(q_h.astype(BF16), kh[:, sl]), 0.0)
        intra.append(_dot(sc.astype(BF16), iv[:, sl]))
        inter.append(_dot_t1(q_aug, st.astype(BF16)))
        dec_row = jnp.concatenate([dec[b:b + 1, sl] for b in range(SEQS)], axis=-1)
        hst_scr[h] = st * dec_row + _dot_t0(iv[:, sl], k_aug)
    o_inter = jnp.concatenate(inter, axis=-1)
    out_gate = zg * _sigmoid(zg)
    yb = _hgrn_head_norm(jnp.concatenate(intra, axis=-1) + o_inter, gain_ref) * out_gate
    return yb, (qa, kk, zi, bcum, b_last, o_inter, out_gate)


def _hgrn_head_norm(o, gain_ref):
    gain = gain_ref[...]
    return jnp.concatenate(
        [_rmsnorm(o[:, h * HG_HEAD_DIM:(h + 1) * HG_HEAD_DIM], gain[:, h * HG_HEAD_DIM:(h + 1) * HG_HEAD_DIM])
         for h in range(o.shape[-1] // HG_HEAD_DIM)], axis=-1)


def _hgrn_intra_exact(qa, kk, zi, bcum, fb_scr, oi_scr):
    width = qa.shape[-1]
    n_heads = width // HG_HEAD_DIM
    fb_scr[0] = qa
    fb_scr[1] = kk
    fb_scr[2] = zi
    fb_scr[3] = bcum

    def rows(j, t):
        return fb_scr[j, pl.ds(pl.multiple_of(t * SEQS, SEQS), SEQS), :]

    def outer(t, carry):
        q_t = rows(0, t)
        b_t = rows(3, t)

        def inner(s, acc):
            decay = jnp.where(s <= t, jnp.exp(jnp.minimum(b_t - rows(3, s), 0.0)), 0.0)
            p = q_t * rows(1, s) * decay
            i_s = rows(2, s)
            return acc + jnp.concatenate(
                [jnp.sum(p[:, h * HG_HEAD_DIM:(h + 1) * HG_HEAD_DIM], axis=-1, keepdims=True)
                 * i_s[:, h * HG_HEAD_DIM:(h + 1) * HG_HEAD_DIM] for h in range(n_heads)], axis=-1)

        acc = lax.fori_loop(0, TT, inner, jnp.zeros((SEQS, width), F32))
        oi_scr[pl.ds(pl.multiple_of(t * SEQS, SEQS), SEQS), :] = acc
        return carry

    lax.fori_loop(0, TT, outer, 0)


def _mixer_kernel(x_ref, gmix_ref, win_ref, lam_re_ref, lam_im_ref, bk_ref, ck_ref, d_ref, wglu_ref,
                  bglu_ref, lb_ref, gain_ref, wpa_ref, wpb_ref, wout_ref, out_ref,
                  slab_scr, bu_scr, sst_scr, hst_scr, oi_scr, fb_scr, *, widths):
    s5_w, hg_w, d_model = widths

    @pl.when(pl.program_id(1) == 0)
    def _():
        sst_scr[...] = jnp.zeros_like(sst_scr)
        hst_scr[...] = jnp.zeros_like(hst_scr)

    x = _interleave(x_ref, slab_scr)
    u = _rmsnorm(x, gmix_ref[...]).astype(BF16)

    def proj(lo, width):
        return _dot(u, win_ref[:, lo:lo + width])

    za = proj(0, s5_w)
    _s5_input(za, bk_ref, bu_scr)
    ya = _s5_scan_output(za, lam_re_ref, lam_im_ref, ck_ref, d_ref, wglu_ref, bglu_ref, bu_scr, sst_scr)
    o = s5_w
    yb, (qa, kk, zi, bcum, b_last, o_inter, out_gate) = _hgrn_branch(
        proj(o, hg_w), proj(o + hg_w, hg_w), proj(o + 2 * hg_w, hg_w), proj(o + 3 * hg_w, hg_w),
        lb_ref, gain_ref, hst_scr)
    o += 4 * hg_w
    m_a = _sigmoid(proj(o, d_model)) * _dot(ya.astype(BF16), wpa_ref[...])
    gate_b = _sigmoid(proj(o + d_model, d_model))

    def finish(yb):
        m = m_a + gate_b * _dot(yb.astype(BF16), wpb_ref[...])
        x_res = jnp.concatenate([slab_scr[s] for s in range(slab_scr.shape[0])], axis=-1)
        out_ref[0] = x_res + _dot(m.astype(BF16), wout_ref[...])

    finish(yb)

    @pl.when(jnp.min(b_last) < -HG_FAST_DECAY_LIMIT)
    def _():
        _hgrn_intra_exact(qa, kk, zi, bcum, fb_scr, oi_scr)
        finish(_hgrn_head_norm(oi_scr[...] + o_inter, gain_ref) * out_gate)


def _conv_taps(h, prev, w, b):
    rows = h.shape[0]
    h1 = jnp.concatenate([prev[SEQS:], h[:rows - SEQS]], axis=0)
    h2 = jnp.concatenate([prev, h[:rows - 2 * SEQS]], axis=0)
    return h2 * w[0:1] + h1 * w[1:2] + h * w[2:3] + b


def _ffn_kernel(x_ref, gf_ref, wup_ref, wconv_ref, bconv_ref, wdown_ref, gfin_ref, out_ref,
                halo_scr, slab_scr, act_scr):
    d_ff = wdown_ref.shape[0]
    n_chunks = d_ff // FFN_CHUNK
    blk_rows = ROWS // FFN_ROW_SPLIT

    @pl.when(pl.program_id(1) == 0)
    def _():
        halo_scr[...] = jnp.zeros_like(halo_scr)

    prev = [halo_scr[:, j * FFN_CHUNK:(j + 1) * FFN_CHUNK] for j in range(2 * n_chunks)]
    for blk in range(FFN_ROW_SPLIT):
        rs = slice(blk * blk_rows, (blk + 1) * blk_rows)
        x1 = x_ref[0, rs, :]
        u = _rmsnorm(x1, gf_ref[...]).astype(BF16)
        for c in range(n_chunks):
            cols = []
            for part in range(2):
                j = part * n_chunks + c
                cs = slice(j * FFN_CHUNK, (j + 1) * FFN_CHUNK)
                h = _dot(u, wup_ref[:, cs])
                cols.append(_conv_taps(h, prev[j], wconv_ref[:, cs], bconv_ref[:, cs]))
                prev[j] = h[blk_rows - 2 * SEQS:]
            gate, val = cols
            act_scr[rs, c * FFN_CHUNK:(c + 1) * FFN_CHUNK] = (
                gate * _sigmoid(gate) * val).astype(BF16)
        y = x1 + _dot(act_scr[rs, :], wdown_ref[...])
        _deinterleave(_rmsnorm(y, gfin_ref[...]), slab_scr, out_ref, blk * blk_rows)
    for j in range(2 * n_chunks):
        halo_scr[:, j * FFN_CHUNK:(j + 1) * FFN_CHUNK] = prev[j]


def _block_diag(vals):
    nt, g, a, b = vals.shape
    same = jnp.arange(g)[:, None, None, None] == jnp.arange(g)[None, None, :, None]
    return jnp.where(same, vals[:, :, :, None, :], 0).reshape(nt, g * a, g * b)


def kernel(x, g_mix, w_in, s5_a_re, s5_a_im, s5_log_dt, s5_b_re, s5_b_im, s5_c_re, s5_c_im, s5_d,
           w_glu, b_glu, hg_lb_logits, hg_norm_gain, w_pa, w_pb, w_out, g_ffn, w_up, w_conv,
           b_conv, w_down, g_final):
    depth = w_in.shape[0]
    assert depth == 1, "kernel is written for a single layer"
    bsz, seq, d_model = x.shape
    s5_w = s5_d.shape[-1]
    hg_w = hg_norm_gain.shape[-1]
    d_ff = w_down.shape[1]
    n_in = w_in.shape[-1]
    n_groups = s5_a_re.shape[1]
    n_tiles = s5_w // LANES
    n_heads = hg_w // HG_HEAD_DIM
    n_chunks = d_ff // FFN_CHUNK
    assert bsz % SEQS == 0 and seq % TT == 0 and s5_w % LANES == 0 and d_ff % FFN_CHUNK == 0
    assert d_model % LANES == 0 and n_in == s5_w + 4 * hg_w + 2 * d_model
    grid = (bsz // SEQS, seq // TT)
    cparams = pltpu.CompilerParams(vmem_limit_bytes=VMEM_LIMIT,
                                   dimension_semantics=("arbitrary", "arbitrary"))

    lam_re, lam_im, bbt_re, bbt_im, lb = pl.pallas_call(
        functools.partial(_prep_kernel, layer=0),
        out_shape=[jax.ShapeDtypeStruct((n_groups, S5_STATE), F32)] * 2
        + [jax.ShapeDtypeStruct((n_groups, S5_GROUP, S5_STATE), F32)] * 2
        + [jax.ShapeDtypeStruct((1, hg_w), F32)],
        name="prep",
    )(s5_a_re[0], s5_a_im[0], s5_log_dt[0][:, None],
      jnp.swapaxes(s5_b_re[0], 1, 2), jnp.swapaxes(s5_b_im[0], 1, 2), hg_lb_logits)

    def tiles(v):
        return v.reshape((n_tiles, GROUPS_PER_TILE) + v.shape[1:])
    lam_re_t = jnp.broadcast_to(lam_re.reshape(n_tiles, 1, STATE_TILE), (n_tiles, SEQS, STATE_TILE))
    lam_im_t = jnp.broadcast_to(lam_im.reshape(n_tiles, 1, STATE_TILE), (n_tiles, SEQS, STATE_TILE))
    bk = jnp.concatenate([_block_diag(tiles(bbt_re)), _block_diag(tiles(bbt_im))], axis=-1).astype(BF16)
    c_re_t = jnp.swapaxes(s5_c_re[0], 1, 2)
    c_im_t = jnp.swapaxes(s5_c_im[0], 1, 2)
    ck = jnp.concatenate([_block_diag(tiles(c_re_t)), _block_diag(tiles(-c_im_t))], axis=1).astype(BF16)

    x1 = pl.pallas_call(
        functools.partial(_mixer_kernel, widths=(s5_w, hg_w, d_model)),
        grid=grid,
        in_specs=[
            _seq_spec(d_model),
            _const_spec((1, d_model)),
            _const_spec((d_model, n_in)),
            _const_spec((n_tiles, SEQS, STATE_TILE)),
            _const_spec((n_tiles, SEQS, STATE_TILE)),
            _const_spec((n_tiles, LANES, 2 * STATE_TILE)),
            _const_spec((n_tiles, 2 * STATE_TILE, LANES)),
            _const_spec((1, s5_w)),
            _const_spec((s5_w, s5_w)),
            _const_spec((1, s5_w)),
            _const_spec((1, hg_w)),
            _const_spec((1, hg_w)),
            _const_spec((s5_w, d_model)),
            _const_spec((hg_w, d_model)),
            _const_spec((d_model, d_model))],
        out_specs=pl.BlockSpec((1, ROWS, d_model), lambda g, t: (g, t, 0)),
        out_shape=jax.ShapeDtypeStruct((bsz // SEQS, seq * SEQS, d_model), F32),
        scratch_shapes=[pltpu.VMEM((d_model // LANES, ROWS, LANES), F32),
                        pltpu.VMEM((ROWS, n_tiles * 2 * STATE_TILE), F32),
                        pltpu.VMEM((n_tiles, 2, SEQS, STATE_TILE), F32),
                        pltpu.VMEM((n_heads, HG_HEAD_DIM, SEQS * HG_HEAD_DIM), F32),
                        pltpu.VMEM((ROWS, hg_w), F32),
                        pltpu.VMEM((4, ROWS, hg_w), F32)],
        compiler_params=cparams,
        name="mixer",
    )(x, g_mix[0][None], w_in[0].astype(BF16), lam_re_t, lam_im_t, bk, ck,
      s5_d[0][None], w_glu[0].astype(BF16), b_glu[0][None], lb, hg_norm_gain[0][None],
      w_pa[0].astype(BF16), w_pb[0].astype(BF16), w_out[0].astype(BF16))

    return pl.pallas_call(
        _ffn_kernel,
        grid=grid,
        in_specs=[pl.BlockSpec((1, ROWS, d_model), lambda g, t: (g, t, 0)),
                  _const_spec((1, d_model)),
                  _const_spec((d_model, 2 * d_ff)),
                  _const_spec((CONV_W, 2 * d_ff)),
                  _const_spec((1, 2 * d_ff)),
                  _const_spec((d_ff, d_model)),
                  _const_spec((1, d_model))],
        out_specs=_seq_spec(d_model),
        out_shape=jax.ShapeDtypeStruct((bsz, seq, d_model), F32),
        scratch_shapes=[pltpu.VMEM((2 * SEQS, 2 * d_ff), F32),
                        pltpu.VMEM((d_model // LANES, ROWS, LANES), F32),
                        pltpu.VMEM((ROWS, d_ff), BF16)],
        compiler_params=cparams,
        name="ffn",
    )(x1, g_ffn[0][None], w_up[0].astype(BF16), w_conv[0], b_conv[0][None],
      w_down[0].astype(BF16), g_final[None])
```

```python
import functools

import jax
import jax.numpy as jnp
from jax import lax
from jax.experimental import pallas as pl
from jax.experimental.pallas import tpu as pltpu

EPS = 1e-6
S5_GROUP = 16
S5_STATE = 64
HG_HEAD_DIM = 128
CONV_W = 3

LANES = 128
SEQS = 8
TT = 64
ROWS = SEQS * TT
GROUPS_PER_TILE = LANES // S5_GROUP
STATE_TILE = GROUPS_PER_TILE * S5_STATE
HG_FAST_DECAY_LIMIT = 60.0
FFN_CHUNK = 256
FFN_ROW_SPLIT = 1
VMEM_LIMIT = 56 * 1024 * 1024

F32 = jnp.float32
BF16 = jnp.bfloat16


def _dot(a, b):
    return jnp.dot(a, b, preferred_element_type=F32)


def _dot_t0(a, b):
    return lax.dot_general(a, b, (((0,), (0,)), ((), ())), preferred_element_type=F32)


def _dot_t1(a, b):
    return lax.dot_general(a, b, (((1,), (1,)), ((), ())), preferred_element_type=F32)


def _rmsnorm(x, gain):
    return x * lax.rsqrt(jnp.mean(x * x, axis=-1, keepdims=True) + EPS) * gain


def _sigmoid(x):
    return jax.nn.sigmoid(x)


def _const_spec(shape):
    nd = len(shape)
    return pl.BlockSpec(shape, lambda *_: (0,) * nd, pipeline_mode=pl.Buffered(1))


def _seq_spec(width):
    return pl.BlockSpec((SEQS, TT, width), lambda g, t: (g, t, 0))


def _interleave(seq_ref, slab_scr):
    n_slabs = slab_scr.shape[0]
    for b in range(SEQS):
        for s in range(n_slabs):
            slab_scr[s, pl.ds(b, TT, stride=SEQS), :] = seq_ref[b, :, s * LANES:(s + 1) * LANES]
    return jnp.concatenate([slab_scr[s] for s in range(n_slabs)], axis=-1)


def _deinterleave(val, slab_scr, seq_ref, row0=0):
    n_slabs = slab_scr.shape[0]
    rows = val.shape[0]
    for s in range(n_slabs):
        slab_scr[s, row0:row0 + rows, :] = val[:, s * LANES:(s + 1) * LANES]
    for b in range(SEQS):
        seq_ref[b, row0 // SEQS:(row0 + rows) // SEQS, :] = jnp.concatenate(
            [slab_scr[s, pl.ds(row0 + b, rows // SEQS, stride=SEQS), :] for s in range(n_slabs)],
            axis=-1)


def _prep_kernel(a_re_ref, a_im_ref, log_dt_ref, bt_re_ref, bt_im_ref, c_re_ref, c_im_ref, lbl_ref,
                 lam2_re_ref, lam2_im_ref, bbt_re_ref, bbt_im_ref, lbt_re_ref, lbt_im_ref,
                 cl_re_ref, cl_im_ref, cb_ref, lb_ref, *, layer):
    a_re = a_re_ref[...]
    a_im = a_im_ref[...]
    dt = jnp.exp(log_dt_ref[...])
    mag = jnp.exp(a_re * dt)
    ang = a_im * dt
    lb_re = mag * jnp.cos(ang)
    lb_im = mag * jnp.sin(ang)
    den = a_re * a_re + a_im * a_im
    n_re = lb_re - 1.0
    n_im = lb_im
    co_re = (n_re * a_re + n_im * a_im) / den
    co_im = (n_im * a_re - n_re * a_im) / den
    lam2_re_ref[...] = lb_re * lb_re - lb_im * lb_im
    lam2_im_ref[...] = 2.0 * lb_re * lb_im
    bt_re = bt_re_ref[...]
    bt_im = bt_im_ref[...]
    bbt_re = co_re[:, None, :] * bt_re - co_im[:, None, :] * bt_im
    bbt_im = co_re[:, None, :] * bt_im + co_im[:, None, :] * bt_re
    bbt_re_ref[...] = bbt_re
    bbt_im_ref[...] = bbt_im
    l_re = lb_re[:, None, :]
    l_im = lb_im[:, None, :]
    lbt_re_ref[...] = l_re * bbt_re - l_im * bbt_im
    lbt_im_ref[...] = l_re * bbt_im + l_im * bbt_re
    c_re = c_re_ref[...]
    c_im = c_im_ref[...]
    cl_re_ref[...] = c_re * l_re - c_im * l_im
    cl_im_ref[...] = c_re * l_im + c_im * l_re
    for k in range(bt_re.shape[1]):
        cb_ref[k] = jnp.sum(c_re * bbt_re[:, k:k + 1, :] - c_im * bbt_im[:, k:k + 1, :], axis=-1)
    logits = lbl_ref[...]
    m = jnp.max(logits, axis=0, keepdims=True)
    e = jnp.exp(logits - m)
    tot = jnp.sum(e, axis=0, keepdims=True)
    lb_ref[...] = jnp.sum(e[: layer + 1], axis=0, keepdims=True) / tot


def _s5_branch(za, lam2_re_ref, lam2_im_ref, bk_ref, ck_ref, gk_ref, d_ref, wglu_ref, bglu_ref,
               bu_scr, st_scr, ycar_scr):
    n_tiles = bk_ref.shape[0]
    width = za.shape[-1]
    pairs = TT // 2
    prows = pairs * SEQS
    za3 = za.reshape(pairs, 2 * SEQS, width)
    u_even = za3[:, :SEQS, :].reshape(prows, width)
    u_odd = za3[:, SEQS:, :].reshape(prows, width)
    ue_bf = u_even.astype(BF16)
    uo_bf = u_odd.astype(BF16)
    for k in range(n_tiles):
        lt = slice(k * LANES, (k + 1) * LANES)
        bu_scr[:, k * 2 * STATE_TILE:(k + 1) * 2 * STATE_TILE] = _dot(
            jnp.concatenate([ue_bf[:, lt], uo_bf[:, lt]], axis=-1), bk_ref[k])
    for k in range(n_tiles):
        lr = lam2_re_ref[k]
        li = lam2_im_ref[k]
        c_re = k * 2 * STATE_TILE
        c_im = c_re + STATE_TILE
        xr = st_scr[k, 0]
        xi = st_scr[k, 1]
        for p in range(pairs):
            r = slice(p * SEQS, (p + 1) * SEQS)
            nr = lr * xr - li * xi + bu_scr[r, c_re:c_re + STATE_TILE]
            ni = lr * xi + li * xr + bu_scr[r, c_im:c_im + STATE_TILE]
            bu_scr[r, c_re:c_re + STATE_TILE] = nr
            bu_scr[r, c_im:c_im + STATE_TILE] = ni
            xr, xi = nr, ni
        st_scr[k, 0] = xr
        st_scr[k, 1] = xi
    y_even, y_odd = [], []
    for k in range(n_tiles):
        lt = slice(k * LANES, (k + 1) * LANES)
        xk = bu_scr[:, k * 2 * STATE_TILE:(k + 1) * 2 * STATE_TILE].astype(BF16)
        yy = _dot(xk, ck_ref[k])
        y_odd.append(yy[:, :LANES])
        nxt = yy[:, LANES:]
        y_even.append(jnp.concatenate([ycar_scr[k], nxt[:prows - SEQS]], axis=0)
                      + _dot(ue_bf[:, lt], gk_ref[k]))
        ycar_scr[k] = nxt[prows - SEQS:]
    d = d_ref[...]
    y_even = jnp.concatenate(y_even, axis=-1) + d * u_even
    y_odd = jnp.concatenate(y_odd, axis=-1) + d * u_odd
    y = jnp.concatenate([y_even.reshape(pairs, SEQS, width), y_odd.reshape(pairs, SEQS, width)],
                        axis=1).reshape(ROWS, width)
    ya = jax.nn.gelu(y)
    return ya * _sigmoid(_dot(ya.astype(BF16), wglu_ref[...]) + bglu_ref[...])


def _hgrn_branch(zq, zf, zi, zg, lb_ref, gain_ref, hst_scr):
    n_heads = hst_scr.shape[0]
    width = zq.shape[-1]
    lb = lb_ref[...]
    f = lb + (1.0 - lb) * _sigmoid(zf)
    lf = jnp.log(f)
    kk = 1.0 - f
    acc = lf[0:SEQS]
    parts = [acc]
    for t in range(1, TT):
        acc = acc + lf[t * SEQS:(t + 1) * SEQS]
        parts.append(acc)
    bcum = jnp.concatenate(parts, axis=0)
    b_last = parts[-1]
    qa = zq * _sigmoid(zq) * (HG_HEAD_DIM ** -0.5)
    qh = qa * jnp.exp(bcum)
    kh = (kk * jnp.exp(-bcum)).astype(BF16)
    ke = (kk.reshape(TT, SEQS, width) * jnp.exp(b_last[None] - bcum.reshape(TT, SEQS, width))
          ).reshape(ROWS, width)
    dec = jnp.exp(b_last)
    iv = zi.astype(BF16)
    dlt = (lax.broadcasted_iota(jnp.int32, (ROWS, ROWS), 0)
           - lax.broadcasted_iota(jnp.int32, (ROWS, ROWS), 1))
    causal = (dlt >= 0) & ((dlt & (SEQS - 1)) == 0)
    seq_of_row = lax.broadcasted_iota(jnp.int32, (ROWS, HG_HEAD_DIM), 0) & (SEQS - 1)
    intra, inter = [], []
    for h in range(n_heads):
        sl = slice(h * HG_HEAD_DIM, (h + 1) * HG_HEAD_DIM)
        q_h = qh[:, sl]
        k_h = ke[:, sl]
        q_aug = jnp.concatenate([jnp.where(seq_of_row == b, q_h, 0.0) for b in range(SEQS)],
                                axis=-1).astype(BF16)
        k_aug = jnp.concatenate([jnp.where(seq_of_row == b, k_h, 0.0) for b in range(SEQS)],
                                axis=-1).astype(BF16)
        st = hst_scr[h]
        sc = jnp.where(causal, _dot_t1(q_h.astype(BF16), kh[:, sl]), 0.0)
        intra.append(_dot(sc.astype(BF16), iv[:, sl]))
        inter.append(_dot_t1(q_aug, st.astype(BF16)))
        dec_row = jnp.concatenate([dec[b:b + 1, sl] for b in range(SEQS)], axis=-1)
        hst_scr[h] = st * dec_row + _dot_t0(iv[:, sl], k_aug)
    o_inter = jnp.concatenate(inter, axis=-1)
    out_gate = zg * _sigmoid(zg)
    yb = _hgrn_head_norm(jnp.concatenate(intra, axis=-1) + o_inter, gain_ref) * out_gate
    return yb, (qa, kk, zi, bcum, b_last, o_inter, out_gate)


def _hgrn_head_norm(o, gain_ref):
    gain = gain_ref[...]
    return jnp.concatenate(
        [_rmsnorm(o[:, h * HG_HEAD_DIM:(h + 1) * HG_HEAD_DIM], gain[:, h * HG_HEAD_DIM:(h + 1) * HG_HEAD_DIM])
         for h in range(o.shape[-1] // HG_HEAD_DIM)], axis=-1)


def _hgrn_intra_exact(qa, kk, zi, bcum, fb_scr, oi_scr):
    width = qa.shape[-1]
    n_heads = width // HG_HEAD_DIM
    fb_scr[0] = qa
    fb_scr[1] = kk
    fb_scr[2] = zi
    fb_scr[3] = bcum

    def rows(j, t):
        return fb_scr[j, pl.ds(pl.multiple_of(t * SEQS, SEQS), SEQS), :]

    def outer(t, carry):
        q_t = rows(0, t)
        b_t = rows(3, t)

        def inner(s, acc):
            decay = jnp.where(s <= t, jnp.exp(jnp.minimum(b_t - rows(3, s), 0.0)), 0.0)
            p = q_t * rows(1, s) * decay
            i_s = rows(2, s)
            return acc + jnp.concatenate(
                [jnp.sum(p[:, h * HG_HEAD_DIM:(h + 1) * HG_HEAD_DIM], axis=-1, keepdims=True)
                 * i_s[:, h * HG_HEAD_DIM:(h + 1) * HG_HEAD_DIM] for h in range(n_heads)], axis=-1)

        acc = lax.fori_loop(0, TT, inner, jnp.zeros((SEQS, width), F32))
        oi_scr[pl.ds(pl.multiple_of(t * SEQS, SEQS), SEQS), :] = acc
        return carry

    lax.fori_loop(0, TT, outer, 0)


def _mixer_kernel(x_ref, gmix_ref, win_ref, lam2_re_ref, lam2_im_ref, bk_ref, ck_ref, gk_ref, d_ref,
                  wglu_ref, bglu_ref, lb_ref, gain_ref, wpa_ref, wpb_ref, wout_ref, out_ref,
                  slab_scr, bu_scr, sst_scr, ycar_scr, hst_scr, oi_scr, fb_scr, *, widths):
    s5_w, hg_w, d_model = widths

    @pl.when(pl.program_id(1) == 0)
    def _():
        sst_scr[...] = jnp.zeros_like(sst_scr)
        ycar_scr[...] = jnp.zeros_like(ycar_scr)
        hst_scr[...] = jnp.zeros_like(hst_scr)

    x = _interleave(x_ref, slab_scr)
    u = _rmsnorm(x, gmix_ref[...]).astype(BF16)

    def proj(lo, width):
        return _dot(u, win_ref[:, lo:lo + width])

    ya = _s5_branch(proj(0, s5_w), lam2_re_ref, lam2_im_ref, bk_ref, ck_ref, gk_ref, d_ref, wglu_ref,
                    bglu_ref, bu_scr, sst_scr, ycar_scr)
    o = s5_w
    yb, (qa, kk, zi, bcum, b_last, o_inter, out_gate) = _hgrn_branch(
        proj(o, hg_w), proj(o + hg_w, hg_w), proj(o + 2 * hg_w, hg_w), proj(o + 3 * hg_w, hg_w),
        lb_ref, gain_ref, hst_scr)
    o += 4 * hg_w
    m_a = _sigmoid(proj(o, d_model)) * _dot(ya.astype(BF16), wpa_ref[...])
    gate_b = _sigmoid(proj(o + d_model, d_model))

    def finish(yb):
        m = m_a + gate_b * _dot(yb.astype(BF16), wpb_ref[...])
        x_res = jnp.concatenate([slab_scr[s] for s in range(slab_scr.shape[0])], axis=-1)
        out_ref[0] = x_res + _dot(m.astype(BF16), wout_ref[...])

    finish(yb)

    @pl.when(jnp.min(b_last) < -HG_FAST_DECAY_LIMIT)
    def _():
        _hgrn_intra_exact(qa, kk, zi, bcum, fb_scr, oi_scr)
        finish(_hgrn_head_norm(oi_scr[...] + o_inter, gain_ref) * out_gate)


def _conv_taps(h, prev, w, b):
    rows = h.shape[0]
    h1 = jnp.concatenate([prev[SEQS:], h[:rows - SEQS]], axis=0)
    h2 = jnp.concatenate([prev, h[:rows - 2 * SEQS]], axis=0)
    return h2 * w[0:1] + h1 * w[1:2] + h * w[2:3] + b


def _ffn_kernel(x_ref, gf_ref, wup_ref, wconv_ref, bconv_ref, wdown_ref, gfin_ref, out_ref,
                halo_scr, slab_scr, act_scr):
    d_ff = wdown_ref.shape[0]
    n_chunks = d_ff // FFN_CHUNK
    blk_rows = ROWS // FFN_ROW_SPLIT

    @pl.when(pl.program_id(1) == 0)
    def _():
        halo_scr[...] = jnp.zeros_like(halo_scr)

    prev = [halo_scr[:, j * FFN_CHUNK:(j + 1) * FFN_CHUNK] for j in range(2 * n_chunks)]
    for blk in range(FFN_ROW_SPLIT):
        rs = slice(blk * blk_rows, (blk + 1) * blk_rows)
        x1 = x_ref[0, rs, :]
        u = _rmsnorm(x1, gf_ref[...]).astype(BF16)
        for c in range(n_chunks):
            cols = []
            for part in range(2):
                j = part * n_chunks + c
                cs = slice(j * FFN_CHUNK, (j + 1) * FFN_CHUNK)
                h = _dot(u, wup_ref[:, cs])
                cols.append(_conv_taps(h, prev[j], wconv_ref[:, cs], bconv_ref[:, cs]))
                prev[j] = h[blk_rows - 2 * SEQS:]
            gate, val = cols
            act_scr[rs, c * FFN_CHUNK:(c + 1) * FFN_CHUNK] = (
                gate * _sigmoid(gate) * val).astype(BF16)
        y = x1 + _dot(act_scr[rs, :], wdown_ref[...])
        _deinterleave(_rmsnorm(y, gfin_ref[...]), slab_scr, out_ref, blk * blk_rows)
    for j in range(2 * n_chunks):
        halo_scr[:, j * FFN_CHUNK:(j + 1) * FFN_CHUNK] = prev[j]


def _block_diag(vals):
    nt, g, a, b = vals.shape
    same = jnp.arange(g)[:, None, None, None] == jnp.arange(g)[None, None, :, None]
    return jnp.where(same, vals[:, :, :, None, :], 0).reshape(nt, g * a, g * b)


def kernel(x, g_mix, w_in, s5_a_re, s5_a_im, s5_log_dt, s5_b_re, s5_b_im, s5_c_re, s5_c_im, s5_d,
           w_glu, b_glu, hg_lb_logits, hg_norm_gain, w_pa, w_pb, w_out, g_ffn, w_up, w_conv,
           b_conv, w_down, g_final):
    depth = w_in.shape[0]
    assert depth == 1, "kernel is written for a single layer"
    bsz, seq, d_model = x.shape
    s5_w = s5_d.shape[-1]
    hg_w = hg_norm_gain.shape[-1]
    d_ff = w_down.shape[1]
    n_in = w_in.shape[-1]
    n_groups = s5_a_re.shape[1]
    n_tiles = s5_w // LANES
    n_heads = hg_w // HG_HEAD_DIM
    n_chunks = d_ff // FFN_CHUNK
    assert bsz % SEQS == 0 and seq % TT == 0 and s5_w % LANES == 0 and d_ff % FFN_CHUNK == 0
    assert d_model % LANES == 0 and n_in == s5_w + 4 * hg_w + 2 * d_model
    grid = (bsz // SEQS, seq // TT)
    cparams = pltpu.CompilerParams(vmem_limit_bytes=VMEM_LIMIT,
                                   dimension_semantics=("arbitrary", "arbitrary"))

    gps = jax.ShapeDtypeStruct((n_groups, S5_STATE), F32)
    ghps = jax.ShapeDtypeStruct((n_groups, S5_GROUP, S5_STATE), F32)
    lam2_re, lam2_im, bbt_re, bbt_im, lbt_re, lbt_im, cl_re, cl_im, cb, lb = pl.pallas_call(
        functools.partial(_prep_kernel, layer=0),
        out_shape=[gps] * 2 + [ghps] * 6
        + [jax.ShapeDtypeStruct((S5_GROUP, n_groups, S5_GROUP), F32),
           jax.ShapeDtypeStruct((1, hg_w), F32)],
        name="prep",
    )(s5_a_re[0], s5_a_im[0], s5_log_dt[0][:, None],
      jnp.swapaxes(s5_b_re[0], 1, 2), jnp.swapaxes(s5_b_im[0], 1, 2), s5_c_re[0], s5_c_im[0],
      hg_lb_logits)

    def tiles(v):
        return v.reshape((n_tiles, GROUPS_PER_TILE) + v.shape[1:])

    def bd(v):
        return _block_diag(tiles(v))

    def bd_t(v):
        return _block_diag(tiles(jnp.swapaxes(v, 1, 2)))
    lam2_re_t = jnp.broadcast_to(lam2_re.reshape(n_tiles, 1, STATE_TILE), (n_tiles, SEQS, STATE_TILE))
    lam2_im_t = jnp.broadcast_to(lam2_im.reshape(n_tiles, 1, STATE_TILE), (n_tiles, SEQS, STATE_TILE))
    bk = jnp.concatenate([jnp.concatenate([bd(lbt_re), bd(lbt_im)], axis=-1),
                          jnp.concatenate([bd(bbt_re), bd(bbt_im)], axis=-1)], axis=1).astype(BF16)
    ck = jnp.concatenate([jnp.concatenate([bd_t(s5_c_re[0]), bd_t(-s5_c_im[0])], axis=1),
                          jnp.concatenate([bd_t(cl_re), bd_t(-cl_im)], axis=1)], axis=-1).astype(BF16)
    gk = bd(jnp.swapaxes(cb, 0, 1)).astype(BF16)

    x1 = pl.pallas_call(
        functools.partial(_mixer_kernel, widths=(s5_w, hg_w, d_model)),
        grid=grid,
        in_specs=[
            _seq_spec(d_model),
            _const_spec((1, d_model)),
            _const_spec((d_model, n_in)),
            _const_spec((n_tiles, SEQS, STATE_TILE)),
            _const_spec((n_tiles, SEQS, STATE_TILE)),
            _const_spec((n_tiles, 2 * LANES, 2 * STATE_TILE)),
            _const_spec((n_tiles, 2 * STATE_TILE, 2 * LANES)),
            _const_spec((n_tiles, LANES, LANES)),
            _const_spec((1, s5_w)),
            _const_spec((s5_w, s5_w)),
            _const_spec((1, s5_w)),
            _const_spec((1, hg_w)),
            _const_spec((1, hg_w)),
            _const_spec((s5_w, d_model)),
            _const_spec((hg_w, d_model)),
            _const_spec((d_model, d_model))],
        out_specs=pl.BlockSpec((1, ROWS, d_model), lambda g, t: (g, t, 0)),
        out_shape=jax.ShapeDtypeStruct((bsz // SEQS, seq * SEQS, d_model), F32),
        scratch_shapes=[pltpu.VMEM((d_model // LANES, ROWS, LANES), F32),
                        pltpu.VMEM((ROWS // 2, n_tiles * 2 * STATE_TILE), F32),
                        pltpu.VMEM((n_tiles, 2, SEQS, STATE_TILE), F32),
                        pltpu.VMEM((n_tiles, SEQS, LANES), F32),
                        pltpu.VMEM((n_heads, HG_HEAD_DIM, SEQS * HG_HEAD_DIM), F32),
                        pltpu.VMEM((ROWS, hg_w), F32),
                        pltpu.VMEM((4, ROWS, hg_w), F32)],
        compiler_params=cparams,
        name="mixer",
    )(x, g_mix[0][None], w_in[0].astype(BF16), lam2_re_t, lam2_im_t, bk, ck, gk,
      s5_d[0][None], w_glu[0].astype(BF16), b_glu[0][None], lb, hg_norm_gain[0][None],
      w_pa[0].astype(BF16), w_pb[0].astype(BF16), w_out[0].astype(BF16))

    return pl.pallas_call(
        _ffn_kernel,
        grid=grid,
        in_specs=[pl.BlockSpec((1, ROWS, d_model), lambda g, t: (g, t, 0)),
                  _const_spec((1, d_model)),
                  _const_spec((d_model, 2 * d_ff)),
                  _const_spec((CONV_W, 2 * d_ff)),
                  _const_spec((1, 2 * d_ff)),
                  _const_spec((d_ff, d_model)),
                  _const_spec((1, d_model))],
        out_specs=_seq_spec(d_model),
        out_shape=jax.ShapeDtypeStruct((bsz, seq, d_model), F32),
        scratch_shapes=[pltpu.VMEM((2 * SEQS, 2 * d_ff), F32),
                        pltpu.VMEM((d_model // LANES, ROWS, LANES), F32),
                        pltpu.VMEM((ROWS, d_ff), BF16)],
        compiler_params=cparams,
        name="ffn",
    )(x1, g_ffn[0][None], w_up[0].astype(BF16), w_conv[0], b_conv[0][None],
      w_down[0].astype(BF16), g_final[None])
```

```python
import functools

import jax
import jax.numpy as jnp
from jax import lax
from jax.experimental import pallas as pl
from jax.experimental.pallas import tpu as pltpu

EPS = 1e-6
S5_GROUP = 16
S5_STATE = 64
HG_HEAD_DIM = 128
CONV_W = 3

LANES = 128
SEQS = 8
TT = 64
ROWS = SEQS * TT
GROUPS_PER_TILE = LANES // S5_GROUP
STATE_TILE = GROUPS_PER_TILE * S5_STATE
HG_FAST_DECAY_LIMIT = 60.0
FFN_CHUNK = 256
FFN_OUT_SPLIT = 2
MIX_HEAD_CHUNKS = 4
VMEM_LIMIT = 56 * 1024 * 1024

F32 = jnp.float32
BF16 = jnp.bfloat16


def _dot(a, b):
    return jnp.dot(a, b, preferred_element_type=F32)


def _dot_t0(a, b):
    return lax.dot_general(a, b, (((0,), (0,)), ((), ())), preferred_element_type=F32)


def _dot_t1(a, b):
    return lax.dot_general(a, b, (((1,), (1,)), ((), ())), preferred_element_type=F32)


def _rmsnorm(x, gain):
    return x * lax.rsqrt(jnp.mean(x * x, axis=-1, keepdims=True) + EPS) * gain


def _sigmoid(x):
    return jax.nn.sigmoid(x)


def _const_spec(shape):
    nd = len(shape)
    return pl.BlockSpec(shape, lambda *_: (0,) * nd, pipeline_mode=pl.Buffered(1))


def _seq_spec(width):
    return pl.BlockSpec((SEQS, TT, width), lambda g, t: (g, t, 0))


def _interleave(seq_ref, slab_scr, t0, nt):
    n_slabs = slab_scr.shape[0]
    for b in range(SEQS):
        for s in range(n_slabs):
            slab_scr[s, pl.ds(t0 * SEQS + b, nt, stride=SEQS), :] = (
                seq_ref[b, t0:t0 + nt, s * LANES:(s + 1) * LANES])
    return jnp.concatenate([slab_scr[s, t0 * SEQS:(t0 + nt) * SEQS, :] for s in range(n_slabs)], axis=-1)


def _deinterleave(val, slab_scr, seq_ref, row0=0):
    n_slabs = slab_scr.shape[0]
    rows = val.shape[0]
    for s in range(n_slabs):
        slab_scr[s, row0:row0 + rows, :] = val[:, s * LANES:(s + 1) * LANES]
    for b in range(SEQS):
        seq_ref[b, row0 // SEQS:(row0 + rows) // SEQS, :] = jnp.concatenate(
            [slab_scr[s, pl.ds(row0 + b, rows // SEQS, stride=SEQS), :] for s in range(n_slabs)],
            axis=-1)


def _prep_kernel(a_re_ref, a_im_ref, log_dt_ref, bt_re_ref, bt_im_ref, c_re_ref, c_im_ref, lbl_ref,
                 lam2_re_ref, lam2_im_ref, bbt_re_ref, bbt_im_ref, lbt_re_ref, lbt_im_ref,
                 cl_re_ref, cl_im_ref, cb_ref, lb_ref, *, layer):
    a_re = a_re_ref[...]
    a_im = a_im_ref[...]
    dt = jnp.exp(log_dt_ref[...])
    mag = jnp.exp(a_re * dt)
    ang = a_im * dt
    lb_re = mag * jnp.cos(ang)
    lb_im = mag * jnp.sin(ang)
    den = a_re * a_re + a_im * a_im
    n_re = lb_re - 1.0
    n_im = lb_im
    co_re = (n_re * a_re + n_im * a_im) / den
    co_im = (n_im * a_re - n_re * a_im) / den
    lam2_re_ref[...] = lb_re * lb_re - lb_im * lb_im
    lam2_im_ref[...] = 2.0 * lb_re * lb_im
    bt_re = bt_re_ref[...]
    bt_im = bt_im_ref[...]
    bbt_re = co_re[:, None, :] * bt_re - co_im[:, None, :] * bt_im
    bbt_im = co_re[:, None, :] * bt_im + co_im[:, None, :] * bt_re
    bbt_re_ref[...] = bbt_re
    bbt_im_ref[...] = bbt_im
    l_re = lb_re[:, None, :]
    l_im = lb_im[:, None, :]
    lbt_re_ref[...] = l_re * bbt_re - l_im * bbt_im
    lbt_im_ref[...] = l_re * bbt_im + l_im * bbt_re
    c_re = c_re_ref[...]
    c_im = c_im_ref[...]
    cl_re_ref[...] = c_re * l_re - c_im * l_im
    cl_im_ref[...] = c_re * l_im + c_im * l_re
    for k in range(bt_re.shape[1]):
        cb_ref[k] = jnp.sum(c_re * bbt_re[:, k:k + 1, :] - c_im * bbt_im[:, k:k + 1, :], axis=-1)
    logits = lbl_ref[...]
    m = jnp.max(logits, axis=0, keepdims=True)
    e = jnp.exp(logits - m)
    tot = jnp.sum(e, axis=0, keepdims=True)
    lb_ref[...] = jnp.sum(e[: layer + 1], axis=0, keepdims=True) / tot


def _s5_branch(za, lam2_re_ref, lam2_im_ref, bk_ref, ck_ref, gk_ref, d_ref, bu_scr, st_scr, ycar_scr):
    n_tiles = bk_ref.shape[0]
    width = za.shape[-1]
    pairs = TT // 2
    prows = pairs * SEQS
    za3 = za.reshape(pairs, 2 * SEQS, width)
    u_even = za3[:, :SEQS, :].reshape(prows, width)
    u_odd = za3[:, SEQS:, :].reshape(prows, width)
    ue_bf = u_even.astype(BF16)
    uo_bf = u_odd.astype(BF16)
    for k in range(n_tiles):
        lt = slice(k * LANES, (k + 1) * LANES)
        bu_scr[:, k * 2 * STATE_TILE:(k + 1) * 2 * STATE_TILE] = _dot(
            jnp.concatenate([ue_bf[:, lt], uo_bf[:, lt]], axis=-1), bk_ref[k])
    for k in range(n_tiles):
        lr = lam2_re_ref[k]
        li = lam2_im_ref[k]
        c_re = k * 2 * STATE_TILE
        c_im = c_re + STATE_TILE
        xr = st_scr[k, 0]
        xi = st_scr[k, 1]
        for p in range(pairs):
            r = slice(p * SEQS, (p + 1) * SEQS)
            nr = lr * xr - li * xi + bu_scr[r, c_re:c_re + STATE_TILE]
            ni = lr * xi + li * xr + bu_scr[r, c_im:c_im + STATE_TILE]
            bu_scr[r, c_re:c_re + STATE_TILE] = nr
            bu_scr[r, c_im:c_im + STATE_TILE] = ni
            xr, xi = nr, ni
        st_scr[k, 0] = xr
        st_scr[k, 1] = xi
    y_even, y_odd = [], []
    for k in range(n_tiles):
        lt = slice(k * LANES, (k + 1) * LANES)
        xk = bu_scr[:, k * 2 * STATE_TILE:(k + 1) * 2 * STATE_TILE].astype(BF16)
        yy = _dot(xk, ck_ref[k])
        y_odd.append(yy[:, :LANES])
        nxt = yy[:, LANES:]
        y_even.append(jnp.concatenate([ycar_scr[k], nxt[:prows - SEQS]], axis=0)
                      + _dot(ue_bf[:, lt], gk_ref[k]))
        ycar_scr[k] = nxt[prows - SEQS:]
    d = d_ref[...]
    y_even = jnp.concatenate(y_even, axis=-1) + d * u_even
    y_odd = jnp.concatenate(y_odd, axis=-1) + d * u_odd
    return jnp.concatenate([y_even.reshape(pairs, SEQS, width), y_odd.reshape(pairs, SEQS, width)],
                           axis=1).reshape(ROWS, width)


def _hgrn_branch(zq, zf, zi, zg, lb_ref, gain_ref, hst_scr):
    n_heads = hst_scr.shape[0]
    width = zq.shape[-1]
    lb = lb_ref[...]
    f = lb + (1.0 - lb) * _sigmoid(zf)
    lf = jnp.log(f)
    kk = 1.0 - f
    acc = lf[0:SEQS]
    parts = [acc]
    for t in range(1, TT):
        acc = acc + lf[t * SEQS:(t + 1) * SEQS]
        parts.append(acc)
    bcum = jnp.concatenate(parts, axis=0)
    b_last = parts[-1]
    qa = zq * _sigmoid(zq) * (HG_HEAD_DIM ** -0.5)
    qh = qa * jnp.exp(bcum)
    kh = (kk * jnp.exp(-bcum)).astype(BF16)
    ke = (kk.reshape(TT, SEQS, width) * jnp.exp(b_last[None] - bcum.reshape(TT, SEQS, width))
          ).reshape(ROWS, width)
    dec = jnp.exp(b_last)
    iv = zi.astype(BF16)
    half = ROWS // 2
    dlt = (lax.broadcasted_iota(jnp.int32, (half, half), 0)
           - lax.broadcasted_iota(jnp.int32, (half, half), 1))
    same_seq = (dlt & (SEQS - 1)) == 0
    causal = same_seq & (dlt >= 0)
    seq_of_row = lax.broadcasted_iota(jnp.int32, (ROWS, HG_HEAD_DIM), 0) & (SEQS - 1)
    intra, inter = [], []
    for h in range(n_heads):
        sl = slice(h * HG_HEAD_DIM, (h + 1) * HG_HEAD_DIM)
        q_h = qh[:, sl]
        k_h = ke[:, sl]
        q_aug = jnp.concatenate([jnp.where(seq_of_row == b, q_h, 0.0) for b in range(SEQS)],
                                axis=-1).astype(BF16)
        k_aug = jnp.concatenate([jnp.where(seq_of_row == b, k_h, 0.0) for b in range(SEQS)],
                                axis=-1).astype(BF16)
        st = hst_scr[h]
        q_bf = q_h.astype(BF16)
        k_e, k_l = kh[:half, sl], kh[half:, sl]
        sc_ee = jnp.where(causal, _dot_t1(q_bf[:half], k_e), 0.0).astype(BF16)
        sc_le = jnp.where(same_seq, _dot_t1(q_bf[half:], k_e), 0.0).astype(BF16)
        sc_ll = jnp.where(causal, _dot_t1(q_bf[half:], k_l), 0.0).astype(BF16)
        intra.append(jnp.concatenate(
            [_dot(sc_ee, iv[:half, sl]),
             _dot(jnp.concatenate([sc_le, sc_ll], axis=-1), iv[:, sl])], axis=0))
        inter.append(_dot_t1(q_aug, st.astype(BF16)))
        dec_row = jnp.concatenate([dec[b:b + 1, sl] for b in range(SEQS)], axis=-1)
        hst_scr[h] = st * dec_row + _dot_t0(iv[:, sl], k_aug)
    o_inter = jnp.concatenate(inter, axis=-1)
    out_gate = zg * _sigmoid(zg)
    yb = _hgrn_head_norm(jnp.concatenate(intra, axis=-1) + o_inter, gain_ref) * out_gate
    return yb, (qa, kk, zi, bcum, b_last, o_inter, out_gate)


def _hgrn_head_norm(o, gain_ref):
    gain = gain_ref[...]
    return jnp.concatenate(
        [_rmsnorm(o[:, h * HG_HEAD_DIM:(h + 1) * HG_HEAD_DIM], gain[:, h * HG_HEAD_DIM:(h + 1) * HG_HEAD_DIM])
         for h in range(o.shape[-1] // HG_HEAD_DIM)], axis=-1)


def _hgrn_intra_exact(qa, kk, zi, bcum, fb_scr, oi_scr):
    width = qa.shape[-1]
    n_heads = width // HG_HEAD_DIM
    fb_scr[0] = qa
    fb_scr[1] = kk
    fb_scr[2] = zi
    fb_scr[3] = bcum

    def rows(j, t):
        return fb_scr[j, pl.ds(pl.multiple_of(t * SEQS, SEQS), SEQS), :]

    def outer(t, carry):
        q_t = rows(0, t)
        b_t = rows(3, t)

        def inner(s, acc):
            decay = jnp.where(s <= t, jnp.exp(jnp.minimum(b_t - rows(3, s), 0.0)), 0.0)
            p = q_t * rows(1, s) * decay
            i_s = rows(2, s)
            return acc + jnp.concatenate(
                [jnp.sum(p[:, h * HG_HEAD_DIM:(h + 1) * HG_HEAD_DIM], axis=-1, keepdims=True)
                 * i_s[:, h * HG_HEAD_DIM:(h + 1) * HG_HEAD_DIM] for h in range(n_heads)], axis=-1)

        acc = lax.fori_loop(0, TT, inner, jnp.zeros((SEQS, width), F32))
        oi_scr[pl.ds(pl.multiple_of(t * SEQS, SEQS), SEQS), :] = acc
        return carry

    lax.fori_loop(0, TT, outer, 0)


def _mixer_kernel(x_ref, gmix_ref, win_ref, lam2_re_ref, lam2_im_ref, bk_ref, ck_ref, gk_ref, d_ref,
                  wglu_ref, bglu_ref, lb_ref, gain_ref, wpa_ref, wpb_ref, wout_ref, out_ref,
                  slab_scr, bu_scr, sst_scr, ycar_scr, hst_scr, oi_scr, fb_scr, *, widths):
    s5_w, hg_w, d_model = widths

    @pl.when(pl.program_id(1) == 0)
    def _():
        sst_scr[...] = jnp.zeros_like(sst_scr)
        ycar_scr[...] = jnp.zeros_like(ycar_scr)
        hst_scr[...] = jnp.zeros_like(hst_scr)

    nt = TT // MIX_HEAD_CHUNKS
    us = [_rmsnorm(_interleave(x_ref, slab_scr, c * nt, nt), gmix_ref[...]).astype(BF16)
          for c in range(MIX_HEAD_CHUNKS)]
    za = jnp.concatenate([_dot(uc, win_ref[:, :s5_w]) for uc in us], axis=0)
    u = jnp.concatenate(us, axis=0)

    def proj(lo, width):
        return _dot(u, win_ref[:, lo:lo + width])

    y_s5 = _s5_branch(za, lam2_re_ref, lam2_im_ref, bk_ref, ck_ref, gk_ref, d_ref, bu_scr, sst_scr,
                      ycar_scr)
    o = s5_w
    yb, (qa, kk, zi, bcum, b_last, o_inter, out_gate) = _hgrn_branch(
        proj(o, hg_w), proj(o + hg_w, hg_w), proj(o + 2 * hg_w, hg_w), proj(o + 3 * hg_w, hg_w),
        lb_ref, gain_ref, hst_scr)
    o += 4 * hg_w
    gate_a = _sigmoid(proj(o, d_model))
    gate_b = _sigmoid(proj(o + d_model, d_model))

    ya = jax.nn.gelu(y_s5)
    ya = ya * _sigmoid(_dot(ya.astype(BF16), wglu_ref[...]) + bglu_ref[...])
    m_a = gate_a * _dot(ya.astype(BF16), wpa_ref[...])

    def finish(yb):
        m = m_a + gate_b * _dot(yb.astype(BF16), wpb_ref[...])
        x_res = jnp.concatenate([slab_scr[s] for s in range(slab_scr.shape[0])], axis=-1)
        out_ref[0] = x_res + _dot(m.astype(BF16), wout_ref[...])

    finish(yb)

    @pl.when(jnp.min(b_last) < -HG_FAST_DECAY_LIMIT)
    def _():
        _hgrn_intra_exact(qa, kk, zi, bcum, fb_scr, oi_scr)
        finish(_hgrn_head_norm(oi_scr[...] + o_inter, gain_ref) * out_gate)


def _conv_taps(h, prev, w, b):
    rows = h.shape[0]
    h1 = jnp.concatenate([prev[SEQS:], h[:rows - SEQS]], axis=0)
    h2 = jnp.concatenate([prev, h[:rows - 2 * SEQS]], axis=0)
    return h2 * w[0:1] + h1 * w[1:2] + h * w[2:3] + b


def _ffn_kernel(x_ref, gf_ref, wup_ref, wconv_ref, bconv_ref, wdown_ref, gfin_ref, out_ref,
                halo_scr, slab_scr, act_scr):
    d_ff = wdown_ref.shape[0]
    n_chunks = d_ff // FFN_CHUNK

    @pl.when(pl.program_id(1) == 0)
    def _():
        halo_scr[...] = jnp.zeros_like(halo_scr)

    x1 = x_ref[0]
    u = _rmsnorm(x1, gf_ref[...]).astype(BF16)
    for c in range(n_chunks):
        cols = []
        for part in range(2):
            j = part * n_chunks + c
            cs = slice(j * FFN_CHUNK, (j + 1) * FFN_CHUNK)
            h = _dot(u, wup_ref[:, cs])
            cols.append(_conv_taps(h, halo_scr[:, cs], wconv_ref[:, cs], bconv_ref[:, cs]))
            halo_scr[:, cs] = h[ROWS - 2 * SEQS:]
        gate, val = cols
        act_scr[:, c * FFN_CHUNK:(c + 1) * FFN_CHUNK] = (gate * _sigmoid(gate) * val).astype(BF16)
    blk = ROWS // FFN_OUT_SPLIT
    for i in range(FFN_OUT_SPLIT):
        rs = slice(i * blk, (i + 1) * blk)
        y = x1[rs] + _dot(act_scr[rs, :], wdown_ref[...])
        _deinterleave(_rmsnorm(y, gfin_ref[...]), slab_scr, out_ref, i * blk)


def _block_diag(vals):
    nt, g, a, b = vals.shape
    same = jnp.arange(g)[:, None, None, None] == jnp.arange(g)[None, None, :, None]
    return jnp.where(same, vals[:, :, :, None, :], 0).reshape(nt, g * a, g * b)


def kernel(x, g_mix, w_in, s5_a_re, s5_a_im, s5_log_dt, s5_b_re, s5_b_im, s5_c_re, s5_c_im, s5_d,
           w_glu, b_glu, hg_lb_logits, hg_norm_gain, w_pa, w_pb, w_out, g_ffn, w_up, w_conv,
           b_conv, w_down, g_final):
    depth = w_in.shape[0]
    assert depth == 1, "kernel is written for a single layer"
    bsz, seq, d_model = x.shape
    s5_w = s5_d.shape[-1]
    hg_w = hg_norm_gain.shape[-1]
    d_ff = w_down.shape[1]
    n_in = w_in.shape[-1]
    n_groups = s5_a_re.shape[1]
    n_tiles = s5_w // LANES
    n_heads = hg_w // HG_HEAD_DIM
    n_chunks = d_ff // FFN_CHUNK
    assert bsz % SEQS == 0 and seq % TT == 0 and s5_w % LANES == 0 and d_ff % FFN_CHUNK == 0
    assert d_model % LANES == 0 and n_in == s5_w + 4 * hg_w + 2 * d_model
    grid = (bsz // SEQS, seq // TT)
    cparams = pltpu.CompilerParams(vmem_limit_bytes=VMEM_LIMIT,
                                   dimension_semantics=("arbitrary", "arbitrary"))

    gps = jax.ShapeDtypeStruct((n_groups, S5_STATE), F32)
    ghps = jax.ShapeDtypeStruct((n_groups, S5_GROUP, S5_STATE), F32)
    lam2_re, lam2_im, bbt_re, bbt_im, lbt_re, lbt_im, cl_re, cl_im, cb, lb = pl.pallas_call(
        functools.partial(_prep_kernel, layer=0),
        out_shape=[gps] * 2 + [ghps] * 6
        + [jax.ShapeDtypeStruct((S5_GROUP, n_groups, S5_GROUP), F32),
           jax.ShapeDtypeStruct((1, hg_w), F32)],
        name="prep",
    )(s5_a_re[0], s5_a_im[0], s5_log_dt[0][:, None],
      jnp.swapaxes(s5_b_re[0], 1, 2), jnp.swapaxes(s5_b_im[0], 1, 2), s5_c_re[0], s5_c_im[0],
      hg_lb_logits)

    def tiles(v):
        return v.reshape((n_tiles, GROUPS_PER_TILE) + v.shape[1:])

    def bd(v):
        return _block_diag(tiles(v))

    def bd_t(v):
        return _block_diag(tiles(jnp.swapaxes(v, 1, 2)))
    lam2_re_t = jnp.broadcast_to(lam2_re.reshape(n_tiles, 1, STATE_TILE), (n_tiles, SEQS, STATE_TILE))
    lam2_im_t = jnp.broadcast_to(lam2_im.reshape(n_tiles, 1, STATE_TILE), (n_tiles, SEQS, STATE_TILE))
    bk = jnp.concatenate([jnp.concatenate([bd(lbt_re), bd(lbt_im)], axis=-1),
                          jnp.concatenate([bd(bbt_re), bd(bbt_im)], axis=-1)], axis=1).astype(BF16)
    ck = jnp.concatenate([jnp.concatenate([bd_t(s5_c_re[0]), bd_t(-s5_c_im[0])], axis=1),
                          jnp.concatenate([bd_t(cl_re), bd_t(-cl_im)], axis=1)], axis=-1).astype(BF16)
    gk = bd(jnp.swapaxes(cb, 0, 1)).astype(BF16)

    x1 = pl.pallas_call(
        functools.partial(_mixer_kernel, widths=(s5_w, hg_w, d_model)),
        grid=grid,
        in_specs=[
            _seq_spec(d_model),
            _const_spec((1, d_model)),
            _const_spec((d_model, n_in)),
            _const_spec((n_tiles, SEQS, STATE_TILE)),
            _const_spec((n_tiles, SEQS, STATE_TILE)),
            _const_spec((n_tiles, 2 * LANES, 2 * STATE_TILE)),
            _const_spec((n_tiles, 2 * STATE_TILE, 2 * LANES)),
            _const_spec((n_tiles, LANES, LANES)),
            _const_spec((1, s5_w)),
            _const_spec((s5_w, s5_w)),
            _const_spec((1, s5_w)),
            _const_spec((1, hg_w)),
            _const_spec((1, hg_w)),
            _const_spec((s5_w, d_model)),
            _const_spec((hg_w, d_model)),
            _const_spec((d_model, d_model))],
        out_specs=pl.BlockSpec((1, ROWS, d_model), lambda g, t: (g, t, 0)),
        out_shape=jax.ShapeDtypeStruct((bsz // SEQS, seq * SEQS, d_model), F32),
        scratch_shapes=[pltpu.VMEM((d_model // LANES, ROWS, LANES), F32),
                        pltpu.VMEM((ROWS // 2, n_tiles * 2 * STATE_TILE), F32),
                        pltpu.VMEM((n_tiles, 2, SEQS, STATE_TILE), F32),
                        pltpu.VMEM((n_tiles, SEQS, LANES), F32),
                        pltpu.VMEM((n_heads, HG_HEAD_DIM, SEQS * HG_HEAD_DIM), F32),
                        pltpu.VMEM((ROWS, hg_w), F32),
                        pltpu.VMEM((4, ROWS, hg_w), F32)],
        compiler_params=cparams,
        name="mixer",
    )(x, g_mix[0][None], w_in[0].astype(BF16), lam2_re_t, lam2_im_t, bk, ck, gk,
      s5_d[0][None], w_glu[0].astype(BF16), b_glu[0][None], lb, hg_norm_gain[0][None],
      w_pa[0].astype(BF16), w_pb[0].astype(BF16), w_out[0].astype(BF16))

    return pl.pallas_call(
        _ffn_kernel,
        grid=grid,
        in_specs=[pl.BlockSpec((1, ROWS, d_model), lambda g, t: (g, t, 0)),
                  _const_spec((1, d_model)),
                  _const_spec((d_model, 2 * d_ff)),
                  _const_spec((CONV_W, 2 * d_ff)),
                  _const_spec((1, 2 * d_ff)),
                  _const_spec((d_ff, d_model)),
                  _const_spec((1, d_model))],
        out_specs=_seq_spec(d_model),
        out_shape=jax.ShapeDtypeStruct((bsz, seq, d_model), F32),
        scratch_shapes=[pltpu.VMEM((2 * SEQS, 2 * d_ff), F32),
                        pltpu.VMEM((d_model // LANES, ROWS, LANES), F32),
                        pltpu.VMEM((ROWS, d_ff), BF16)],
        compiler_params=cparams,
        name="ffn",
    )(x1, g_ffn[0][None], w_up[0].astype(BF16), w_conv[0], b_conv[0][None],
      w_down[0].astype(BF16), g_final[None])
```

```python
import functools

import jax
import jax.numpy as jnp
from jax import lax
from jax.experimental import pallas as pl
from jax.experimental.pallas import tpu as pltpu

EPS = 1e-6
S5_GROUP = 16
S5_STATE = 64
HG_HEAD_DIM = 128
CONV_W = 3

LANES = 128
SEQS = 8
TT = 64
ROWS = SEQS * TT
GROUPS_PER_TILE = LANES // S5_GROUP
STATE_TILE = GROUPS_PER_TILE * S5_STATE
HG_FAST_DECAY_LIMIT = 60.0
FFN_TT = 128
FFN_ROWS = SEQS * FFN_TT
FFN_CHUNK = 256
FFN_OUT_SPLIT = 2
MIX_HEAD_CHUNKS = 4
VMEM_LIMIT = 56 * 1024 * 1024

F32 = jnp.float32
BF16 = jnp.bfloat16


def _dot(a, b):
    return jnp.dot(a, b, preferred_element_type=F32)


def _dot_t0(a, b):
    return lax.dot_general(a, b, (((0,), (0,)), ((), ())), preferred_element_type=F32)


def _dot_t1(a, b):
    return lax.dot_general(a, b, (((1,), (1,)), ((), ())), preferred_element_type=F32)


def _rmsnorm(x, gain):
    return x * lax.rsqrt(jnp.mean(x * x, axis=-1, keepdims=True) + EPS) * gain


def _sigmoid(x):
    return jax.nn.sigmoid(x)


def _const_spec(shape):
    nd = len(shape)
    return pl.BlockSpec(shape, lambda *_: (0,) * nd, pipeline_mode=pl.Buffered(1))


def _seq_spec(width, tokens):
    return pl.BlockSpec((SEQS, tokens, width), lambda g, t: (g, t, 0))


def _interleave(seq_ref, slab_scr, t0, nt):
    n_slabs = slab_scr.shape[0]
    for b in range(SEQS):
        for s in range(n_slabs):
            slab_scr[s, pl.ds(t0 * SEQS + b, nt, stride=SEQS), :] = (
                seq_ref[b, t0:t0 + nt, s * LANES:(s + 1) * LANES])
    return jnp.concatenate([slab_scr[s, t0 * SEQS:(t0 + nt) * SEQS, :] for s in range(n_slabs)], axis=-1)


def _deinterleave(val, slab_scr, seq_ref, row0=0):
    n_slabs = slab_scr.shape[0]
    rows = val.shape[0]
    for s in range(n_slabs):
        slab_scr[s, row0:row0 + rows, :] = val[:, s * LANES:(s + 1) * LANES]
    for b in range(SEQS):
        seq_ref[b, row0 // SEQS:(row0 + rows) // SEQS, :] = jnp.concatenate(
            [slab_scr[s, pl.ds(row0 + b, rows // SEQS, stride=SEQS), :] for s in range(n_slabs)],
            axis=-1)


def _prep_kernel(a_re_ref, a_im_ref, log_dt_ref, bt_re_ref, bt_im_ref, c_re_ref, c_im_ref, lbl_ref,
                 lam2_re_ref, lam2_im_ref, bbt_re_ref, bbt_im_ref, lbt_re_ref, lbt_im_ref,
                 cl_re_ref, cl_im_ref, cb_ref, lb_ref, *, layer):
    a_re = a_re_ref[...]
    a_im = a_im_ref[...]
    dt = jnp.exp(log_dt_ref[...])
    mag = jnp.exp(a_re * dt)
    ang = a_im * dt
    lb_re = mag * jnp.cos(ang)
    lb_im = mag * jnp.sin(ang)
    den = a_re * a_re + a_im * a_im
    n_re = lb_re - 1.0
    n_im = lb_im
    co_re = (n_re * a_re + n_im * a_im) / den
    co_im = (n_im * a_re - n_re * a_im) / den
    lam2_re_ref[...] = lb_re * lb_re - lb_im * lb_im
    lam2_im_ref[...] = 2.0 * lb_re * lb_im
    bt_re = bt_re_ref[...]
    bt_im = bt_im_ref[...]
    bbt_re = co_re[:, None, :] * bt_re - co_im[:, None, :] * bt_im
    bbt_im = co_re[:, None, :] * bt_im + co_im[:, None, :] * bt_re
    bbt_re_ref[...] = bbt_re
    bbt_im_ref[...] = bbt_im
    l_re = lb_re[:, None, :]
    l_im = lb_im[:, None, :]
    lbt_re_ref[...] = l_re * bbt_re - l_im * bbt_im
    lbt_im_ref[...] = l_re * bbt_im + l_im * bbt_re
    c_re = c_re_ref[...]
    c_im = c_im_ref[...]
    cl_re_ref[...] = c_re * l_re - c_im * l_im
    cl_im_ref[...] = c_re * l_im + c_im * l_re
    for k in range(bt_re.shape[1]):
        cb_ref[k] = jnp.sum(c_re * bbt_re[:, k:k + 1, :] - c_im * bbt_im[:, k:k + 1, :], axis=-1)
    logits = lbl_ref[...]
    m = jnp.max(logits, axis=0, keepdims=True)
    e = jnp.exp(logits - m)
    tot = jnp.sum(e, axis=0, keepdims=True)
    lb_ref[...] = jnp.sum(e[: layer + 1], axis=0, keepdims=True) / tot


def _s5_branch(za, lam2_re_ref, lam2_im_ref, bk_ref, ck_ref, gk_ref, d_ref, bu_scr, st_scr, ycar_scr):
    n_tiles = bk_ref.shape[0]
    width = za.shape[-1]
    pairs = TT // 2
    prows = pairs * SEQS
    za3 = za.reshape(pairs, 2 * SEQS, width)
    u_even = za3[:, :SEQS, :].reshape(prows, width)
    u_odd = za3[:, SEQS:, :].reshape(prows, width)
    ue_bf = u_even.astype(BF16)
    uo_bf = u_odd.astype(BF16)
    for k in range(n_tiles):
        lt = slice(k * LANES, (k + 1) * LANES)
        bu_scr[:, k * 2 * STATE_TILE:(k + 1) * 2 * STATE_TILE] = _dot(
            jnp.concatenate([ue_bf[:, lt], uo_bf[:, lt]], axis=-1), bk_ref[k])
    for k in range(n_tiles):
        lr = lam2_re_ref[k]
        li = lam2_im_ref[k]
        c_re = k * 2 * STATE_TILE
        c_im = c_re + STATE_TILE
        xr = st_scr[k, 0]
        xi = st_scr[k, 1]
        for p in range(pairs):
            r = slice(p * SEQS, (p + 1) * SEQS)
            nr = lr * xr - li * xi + bu_scr[r, c_re:c_re + STATE_TILE]
            ni = lr * xi + li * xr + bu_scr[r, c_im:c_im + STATE_TILE]
            bu_scr[r, c_re:c_re + STATE_TILE] = nr
            bu_scr[r, c_im:c_im + STATE_TILE] = ni
            xr, xi = nr, ni
        st_scr[k, 0] = xr
        st_scr[k, 1] = xi
    y_even, y_odd = [], []
    for k in range(n_tiles):
        lt = slice(k * LANES, (k + 1) * LANES)
        xk = bu_scr[:, k * 2 * STATE_TILE:(k + 1) * 2 * STATE_TILE].astype(BF16)
        yy = _dot(xk, ck_ref[k])
        y_odd.append(yy[:, :LANES])
        nxt = yy[:, LANES:]
        y_even.append(jnp.concatenate([ycar_scr[k], nxt[:prows - SEQS]], axis=0)
                      + _dot(ue_bf[:, lt], gk_ref[k]))
        ycar_scr[k] = nxt[prows - SEQS:]
    d = d_ref[...]
    y_even = jnp.concatenate(y_even, axis=-1) + d * u_even
    y_odd = jnp.concatenate(y_odd, axis=-1) + d * u_odd
    return jnp.concatenate([y_even.reshape(pairs, SEQS, width), y_odd.reshape(pairs, SEQS, width)],
                           axis=1).reshape(ROWS, width)


def _hgrn_branch(zq, zf, zi, zg, lb_ref, gain_ref, hst_scr):
    n_heads = hst_scr.shape[0]
    width = zq.shape[-1]
    lb = lb_ref[...]
    f = lb + (1.0 - lb) * _sigmoid(zf)
    lf = jnp.log(f)
    kk = 1.0 - f
    acc = lf[0:SEQS]
    parts = [acc]
    for t in range(1, TT):
        acc = acc + lf[t * SEQS:(t + 1) * SEQS]
        parts.append(acc)
    bcum = jnp.concatenate(parts, axis=0)
    b_last = parts[-1]
    qa = zq * _sigmoid(zq) * (HG_HEAD_DIM ** -0.5)
    qh = qa * jnp.exp(bcum)
    kh = (kk * jnp.exp(-bcum)).astype(BF16)
    ke = (kk.reshape(TT, SEQS, width) * jnp.exp(b_last[None] - bcum.reshape(TT, SEQS, width))
          ).reshape(ROWS, width)
    dec = jnp.exp(b_last)
    iv = zi.astype(BF16)
    half = ROWS // 2
    dlt = (lax.broadcasted_iota(jnp.int32, (half, half), 0)
           - lax.broadcasted_iota(jnp.int32, (half, half), 1))
    same_seq = (dlt & (SEQS - 1)) == 0
    causal = same_seq & (dlt >= 0)
    seq_of_row = lax.broadcasted_iota(jnp.int32, (ROWS, HG_HEAD_DIM), 0) & (SEQS - 1)
    intra, inter = [], []
    for h in range(n_heads):
        sl = slice(h * HG_HEAD_DIM, (h + 1) * HG_HEAD_DIM)
        q_h = qh[:, sl]
        k_h = ke[:, sl]
        q_aug = jnp.concatenate([jnp.where(seq_of_row == b, q_h, 0.0) for b in range(SEQS)],
                                axis=-1).astype(BF16)
        k_aug = jnp.concatenate([jnp.where(seq_of_row == b, k_h, 0.0) for b in range(SEQS)],
                                axis=-1).astype(BF16)
        st = hst_scr[h]
        q_bf = q_h.astype(BF16)
        k_e, k_l = kh[:half, sl], kh[half:, sl]
        sc_ee = jnp.where(causal, _dot_t1(q_bf[:half], k_e), 0.0).astype(BF16)
        sc_le = jnp.where(same_seq, _dot_t1(q_bf[half:], k_e), 0.0).astype(BF16)
        sc_ll = jnp.where(causal, _dot_t1(q_bf[half:], k_l), 0.0).astype(BF16)
        intra.append(jnp.concatenate(
            [_dot(sc_ee, iv[:half, sl]),
             _dot(jnp.concatenate([sc_le, sc_ll], axis=-1), iv[:, sl])], axis=0))
        inter.append(_dot_t1(q_aug, st.astype(BF16)))
        dec_row = jnp.concatenate([dec[b:b + 1, sl] for b in range(SEQS)], axis=-1)
        hst_scr[h] = st * dec_row + _dot_t0(iv[:, sl], k_aug)
    o_inter = jnp.concatenate(inter, axis=-1)
    out_gate = zg * _sigmoid(zg)
    yb = _hgrn_head_norm(jnp.concatenate(intra, axis=-1) + o_inter, gain_ref) * out_gate
    return yb, (qa, kk, zi, bcum, b_last, o_inter, out_gate)


def _hgrn_head_norm(o, gain_ref):
    gain = gain_ref[...]
    return jnp.concatenate(
        [_rmsnorm(o[:, h * HG_HEAD_DIM:(h + 1) * HG_HEAD_DIM], gain[:, h * HG_HEAD_DIM:(h + 1) * HG_HEAD_DIM])
         for h in range(o.shape[-1] // HG_HEAD_DIM)], axis=-1)


def _hgrn_intra_exact(qa, kk, zi, bcum, fb_scr, oi_scr):
    width = qa.shape[-1]
    n_heads = width // HG_HEAD_DIM
    fb_scr[0] = qa
    fb_scr[1] = kk
    fb_scr[2] = zi
    fb_scr[3] = bcum

    def rows(j, t):
        return fb_scr[j, pl.ds(pl.multiple_of(t * SEQS, SEQS), SEQS), :]

    def outer(t, carry):
        q_t = rows(0, t)
        b_t = rows(3, t)

        def inner(s, acc):
            decay = jnp.where(s <= t, jnp.exp(jnp.minimum(b_t - rows(3, s), 0.0)), 0.0)
            p = q_t * rows(1, s) * decay
            i_s = rows(2, s)
            return acc + jnp.concatenate(
                [jnp.sum(p[:, h * HG_HEAD_DIM:(h + 1) * HG_HEAD_DIM], axis=-1, keepdims=True)
                 * i_s[:, h * HG_HEAD_DIM:(h + 1) * HG_HEAD_DIM] for h in range(n_heads)], axis=-1)

        acc = lax.fori_loop(0, TT, inner, jnp.zeros((SEQS, width), F32))
        oi_scr[pl.ds(pl.multiple_of(t * SEQS, SEQS), SEQS), :] = acc
        return carry

    lax.fori_loop(0, TT, outer, 0)


def _mixer_kernel(x_ref, gmix_ref, win_ref, lam2_re_ref, lam2_im_ref, bk_ref, ck_ref, gk_ref, d_ref,
                  wglu_ref, bglu_ref, lb_ref, gain_ref, wpa_ref, wpb_ref, wout_ref, out_ref,
                  slab_scr, bu_scr, sst_scr, ycar_scr, hst_scr, oi_scr, fb_scr, *, widths):
    s5_w, hg_w, d_model = widths

    @pl.when(pl.program_id(1) == 0)
    def _():
        sst_scr[...] = jnp.zeros_like(sst_scr)
        ycar_scr[...] = jnp.zeros_like(ycar_scr)
        hst_scr[...] = jnp.zeros_like(hst_scr)

    nt = TT // MIX_HEAD_CHUNKS
    us = [_rmsnorm(_interleave(x_ref, slab_scr, c * nt, nt), gmix_ref[...]).astype(BF16)
          for c in range(MIX_HEAD_CHUNKS)]
    za = jnp.concatenate([_dot(uc, win_ref[:, :s5_w]) for uc in us], axis=0)
    u = jnp.concatenate(us, axis=0)

    def proj(lo, width):
        return _dot(u, win_ref[:, lo:lo + width])

    y_s5 = _s5_branch(za, lam2_re_ref, lam2_im_ref, bk_ref, ck_ref, gk_ref, d_ref, bu_scr, sst_scr,
                      ycar_scr)
    o = s5_w
    yb, (qa, kk, zi, bcum, b_last, o_inter, out_gate) = _hgrn_branch(
        proj(o, hg_w), proj(o + hg_w, hg_w), proj(o + 2 * hg_w, hg_w), proj(o + 3 * hg_w, hg_w),
        lb_ref, gain_ref, hst_scr)
    o += 4 * hg_w
    gate_a = _sigmoid(proj(o, d_model))
    gate_b = _sigmoid(proj(o + d_model, d_model))

    ya = jax.nn.gelu(y_s5)
    ya = ya * _sigmoid(_dot(ya.astype(BF16), wglu_ref[...]) + bglu_ref[...])
    m_a = gate_a * _dot(ya.astype(BF16), wpa_ref[...])

    def finish(yb):
        m = m_a + gate_b * _dot(yb.astype(BF16), wpb_ref[...])
        x_res = jnp.concatenate([slab_scr[s] for s in range(slab_scr.shape[0])], axis=-1)
        out_ref[0] = x_res + _dot(m.astype(BF16), wout_ref[...])

    finish(yb)

    @pl.when(jnp.min(b_last) < -HG_FAST_DECAY_LIMIT)
    def _():
        _hgrn_intra_exact(qa, kk, zi, bcum, fb_scr, oi_scr)
        finish(_hgrn_head_norm(oi_scr[...] + o_inter, gain_ref) * out_gate)


def _conv_taps(h, prev, w, b):
    rows = h.shape[0]
    h1 = jnp.concatenate([prev[SEQS:], h[:rows - SEQS]], axis=0)
    h2 = jnp.concatenate([prev, h[:rows - 2 * SEQS]], axis=0)
    return h2 * w[0:1] + h1 * w[1:2] + h * w[2:3] + b


def _ffn_kernel(x_ref, gf_ref, wup_ref, wconv_ref, bconv_ref, wdown_ref, gfin_ref, out_ref,
                halo_scr, slab_scr, act_scr):
    d_ff = wdown_ref.shape[0]
    n_chunks = d_ff // FFN_CHUNK

    @pl.when(pl.program_id(1) == 0)
    def _():
        halo_scr[...] = jnp.zeros_like(halo_scr)

    x1 = x_ref[0]
    u = _rmsnorm(x1, gf_ref[...]).astype(BF16)
    for c in range(n_chunks):
        cols = []
        for part in range(2):
            j = part * n_chunks + c
            cs = slice(j * FFN_CHUNK, (j + 1) * FFN_CHUNK)
            h = _dot(u, wup_ref[:, cs])
            cols.append(_conv_taps(h, halo_scr[:, cs], wconv_ref[:, cs], bconv_ref[:, cs]))
            halo_scr[:, cs] = h[FFN_ROWS - 2 * SEQS:]
        gate, val = cols
        act_scr[:, c * FFN_CHUNK:(c + 1) * FFN_CHUNK] = (gate * _sigmoid(gate) * val).astype(BF16)
    blk = FFN_ROWS // FFN_OUT_SPLIT
    for i in range(FFN_OUT_SPLIT):
        rs = slice(i * blk, (i + 1) * blk)
        y = x1[rs] + _dot(act_scr[rs, :], wdown_ref[...])
        _deinterleave(_rmsnorm(y, gfin_ref[...]), slab_scr, out_ref, i * blk)


def _block_diag(vals):
    nt, g, a, b = vals.shape
    same = jnp.arange(g)[:, None, None, None] == jnp.arange(g)[None, None, :, None]
    return jnp.where(same, vals[:, :, :, None, :], 0).reshape(nt, g * a, g * b)


def kernel(x, g_mix, w_in, s5_a_re, s5_a_im, s5_log_dt, s5_b_re, s5_b_im, s5_c_re, s5_c_im, s5_d,
           w_glu, b_glu, hg_lb_logits, hg_norm_gain, w_pa, w_pb, w_out, g_ffn, w_up, w_conv,
           b_conv, w_down, g_final):
    depth = w_in.shape[0]
    assert depth == 1, "kernel is written for a single layer"
    bsz, seq, d_model = x.shape
    s5_w = s5_d.shape[-1]
    hg_w = hg_norm_gain.shape[-1]
    d_ff = w_down.shape[1]
    n_in = w_in.shape[-1]
    n_groups = s5_a_re.shape[1]
    n_tiles = s5_w // LANES
    n_heads = hg_w // HG_HEAD_DIM
    n_chunks = d_ff // FFN_CHUNK
    assert bsz % SEQS == 0 and seq % TT == 0 and seq % FFN_TT == 0 and s5_w % LANES == 0 and d_ff % FFN_CHUNK == 0
    assert d_model % LANES == 0 and n_in == s5_w + 4 * hg_w + 2 * d_model
    grid = (bsz // SEQS, seq // TT)
    cparams = pltpu.CompilerParams(vmem_limit_bytes=VMEM_LIMIT,
                                   dimension_semantics=("arbitrary", "arbitrary"))

    gps = jax.ShapeDtypeStruct((n_groups, S5_STATE), F32)
    ghps = jax.ShapeDtypeStruct((n_groups, S5_GROUP, S5_STATE), F32)
    lam2_re, lam2_im, bbt_re, bbt_im, lbt_re, lbt_im, cl_re, cl_im, cb, lb = pl.pallas_call(
        functools.partial(_prep_kernel, layer=0),
        out_shape=[gps] * 2 + [ghps] * 6
        + [jax.ShapeDtypeStruct((S5_GROUP, n_groups, S5_GROUP), F32),
           jax.ShapeDtypeStruct((1, hg_w), F32)],
        name="prep",
    )(s5_a_re[0], s5_a_im[0], s5_log_dt[0][:, None],
      jnp.swapaxes(s5_b_re[0], 1, 2), jnp.swapaxes(s5_b_im[0], 1, 2), s5_c_re[0], s5_c_im[0],
      hg_lb_logits)

    def tiles(v):
        return v.reshape((n_tiles, GROUPS_PER_TILE) + v.shape[1:])

    def bd(v):
        return _block_diag(tiles(v))

    def bd_t(v):
        return _block_diag(tiles(jnp.swapaxes(v, 1, 2)))
    lam2_re_t = jnp.broadcast_to(lam2_re.reshape(n_tiles, 1, STATE_TILE), (n_tiles, SEQS, STATE_TILE))
    lam2_im_t = jnp.broadcast_to(lam2_im.reshape(n_tiles, 1, STATE_TILE), (n_tiles, SEQS, STATE_TILE))
    bk = jnp.concatenate([jnp.concatenate([bd(lbt_re), bd(lbt_im)], axis=-1),
                          jnp.concatenate([bd(bbt_re), bd(bbt_im)], axis=-1)], axis=1).astype(BF16)
    ck = jnp.concatenate([jnp.concatenate([bd_t(s5_c_re[0]), bd_t(-s5_c_im[0])], axis=1),
                          jnp.concatenate([bd_t(cl_re), bd_t(-cl_im)], axis=1)], axis=-1).astype(BF16)
    gk = bd(jnp.swapaxes(cb, 0, 1)).astype(BF16)

    x1 = pl.pallas_call(
        functools.partial(_mixer_kernel, widths=(s5_w, hg_w, d_model)),
        grid=grid,
        in_specs=[
            _seq_spec(d_model, TT),
            _const_spec((1, d_model)),
            _const_spec((d_model, n_in)),
            _const_spec((n_tiles, SEQS, STATE_TILE)),
            _const_spec((n_tiles, SEQS, STATE_TILE)),
            _const_spec((n_tiles, 2 * LANES, 2 * STATE_TILE)),
            _const_spec((n_tiles, 2 * STATE_TILE, 2 * LANES)),
            _const_spec((n_tiles, LANES, LANES)),
            _const_spec((1, s5_w)),
            _const_spec((s5_w, s5_w)),
            _const_spec((1, s5_w)),
            _const_spec((1, hg_w)),
            _const_spec((1, hg_w)),
            _const_spec((s5_w, d_model)),
            _const_spec((hg_w, d_model)),
            _const_spec((d_model, d_model))],
        out_specs=pl.BlockSpec((1, ROWS, d_model), lambda g, t: (g, t, 0)),
        out_shape=jax.ShapeDtypeStruct((bsz // SEQS, seq * SEQS, d_model), F32),
        scratch_shapes=[pltpu.VMEM((d_model // LANES, ROWS, LANES), F32),
                        pltpu.VMEM((ROWS // 2, n_tiles * 2 * STATE_TILE), F32),
                        pltpu.VMEM((n_tiles, 2, SEQS, STATE_TILE), F32),
                        pltpu.VMEM((n_tiles, SEQS, LANES), F32),
                        pltpu.VMEM((n_heads, HG_HEAD_DIM, SEQS * HG_HEAD_DIM), F32),
                        pltpu.VMEM((ROWS, hg_w), F32),
                        pltpu.VMEM((4, ROWS, hg_w), F32)],
        compiler_params=cparams,
        name="mixer",
    )(x, g_mix[0][None], w_in[0].astype(BF16), lam2_re_t, lam2_im_t, bk, ck, gk,
      s5_d[0][None], w_glu[0].astype(BF16), b_glu[0][None], lb, hg_norm_gain[0][None],
      w_pa[0].astype(BF16), w_pb[0].astype(BF16), w_out[0].astype(BF16))

    return pl.pallas_call(
        _ffn_kernel,
        grid=(bsz // SEQS, seq // FFN_TT),
        in_specs=[pl.BlockSpec((1, FFN_ROWS, d_model), lambda g, t: (g, t, 0)),
                  _const_spec((1, d_model)),
                  _const_spec((d_model, 2 * d_ff)),
                  _const_spec((CONV_W, 2 * d_ff)),
                  _const_spec((1, 2 * d_ff)),
                  _const_spec((d_ff, d_model)),
                  _const_spec((1, d_model))],
        out_specs=_seq_spec(d_model, FFN_TT),
        out_shape=jax.ShapeDtypeStruct((bsz, seq, d_model), F32),
        scratch_shapes=[pltpu.VMEM((2 * SEQS, 2 * d_ff), F32),
                        pltpu.VMEM((d_model // LANES, FFN_ROWS, LANES), F32),
                        pltpu.VMEM((FFN_ROWS, d_ff), BF16)],
        compiler_params=cparams,
        name="ffn",
    )(x1, g_ffn[0][None], w_up[0].astype(BF16), w_conv[0], b_conv[0][None],
      w_down[0].astype(BF16), g_final[None])
```

```python
import functools

import jax
import jax.numpy as jnp
from jax import lax
from jax.experimental import pallas as pl
from jax.experimental.pallas import tpu as pltpu

EPS = 1e-6
S5_GROUP = 16
S5_STATE = 64
HG_HEAD_DIM = 128
CONV_W = 3

LANES = 128
SEQS = 8
TT = 64
ROWS = SEQS * TT
GROUPS_PER_TILE = LANES // S5_GROUP
STATE_TILE = GROUPS_PER_TILE * S5_STATE
HG_FAST_DECAY_LIMIT = 60.0
FFN_TT = 128
FFN_ROWS = SEQS * FFN_TT
FFN_CHUNK = 256
FFN_OUT_SPLIT = 2
MIX_HEAD_CHUNKS = 4
VMEM_LIMIT = 56 * 1024 * 1024

F32 = jnp.float32
BF16 = jnp.bfloat16


def _dot(a, b):
    return jnp.dot(a, b, preferred_element_type=F32)


def _dot_t0(a, b):
    return lax.dot_general(a, b, (((0,), (0,)), ((), ())), preferred_element_type=F32)


def _dot_t1(a, b):
    return lax.dot_general(a, b, (((1,), (1,)), ((), ())), preferred_element_type=F32)


def _rmsnorm(x, gain):
    return x * lax.rsqrt(jnp.mean(x * x, axis=-1, keepdims=True) + EPS) * gain


def _sigmoid(x):
    return jax.nn.sigmoid(x)


def _const_spec(shape):
    nd = len(shape)
    return pl.BlockSpec(shape, lambda *_: (0,) * nd, pipeline_mode=pl.Buffered(1))


def _seq_spec(width, tokens):
    return pl.BlockSpec((SEQS, tokens, width), lambda g, t: (g, t, 0))


def _interleave(seq_ref, slab_scr, t0, nt):
    n_slabs = slab_scr.shape[0]
    for b in range(SEQS):
        for s in range(n_slabs):
            slab_scr[s, pl.ds(t0 * SEQS + b, nt, stride=SEQS), :] = (
                seq_ref[b, t0:t0 + nt, s * LANES:(s + 1) * LANES])
    return jnp.concatenate([slab_scr[s, t0 * SEQS:(t0 + nt) * SEQS, :] for s in range(n_slabs)], axis=-1)


def _deinterleave(val, slab_scr, seq_ref, row0=0):
    n_slabs = slab_scr.shape[0]
    rows = val.shape[0]
    for s in range(n_slabs):
        slab_scr[s, row0:row0 + rows, :] = val[:, s * LANES:(s + 1) * LANES]
    for b in range(SEQS):
        seq_ref[b, row0 // SEQS:(row0 + rows) // SEQS, :] = jnp.concatenate(
            [slab_scr[s, pl.ds(row0 + b, rows // SEQS, stride=SEQS), :] for s in range(n_slabs)],
            axis=-1)


def _prep_kernel(a_re_ref, a_im_ref, log_dt_ref, bt_re_ref, bt_im_ref, c_re_ref, c_im_ref, lbl_ref,
                 lam2_re_ref, lam2_im_ref, bbt_re_ref, bbt_im_ref, lbt_re_ref, lbt_im_ref,
                 cl_re_ref, cl_im_ref, cb_ref, lb_ref, *, layer):
    a_re = a_re_ref[...]
    a_im = a_im_ref[...]
    dt = jnp.exp(log_dt_ref[...])
    mag = jnp.exp(a_re * dt)
    ang = a_im * dt
    lb_re = mag * jnp.cos(ang)
    lb_im = mag * jnp.sin(ang)
    den = a_re * a_re + a_im * a_im
    n_re = lb_re - 1.0
    n_im = lb_im
    co_re = (n_re * a_re + n_im * a_im) / den
    co_im = (n_im * a_re - n_re * a_im) / den
    lam2_re_ref[...] = lb_re * lb_re - lb_im * lb_im
    lam2_im_ref[...] = 2.0 * lb_re * lb_im
    bt_re = bt_re_ref[...]
    bt_im = bt_im_ref[...]
    bbt_re = co_re[:, None, :] * bt_re - co_im[:, None, :] * bt_im
    bbt_im = co_re[:, None, :] * bt_im + co_im[:, None, :] * bt_re
    bbt_re_ref[...] = bbt_re
    bbt_im_ref[...] = bbt_im
    l_re = lb_re[:, None, :]
    l_im = lb_im[:, None, :]
    lbt_re_ref[...] = l_re * bbt_re - l_im * bbt_im
    lbt_im_ref[...] = l_re * bbt_im + l_im * bbt_re
    c_re = c_re_ref[...]
    c_im = c_im_ref[...]
    cl_re_ref[...] = c_re * l_re - c_im * l_im
    cl_im_ref[...] = c_re * l_im + c_im * l_re
    for k in range(bt_re.shape[1]):
        cb_ref[k] = jnp.sum(c_re * bbt_re[:, k:k + 1, :] - c_im * bbt_im[:, k:k + 1, :], axis=-1)
    logits = lbl_ref[...]
    m = jnp.max(logits, axis=0, keepdims=True)
    e = jnp.exp(logits - m)
    tot = jnp.sum(e, axis=0, keepdims=True)
    lb_ref[...] = jnp.sum(e[: layer + 1], axis=0, keepdims=True) / tot


def _s5_branch(za, lam2_re_ref, lam2_im_ref, bk_ref, ck_ref, gk_ref, d_ref, bu_scr, st_scr, ycar_scr):
    n_tiles = bk_ref.shape[0]
    width = za.shape[-1]
    pairs = TT // 2
    prows = pairs * SEQS
    za3 = za.reshape(pairs, 2 * SEQS, width)
    u_even = za3[:, :SEQS, :].reshape(prows, width)
    u_odd = za3[:, SEQS:, :].reshape(prows, width)
    ue_bf = u_even.astype(BF16)
    uo_bf = u_odd.astype(BF16)
    for k in range(n_tiles):
        lt = slice(k * LANES, (k + 1) * LANES)
        bu_scr[:, k * 2 * STATE_TILE:(k + 1) * 2 * STATE_TILE] = _dot(
            jnp.concatenate([ue_bf[:, lt], uo_bf[:, lt]], axis=-1), bk_ref[k])
    for k in range(n_tiles):
        lr = lam2_re_ref[k]
        li = lam2_im_ref[k]
        c_re = k * 2 * STATE_TILE
        c_im = c_re + STATE_TILE
        xr = st_scr[k, 0]
        xi = st_scr[k, 1]
        for p in range(pairs):
            r = slice(p * SEQS, (p + 1) * SEQS)
            nr = lr * xr - li * xi + bu_scr[r, c_re:c_re + STATE_TILE]
            ni = lr * xi + li * xr + bu_scr[r, c_im:c_im + STATE_TILE]
            bu_scr[r, c_re:c_re + STATE_TILE] = nr
            bu_scr[r, c_im:c_im + STATE_TILE] = ni
            xr, xi = nr, ni
        st_scr[k, 0] = xr
        st_scr[k, 1] = xi
    y_even, y_odd = [], []
    for k in range(n_tiles):
        lt = slice(k * LANES, (k + 1) * LANES)
        xk = bu_scr[:, k * 2 * STATE_TILE:(k + 1) * 2 * STATE_TILE].astype(BF16)
        yy = _dot(xk, ck_ref[k])
        y_odd.append(yy[:, :LANES])
        nxt = yy[:, LANES:]
        y_even.append(jnp.concatenate([ycar_scr[k], nxt[:prows - SEQS]], axis=0)
                      + _dot(ue_bf[:, lt], gk_ref[k]))
        ycar_scr[k] = nxt[prows - SEQS:]
    d = d_ref[...]
    y_even = jnp.concatenate(y_even, axis=-1) + d * u_even
    y_odd = jnp.concatenate(y_odd, axis=-1) + d * u_odd
    return jnp.concatenate([y_even.reshape(pairs, SEQS, width), y_odd.reshape(pairs, SEQS, width)],
                           axis=1).reshape(ROWS, width)


def _hgrn_branch(zq, zf, zi, zg, lb_ref, gain_ref, hst_scr):
    n_heads = hst_scr.shape[0]
    width = zq.shape[-1]
    lb = lb_ref[...]
    f = lb + (1.0 - lb) * _sigmoid(zf)
    lf = jnp.log(f)
    kk = 1.0 - f
    acc = lf[0:SEQS]
    parts = [acc]
    for t in range(1, TT):
        acc = acc + lf[t * SEQS:(t + 1) * SEQS]
        parts.append(acc)
    bcum = jnp.concatenate(parts, axis=0)
    b_last = parts[-1]
    qa = zq * _sigmoid(zq) * (HG_HEAD_DIM ** -0.5)
    qh = qa * jnp.exp(bcum)
    kh = (kk * jnp.exp(-bcum)).astype(BF16)
    ke = (kk.reshape(TT, SEQS, width) * jnp.exp(b_last[None] - bcum.reshape(TT, SEQS, width))
          ).reshape(ROWS, width)
    dec = jnp.exp(b_last)
    half = ROWS // 2
    dlt = (lax.broadcasted_iota(jnp.int32, (half, half), 1)
           - lax.broadcasted_iota(jnp.int32, (half, half), 0))
    same_seq = (dlt & (SEQS - 1)) == 0
    causal = same_seq & (dlt >= 0)
    seq_of_row = lax.broadcasted_iota(jnp.int32, (ROWS, HG_HEAD_DIM), 0) & (SEQS - 1)
    seq_of_col = lax.broadcasted_iota(jnp.int32, (HG_HEAD_DIM, ROWS), 1) & (SEQS - 1)
    intra, inter = [], []
    for h in range(n_heads):
        sl = slice(h * HG_HEAD_DIM, (h + 1) * HG_HEAD_DIM)
        q_t = qh[:, sl].T
        i_t = zi[:, sl].T.astype(BF16)
        q_tb = q_t.astype(BF16)
        k_e, k_l = kh[:half, sl], kh[half:, sl]
        sc_ee = jnp.where(causal, _dot(k_e, q_tb[:, :half]), 0.0).astype(BF16)
        sc_el = jnp.where(same_seq, _dot(k_e, q_tb[:, half:]), 0.0).astype(BF16)
        sc_ll = jnp.where(causal, _dot(k_l, q_tb[:, half:]), 0.0).astype(BF16)
        intra_t = jnp.concatenate(
            [_dot(i_t[:, :half], sc_ee),
             _dot(i_t, jnp.concatenate([sc_el, sc_ll], axis=0))], axis=1)
        q_aug_t = jnp.concatenate([jnp.where(seq_of_col == b, q_t, 0.0) for b in range(SEQS)],
                                  axis=0).astype(BF16)
        k_aug = jnp.concatenate([jnp.where(seq_of_row == b, ke[:, sl], 0.0) for b in range(SEQS)],
                                axis=-1).astype(BF16)
        st = hst_scr[h]
        inter_t = _dot(st.astype(BF16), q_aug_t)
        dec_row = jnp.concatenate([dec[b:b + 1, sl] for b in range(SEQS)], axis=-1)
        hst_scr[h] = st * dec_row + _dot(i_t, k_aug)
        intra.append(intra_t.T)
        inter.append(inter_t.T)
    o_inter = jnp.concatenate(inter, axis=-1)
    out_gate = zg * _sigmoid(zg)
    yb = _hgrn_head_norm(jnp.concatenate(intra, axis=-1) + o_inter, gain_ref) * out_gate
    return yb, (qa, kk, zi, bcum, b_last, o_inter, out_gate)


def _hgrn_head_norm(o, gain_ref):
    gain = gain_ref[...]
    return jnp.concatenate(
        [_rmsnorm(o[:, h * HG_HEAD_DIM:(h + 1) * HG_HEAD_DIM], gain[:, h * HG_HEAD_DIM:(h + 1) * HG_HEAD_DIM])
         for h in range(o.shape[-1] // HG_HEAD_DIM)], axis=-1)


def _hgrn_intra_exact(qa, kk, zi, bcum, fb_scr, oi_scr):
    width = qa.shape[-1]
    n_heads = width // HG_HEAD_DIM
    fb_scr[0] = qa
    fb_scr[1] = kk
    fb_scr[2] = zi
    fb_scr[3] = bcum

    def rows(j, t):
        return fb_scr[j, pl.ds(pl.multiple_of(t * SEQS, SEQS), SEQS), :]

    def outer(t, carry):
        q_t = rows(0, t)
        b_t = rows(3, t)

        def inner(s, acc):
            decay = jnp.where(s <= t, jnp.exp(jnp.minimum(b_t - rows(3, s), 0.0)), 0.0)
            p = q_t * rows(1, s) * decay
            i_s = rows(2, s)
            return acc + jnp.concatenate(
                [jnp.sum(p[:, h * HG_HEAD_DIM:(h + 1) * HG_HEAD_DIM], axis=-1, keepdims=True)
                 * i_s[:, h * HG_HEAD_DIM:(h + 1) * HG_HEAD_DIM] for h in range(n_heads)], axis=-1)

        acc = lax.fori_loop(0, TT, inner, jnp.zeros((SEQS, width), F32))
        oi_scr[pl.ds(pl.multiple_of(t * SEQS, SEQS), SEQS), :] = acc
        return carry

    lax.fori_loop(0, TT, outer, 0)


def _mixer_kernel(x_ref, gmix_ref, win_ref, lam2_re_ref, lam2_im_ref, bk_ref, ck_ref, gk_ref, d_ref,
                  wglu_ref, bglu_ref, lb_ref, gain_ref, wpa_ref, wpb_ref, wout_ref, out_ref,
                  slab_scr, bu_scr, sst_scr, ycar_scr, hst_scr, oi_scr, fb_scr, *, widths):
    s5_w, hg_w, d_model = widths

    @pl.when(pl.program_id(1) == 0)
    def _():
        sst_scr[...] = jnp.zeros_like(sst_scr)
        ycar_scr[...] = jnp.zeros_like(ycar_scr)
        hst_scr[...] = jnp.zeros_like(hst_scr)

    nt = TT // MIX_HEAD_CHUNKS
    us = [_rmsnorm(_interleave(x_ref, slab_scr, c * nt, nt), gmix_ref[...]).astype(BF16)
          for c in range(MIX_HEAD_CHUNKS)]
    za = jnp.concatenate([_dot(uc, win_ref[:, :s5_w]) for uc in us], axis=0)
    u = jnp.concatenate(us, axis=0)

    def proj(lo, width):
        return _dot(u, win_ref[:, lo:lo + width])

    y_s5 = _s5_branch(za, lam2_re_ref, lam2_im_ref, bk_ref, ck_ref, gk_ref, d_ref, bu_scr, sst_scr,
                      ycar_scr)
    o = s5_w
    yb, (qa, kk, zi, bcum, b_last, o_inter, out_gate) = _hgrn_branch(
        proj(o, hg_w), proj(o + hg_w, hg_w), proj(o + 2 * hg_w, hg_w), proj(o + 3 * hg_w, hg_w),
        lb_ref, gain_ref, hst_scr)
    o += 4 * hg_w
    gate_a = _sigmoid(proj(o, d_model))
    gate_b = _sigmoid(proj(o + d_model, d_model))

    ya = jax.nn.gelu(y_s5)
    ya = ya * _sigmoid(_dot(ya.astype(BF16), wglu_ref[...]) + bglu_ref[...])
    m_a = gate_a * _dot(ya.astype(BF16), wpa_ref[...])

    def finish(yb):
        m = m_a + gate_b * _dot(yb.astype(BF16), wpb_ref[...])
        x_res = jnp.concatenate([slab_scr[s] for s in range(slab_scr.shape[0])], axis=-1)
        out_ref[0] = x_res + _dot(m.astype(BF16), wout_ref[...])

    finish(yb)

    @pl.when(jnp.min(b_last) < -HG_FAST_DECAY_LIMIT)
    def _():
        _hgrn_intra_exact(qa, kk, zi, bcum, fb_scr, oi_scr)
        finish(_hgrn_head_norm(oi_scr[...] + o_inter, gain_ref) * out_gate)


def _conv_taps(h, prev, w, b):
    rows = h.shape[0]
    h1 = jnp.concatenate([prev[SEQS:], h[:rows - SEQS]], axis=0)
    h2 = jnp.concatenate([prev, h[:rows - 2 * SEQS]], axis=0)
    return h2 * w[0:1] + h1 * w[1:2] + h * w[2:3] + b


def _ffn_kernel(x_ref, gf_ref, wup_ref, wconv_ref, bconv_ref, wdown_ref, gfin_ref, out_ref,
                halo_scr, slab_scr, act_scr):
    d_ff = wdown_ref.shape[0]
    n_chunks = d_ff // FFN_CHUNK

    @pl.when(pl.program_id(1) == 0)
    def _():
        halo_scr[...] = jnp.zeros_like(halo_scr)

    x1 = x_ref[0]
    u = _rmsnorm(x1, gf_ref[...]).astype(BF16)
    for c in range(n_chunks):
        cols = []
        for part in range(2):
            j = part * n_chunks + c
            cs = slice(j * FFN_CHUNK, (j + 1) * FFN_CHUNK)
            h = _dot(u, wup_ref[:, cs])
            cols.append(_conv_taps(h, halo_scr[:, cs], wconv_ref[:, cs], bconv_ref[:, cs]))
            halo_scr[:, cs] = h[FFN_ROWS - 2 * SEQS:]
        gate, val = cols
        act_scr[:, c * FFN_CHUNK:(c + 1) * FFN_CHUNK] = (gate * _sigmoid(gate) * val).astype(BF16)
    blk = FFN_ROWS // FFN_OUT_SPLIT
    for i in range(FFN_OUT_SPLIT):
        rs = slice(i * blk, (i + 1) * blk)
        y = x1[rs] + _dot(act_scr[rs, :], wdown_ref[...])
        _deinterleave(_rmsnorm(y, gfin_ref[...]), slab_scr, out_ref, i * blk)


def _block_diag(vals):
    nt, g, a, b = vals.shape
    same = jnp.arange(g)[:, None, None, None] == jnp.arange(g)[None, None, :, None]
    return jnp.where(same, vals[:, :, :, None, :], 0).reshape(nt, g * a, g * b)


def kernel(x, g_mix, w_in, s5_a_re, s5_a_im, s5_log_dt, s5_b_re, s5_b_im, s5_c_re, s5_c_im, s5_d,
           w_glu, b_glu, hg_lb_logits, hg_norm_gain, w_pa, w_pb, w_out, g_ffn, w_up, w_conv,
           b_conv, w_down, g_final):
    depth = w_in.shape[0]
    assert depth == 1, "kernel is written for a single layer"
    bsz, seq, d_model = x.shape
    s5_w = s5_d.shape[-1]
    hg_w = hg_norm_gain.shape[-1]
    d_ff = w_down.shape[1]
    n_in = w_in.shape[-1]
    n_groups = s5_a_re.shape[1]
    n_tiles = s5_w // LANES
    n_heads = hg_w // HG_HEAD_DIM
    n_chunks = d_ff // FFN_CHUNK
    assert bsz % SEQS == 0 and seq % TT == 0 and seq % FFN_TT == 0 and s5_w % LANES == 0 and d_ff % FFN_CHUNK == 0
    assert d_model % LANES == 0 and n_in == s5_w + 4 * hg_w + 2 * d_model
    grid = (bsz // SEQS, seq // TT)
    cparams = pltpu.CompilerParams(vmem_limit_bytes=VMEM_LIMIT,
                                   dimension_semantics=("arbitrary", "arbitrary"))

    gps = jax.ShapeDtypeStruct((n_groups, S5_STATE), F32)
    ghps = jax.ShapeDtypeStruct((n_groups, S5_GROUP, S5_STATE), F32)
    lam2_re, lam2_im, bbt_re, bbt_im, lbt_re, lbt_im, cl_re, cl_im, cb, lb = pl.pallas_call(
        functools.partial(_prep_kernel, layer=0),
        out_shape=[gps] * 2 + [ghps] * 6
        + [jax.ShapeDtypeStruct((S5_GROUP, n_groups, S5_GROUP), F32),
           jax.ShapeDtypeStruct((1, hg_w), F32)],
        name="prep",
    )(s5_a_re[0], s5_a_im[0], s5_log_dt[0][:, None],
      jnp.swapaxes(s5_b_re[0], 1, 2), jnp.swapaxes(s5_b_im[0], 1, 2), s5_c_re[0], s5_c_im[0],
      hg_lb_logits)

    def tiles(v):
        return v.reshape((n_tiles, GROUPS_PER_TILE) + v.shape[1:])

    def bd(v):
        return _block_diag(tiles(v))

    def bd_t(v):
        return _block_diag(tiles(jnp.swapaxes(v, 1, 2)))
    lam2_re_t = jnp.broadcast_to(lam2_re.reshape(n_tiles, 1, STATE_TILE), (n_tiles, SEQS, STATE_TILE))
    lam2_im_t = jnp.broadcast_to(lam2_im.reshape(n_tiles, 1, STATE_TILE), (n_tiles, SEQS, STATE_TILE))
    bk = jnp.concatenate([jnp.concatenate([bd(lbt_re), bd(lbt_im)], axis=-1),
                          jnp.concatenate([bd(bbt_re), bd(bbt_im)], axis=-1)], axis=1).astype(BF16)
    ck = jnp.concatenate([jnp.concatenate([bd_t(s5_c_re[0]), bd_t(-s5_c_im[0])], axis=1),
                          jnp.concatenate([bd_t(cl_re), bd_t(-cl_im)], axis=1)], axis=-1).astype(BF16)
    gk = bd(jnp.swapaxes(cb, 0, 1)).astype(BF16)

    x1 = pl.pallas_call(
        functools.partial(_mixer_kernel, widths=(s5_w, hg_w, d_model)),
        grid=grid,
        in_specs=[
            _seq_spec(d_model, TT),
            _const_spec((1, d_model)),
            _const_spec((d_model, n_in)),
            _const_spec((n_tiles, SEQS, STATE_TILE)),
            _const_spec((n_tiles, SEQS, STATE_TILE)),
            _const_spec((n_tiles, 2 * LANES, 2 * STATE_TILE)),
            _const_spec((n_tiles, 2 * STATE_TILE, 2 * LANES)),
            _const_spec((n_tiles, LANES, LANES)),
            _const_spec((1, s5_w)),
            _const_spec((s5_w, s5_w)),
            _const_spec((1, s5_w)),
            _const_spec((1, hg_w)),
            _const_spec((1, hg_w)),
            _const_spec((s5_w, d_model)),
            _const_spec((hg_w, d_model)),
            _const_spec((d_model, d_model))],
        out_specs=pl.BlockSpec((1, ROWS, d_model), lambda g, t: (g, t, 0)),
        out_shape=jax.ShapeDtypeStruct((bsz // SEQS, seq * SEQS, d_model), F32),
        scratch_shapes=[pltpu.VMEM((d_model // LANES, ROWS, LANES), F32),
                        pltpu.VMEM((ROWS // 2, n_tiles * 2 * STATE_TILE), F32),
                        pltpu.VMEM((n_tiles, 2, SEQS, STATE_TILE), F32),
                        pltpu.VMEM((n_tiles, SEQS, LANES), F32),
                        pltpu.VMEM((n_heads, HG_HEAD_DIM, SEQS * HG_HEAD_DIM), F32),
                        pltpu.VMEM((ROWS, hg_w), F32),
                        pltpu.VMEM((4, ROWS, hg_w), F32)],
        compiler_params=cparams,
        name="mixer",
    )(x, g_mix[0][None], w_in[0].astype(BF16), lam2_re_t, lam2_im_t, bk, ck, gk,
      s5_d[0][None], w_glu[0].astype(BF16), b_glu[0][None], lb, hg_norm_gain[0][None],
      w_pa[0].astype(BF16), w_pb[0].astype(BF16), w_out[0].astype(BF16))

    return pl.pallas_call(
        _ffn_kernel,
        grid=(bsz // SEQS, seq // FFN_TT),
        in_specs=[pl.BlockSpec((1, FFN_ROWS, d_model), lambda g, t: (g, t, 0)),
                  _const_spec((1, d_model)),
                  _const_spec((d_model, 2 * d_ff)),
                  _const_spec((CONV_W, 2 * d_ff)),
                  _const_spec((1, 2 * d_ff)),
                  _const_spec((d_ff, d_model)),
                  _const_spec((1, d_model))],
        out_specs=_seq_spec(d_model, FFN_TT),
        out_shape=jax.ShapeDtypeStruct((bsz, seq, d_model), F32),
        scratch_shapes=[pltpu.VMEM((2 * SEQS, 2 * d_ff), F32),
                        pltpu.VMEM((d_model // LANES, FFN_ROWS, LANES), F32),
                        pltpu.VMEM((FFN_ROWS, d_ff), BF16)],
        compiler_params=cparams,
        name="ffn",
    )(x1, g_ffn[0][None], w_up[0].astype(BF16), w_conv[0], b_conv[0][None],
      w_down[0].astype(BF16), g_final[None])
```

```python
import functools

import jax
import jax.numpy as jnp
from jax import lax
from jax.experimental import pallas as pl
from jax.experimental.pallas import tpu as pltpu

EPS = 1e-6
S5_GROUP = 16
S5_STATE = 64
HG_HEAD_DIM = 128
CONV_W = 3

LANES = 128
SEQS = 8
TT = 64
ROWS = SEQS * TT
GROUPS_PER_TILE = LANES // S5_GROUP
STATE_TILE = GROUPS_PER_TILE * S5_STATE
HG_FAST_DECAY_LIMIT = 60.0
CAST_ROWS_MIN = 16
FFN_TT = 128
FFN_ROWS = SEQS * FFN_TT
FFN_CHUNK = 256
FFN_OUT_SPLIT = 2
MIX_HEAD_CHUNKS = 4
VMEM_LIMIT = 56 * 1024 * 1024

F32 = jnp.float32
BF16 = jnp.bfloat16


def _dot(a, b):
    return jnp.dot(a, b, preferred_element_type=F32)


def _dot_t0(a, b):
    return lax.dot_general(a, b, (((0,), (0,)), ((), ())), preferred_element_type=F32)


def _dot_t1(a, b):
    return lax.dot_general(a, b, (((1,), (1,)), ((), ())), preferred_element_type=F32)


def _rmsnorm(x, gain):
    return x * lax.rsqrt(jnp.mean(x * x, axis=-1, keepdims=True) + EPS) * gain


def _sigmoid(x):
    return jax.nn.sigmoid(x)


def _const_spec(shape):
    nd = len(shape)
    return pl.BlockSpec(shape, lambda *_: (0,) * nd, pipeline_mode=pl.Buffered(1))


def _seq_spec(width, tokens):
    return pl.BlockSpec((SEQS, tokens, width), lambda g, t: (g, t, 0))


def _interleave(seq_ref, slab_scr, t0, nt):
    n_slabs = slab_scr.shape[0]
    for b in range(SEQS):
        for s in range(n_slabs):
            slab_scr[s, pl.ds(t0 * SEQS + b, nt, stride=SEQS), :] = (
                seq_ref[b, t0:t0 + nt, s * LANES:(s + 1) * LANES])
    return jnp.concatenate([slab_scr[s, t0 * SEQS:(t0 + nt) * SEQS, :] for s in range(n_slabs)], axis=-1)


def _deinterleave(val, slab_scr, seq_ref, row0=0):
    n_slabs = slab_scr.shape[0]
    rows = val.shape[0]
    for s in range(n_slabs):
        slab_scr[s, row0:row0 + rows, :] = val[:, s * LANES:(s + 1) * LANES]
    for b in range(SEQS):
        seq_ref[b, row0 // SEQS:(row0 + rows) // SEQS, :] = jnp.concatenate(
            [slab_scr[s, pl.ds(row0 + b, rows // SEQS, stride=SEQS), :] for s in range(n_slabs)],
            axis=-1)


def _prep_kernel(a_re_ref, a_im_ref, log_dt_ref, bt_re_ref, bt_im_ref, c_re_ref, c_im_ref, lbl_ref,
                 lam2_re_ref, lam2_im_ref, bbt_re_ref, bbt_im_ref, lbt_re_ref, lbt_im_ref,
                 cl_re_ref, cl_im_ref, cb_ref, lb_ref, *, layer):
    a_re = a_re_ref[...]
    a_im = a_im_ref[...]
    dt = jnp.exp(log_dt_ref[...])
    mag = jnp.exp(a_re * dt)
    ang = a_im * dt
    lb_re = mag * jnp.cos(ang)
    lb_im = mag * jnp.sin(ang)
    den = a_re * a_re + a_im * a_im
    n_re = lb_re - 1.0
    n_im = lb_im
    co_re = (n_re * a_re + n_im * a_im) / den
    co_im = (n_im * a_re - n_re * a_im) / den
    lam2_re_ref[...] = lb_re * lb_re - lb_im * lb_im
    lam2_im_ref[...] = 2.0 * lb_re * lb_im
    bt_re = bt_re_ref[...]
    bt_im = bt_im_ref[...]
    bbt_re = co_re[:, None, :] * bt_re - co_im[:, None, :] * bt_im
    bbt_im = co_re[:, None, :] * bt_im + co_im[:, None, :] * bt_re
    bbt_re_ref[...] = bbt_re
    bbt_im_ref[...] = bbt_im
    l_re = lb_re[:, None, :]
    l_im = lb_im[:, None, :]
    lbt_re_ref[...] = l_re * bbt_re - l_im * bbt_im
    lbt_im_ref[...] = l_re * bbt_im + l_im * bbt_re
    c_re = c_re_ref[...]
    c_im = c_im_ref[...]
    cl_re_ref[...] = c_re * l_re - c_im * l_im
    cl_im_ref[...] = c_re * l_im + c_im * l_re
    for k in range(bt_re.shape[1]):
        cb_ref[k] = jnp.sum(c_re * bbt_re[:, k:k + 1, :] - c_im * bbt_im[:, k:k + 1, :], axis=-1)
    logits = lbl_ref[...]
    m = jnp.max(logits, axis=0, keepdims=True)
    e = jnp.exp(logits - m)
    tot = jnp.sum(e, axis=0, keepdims=True)
    lb_ref[...] = jnp.sum(e[: layer + 1], axis=0, keepdims=True) / tot


def _s5_branch(za, lam2_re_ref, lam2_im_ref, bk_ref, ck_ref, gk_ref, d_ref, bu_scr, st_scr, ycar_scr):
    n_tiles = bk_ref.shape[0]
    width = za.shape[-1]
    pairs = TT // 2
    prows = pairs * SEQS
    za3 = za.reshape(pairs, 2 * SEQS, width)
    u_even = za3[:, :SEQS, :].reshape(prows, width)
    u_odd = za3[:, SEQS:, :].reshape(prows, width)
    ue_bf = u_even.astype(BF16)
    uo_bf = u_odd.astype(BF16)
    for k in range(n_tiles):
        lt = slice(k * LANES, (k + 1) * LANES)
        bu_scr[:, k * 2 * STATE_TILE:(k + 1) * 2 * STATE_TILE] = _dot(
            jnp.concatenate([ue_bf[:, lt], uo_bf[:, lt]], axis=-1), bk_ref[k])
    for k in range(n_tiles):
        lr = lam2_re_ref[k]
        li = lam2_im_ref[k]
        c_re = k * 2 * STATE_TILE
        c_im = c_re + STATE_TILE
        xr = st_scr[k, 0]
        xi = st_scr[k, 1]
        for p in range(pairs):
            r = slice(p * SEQS, (p + 1) * SEQS)
            nr = lr * xr - li * xi + bu_scr[r, c_re:c_re + STATE_TILE]
            ni = lr * xi + li * xr + bu_scr[r, c_im:c_im + STATE_TILE]
            bu_scr[r, c_re:c_re + STATE_TILE] = nr
            bu_scr[r, c_im:c_im + STATE_TILE] = ni
            xr, xi = nr, ni
        st_scr[k, 0] = xr
        st_scr[k, 1] = xi
    y_even, y_odd = [], []
    for k in range(n_tiles):
        lt = slice(k * LANES, (k + 1) * LANES)
        xk = bu_scr[:, k * 2 * STATE_TILE:(k + 1) * 2 * STATE_TILE].astype(BF16)
        yy = _dot(xk, ck_ref[k])
        y_odd.append(yy[:, :LANES])
        nxt = yy[:, LANES:]
        y_even.append(jnp.concatenate([ycar_scr[k], nxt[:prows - SEQS]], axis=0)
                      + _dot(ue_bf[:, lt], gk_ref[k]))
        ycar_scr[k] = nxt[prows - SEQS:]
    d = d_ref[...]
    y_even = jnp.concatenate(y_even, axis=-1) + d * u_even
    y_odd = jnp.concatenate(y_odd, axis=-1) + d * u_odd
    return jnp.concatenate([y_even.reshape(pairs, SEQS, width), y_odd.reshape(pairs, SEQS, width)],
                           axis=1).reshape(ROWS, width)


def _hgrn_branch(zq, zf, zi, zg, lb_ref, gain_ref, hst_scr):
    n_heads = hst_scr.shape[0]
    width = zq.shape[-1]
    lb = lb_ref[...]
    f = lb + (1.0 - lb) * _sigmoid(zf)
    lf = jnp.log(f)
    kk = 1.0 - f
    acc = lf[0:SEQS]
    parts = [acc]
    for t in range(1, TT):
        acc = acc + lf[t * SEQS:(t + 1) * SEQS]
        parts.append(acc)
    bcum = jnp.concatenate(parts, axis=0)
    b_last = parts[-1]
    qa = zq * _sigmoid(zq) * (HG_HEAD_DIM ** -0.5)
    qh = qa * jnp.exp(bcum)
    kh = (kk * jnp.exp(-bcum)).astype(BF16)
    ke = (kk.reshape(TT, SEQS, width) * jnp.exp(b_last[None] - bcum.reshape(TT, SEQS, width))
          ).reshape(ROWS, width)
    dec = jnp.exp(b_last)
    half = ROWS // 2
    dlt = (lax.broadcasted_iota(jnp.int32, (half, half), 1)
           - lax.broadcasted_iota(jnp.int32, (half, half), 0))
    same_seq = (dlt & (SEQS - 1)) == 0
    causal = same_seq & (dlt >= 0)
    seq_of_row = lax.broadcasted_iota(jnp.int32, (ROWS, HG_HEAD_DIM), 0) & (SEQS - 1)
    seq_of_col = lax.broadcasted_iota(jnp.int32, (HG_HEAD_DIM, ROWS), 1) & (SEQS - 1)
    intra, inter = [], []
    for h in range(n_heads):
        sl = slice(h * HG_HEAD_DIM, (h + 1) * HG_HEAD_DIM)
        q_t = qh[:, sl].T
        i_t = zi[:, sl].T.astype(BF16)
        q_tb = q_t.astype(BF16)
        k_e, k_l = kh[:half, sl], kh[half:, sl]
        sc_ee = jnp.where(causal, _dot(k_e, q_tb[:, :half]), 0.0).astype(BF16)
        sc_el = jnp.where(same_seq, _dot(k_e, q_tb[:, half:]), 0.0).astype(BF16)
        sc_ll = jnp.where(causal, _dot(k_l, q_tb[:, half:]), 0.0).astype(BF16)
        intra_t = jnp.concatenate(
            [_dot(i_t[:, :half], sc_ee),
             _dot(i_t, jnp.concatenate([sc_el, sc_ll], axis=0))], axis=1)
        q_aug_t = jnp.concatenate([jnp.where(seq_of_col == b, q_t, 0.0) for b in range(SEQS)],
                                  axis=0).astype(BF16)
        k_aug = jnp.concatenate([jnp.where(seq_of_row == b, ke[:, sl], 0.0) for b in range(SEQS)],
                                axis=-1).astype(BF16)
        st = hst_scr[h]
        inter_t = _dot(st.astype(BF16), q_aug_t)
        dec_row = jnp.concatenate([dec[b:b + 1, sl] for b in range(SEQS)], axis=-1)
        hst_scr[h] = st * dec_row + _dot(i_t, k_aug)
        intra.append(intra_t.T)
        inter.append(inter_t.T)
    o_inter = jnp.concatenate(inter, axis=-1)
    out_gate = zg * _sigmoid(zg)
    yb = _hgrn_head_norm(jnp.concatenate(intra, axis=-1) + o_inter, gain_ref) * out_gate
    return yb, (qa, kk, zi, bcum, b_last, o_inter, out_gate)


def _hgrn_head_norm(o, gain_ref):
    gain = gain_ref[...]
    return jnp.concatenate(
        [_rmsnorm(o[:, h * HG_HEAD_DIM:(h + 1) * HG_HEAD_DIM], gain[:, h * HG_HEAD_DIM:(h + 1) * HG_HEAD_DIM])
         for h in range(o.shape[-1] // HG_HEAD_DIM)], axis=-1)


def _hgrn_intra_exact(qa, kk, zi, bcum, fb_scr, oi_scr):
    width = qa.shape[-1]
    n_heads = width // HG_HEAD_DIM
    fb_scr[0] = qa
    fb_scr[1] = kk
    fb_scr[2] = zi
    fb_scr[3] = bcum

    def rows(j, t):
        return fb_scr[j, pl.ds(pl.multiple_of(t * SEQS, SEQS), SEQS), :]

    def outer(t, carry):
        q_t = rows(0, t)
        b_t = rows(3, t)

        def inner(s, acc):
            decay = jnp.where(s <= t, jnp.exp(jnp.minimum(b_t - rows(3, s), 0.0)), 0.0)
            p = q_t * rows(1, s) * decay
            i_s = rows(2, s)
            return acc + jnp.concatenate(
                [jnp.sum(p[:, h * HG_HEAD_DIM:(h + 1) * HG_HEAD_DIM], axis=-1, keepdims=True)
                 * i_s[:, h * HG_HEAD_DIM:(h + 1) * HG_HEAD_DIM] for h in range(n_heads)], axis=-1)

        acc = lax.fori_loop(0, TT, inner, jnp.zeros((SEQS, width), F32))
        oi_scr[pl.ds(pl.multiple_of(t * SEQS, SEQS), SEQS), :] = acc
        return carry

    lax.fori_loop(0, TT, outer, 0)


def _mixer_kernel(x_ref, gmix_ref, win_ref, lam2_re_ref, lam2_im_ref, bk_ref, ck_ref, gk_ref, d_ref,
                  wglu_ref, bglu_ref, lb_ref, gain_ref, wpa_ref, wpb_ref, wout_ref, wup_ref, wdown_ref,
                  out_ref, wup_bf_ref, wdown_bf_ref,
                  slab_scr, bu_scr, sst_scr, ycar_scr, hst_scr, oi_scr, fb_scr, *, widths):
    s5_w, hg_w, d_model = widths
    wup_bf_ref[...] = wup_ref[...].astype(BF16)
    wdown_bf_ref[...] = wdown_ref[...].astype(BF16)

    @pl.when(pl.program_id(1) == 0)
    def _():
        sst_scr[...] = jnp.zeros_like(sst_scr)
        ycar_scr[...] = jnp.zeros_like(ycar_scr)
        hst_scr[...] = jnp.zeros_like(hst_scr)

    nt = TT // MIX_HEAD_CHUNKS
    us = [_rmsnorm(_interleave(x_ref, slab_scr, c * nt, nt), gmix_ref[...]).astype(BF16)
          for c in range(MIX_HEAD_CHUNKS)]
    za = jnp.concatenate([_dot(uc, win_ref[:, :s5_w]) for uc in us], axis=0)
    u = jnp.concatenate(us, axis=0)

    def proj(lo, width):
        return _dot(u, win_ref[:, lo:lo + width])

    y_s5 = _s5_branch(za, lam2_re_ref, lam2_im_ref, bk_ref, ck_ref, gk_ref, d_ref, bu_scr, sst_scr,
                      ycar_scr)
    o = s5_w
    yb, (qa, kk, zi, bcum, b_last, o_inter, out_gate) = _hgrn_branch(
        proj(o, hg_w), proj(o + hg_w, hg_w), proj(o + 2 * hg_w, hg_w), proj(o + 3 * hg_w, hg_w),
        lb_ref, gain_ref, hst_scr)
    o += 4 * hg_w
    gate_a = _sigmoid(proj(o, d_model))
    gate_b = _sigmoid(proj(o + d_model, d_model))

    ya = jax.nn.gelu(y_s5)
    ya = ya * _sigmoid(_dot(ya.astype(BF16), wglu_ref[...]) + bglu_ref[...])
    m_a = gate_a * _dot(ya.astype(BF16), wpa_ref[...])

    def finish(yb):
        m = m_a + gate_b * _dot(yb.astype(BF16), wpb_ref[...])
        x_res = jnp.concatenate([slab_scr[s] for s in range(slab_scr.shape[0])], axis=-1)
        out_ref[0] = x_res + _dot(m.astype(BF16), wout_ref[...])

    finish(yb)

    @pl.when(jnp.min(b_last) < -HG_FAST_DECAY_LIMIT)
    def _():
        _hgrn_intra_exact(qa, kk, zi, bcum, fb_scr, oi_scr)
        finish(_hgrn_head_norm(oi_scr[...] + o_inter, gain_ref) * out_gate)


def _conv_taps(h, prev, w, b):
    rows = h.shape[0]
    h1 = jnp.concatenate([prev[SEQS:], h[:rows - SEQS]], axis=0)
    h2 = jnp.concatenate([prev, h[:rows - 2 * SEQS]], axis=0)
    return h2 * w[0:1] + h1 * w[1:2] + h * w[2:3] + b


def _ffn_kernel(x_ref, gf_ref, wup_ref, wconv_ref, bconv_ref, wdown_ref, gfin_ref, out_ref,
                halo_scr, slab_scr, act_scr):
    d_ff = wdown_ref.shape[0]
    n_chunks = d_ff // FFN_CHUNK

    @pl.when(pl.program_id(1) == 0)
    def _():
        halo_scr[...] = jnp.zeros_like(halo_scr)

    x1 = x_ref[0]
    u = _rmsnorm(x1, gf_ref[...]).astype(BF16)
    for c in range(n_chunks):
        cols = []
        for part in range(2):
            j = part * n_chunks + c
            cs = slice(j * FFN_CHUNK, (j + 1) * FFN_CHUNK)
            h = _dot(u, wup_ref[:, cs])
            cols.append(_conv_taps(h, halo_scr[:, cs], wconv_ref[:, cs], bconv_ref[:, cs]))
            halo_scr[:, cs] = h[FFN_ROWS - 2 * SEQS:]
        gate, val = cols
        act_scr[:, c * FFN_CHUNK:(c + 1) * FFN_CHUNK] = (gate * _sigmoid(gate) * val).astype(BF16)
    blk = FFN_ROWS // FFN_OUT_SPLIT
    for i in range(FFN_OUT_SPLIT):
        rs = slice(i * blk, (i + 1) * blk)
        y = x1[rs] + _dot(act_scr[rs, :], wdown_ref[...])
        _deinterleave(_rmsnorm(y, gfin_ref[...]), slab_scr, out_ref, i * blk)


def _block_diag(vals):
    nt, g, a, b = vals.shape
    same = jnp.arange(g)[:, None, None, None] == jnp.arange(g)[None, None, :, None]
    return jnp.where(same, vals[:, :, :, None, :], 0).reshape(nt, g * a, g * b)


def kernel(x, g_mix, w_in, s5_a_re, s5_a_im, s5_log_dt, s5_b_re, s5_b_im, s5_c_re, s5_c_im, s5_d,
           w_glu, b_glu, hg_lb_logits, hg_norm_gain, w_pa, w_pb, w_out, g_ffn, w_up, w_conv,
           b_conv, w_down, g_final):
    depth = w_in.shape[0]
    assert depth == 1, "kernel is written for a single layer"
    bsz, seq, d_model = x.shape
    s5_w = s5_d.shape[-1]
    hg_w = hg_norm_gain.shape[-1]
    d_ff = w_down.shape[1]
    n_in = w_in.shape[-1]
    n_groups = s5_a_re.shape[1]
    n_tiles = s5_w // LANES
    n_heads = hg_w // HG_HEAD_DIM
    n_chunks = d_ff // FFN_CHUNK
    assert bsz % SEQS == 0 and seq % TT == 0 and seq % FFN_TT == 0 and s5_w % LANES == 0 and d_ff % FFN_CHUNK == 0
    assert d_model % LANES == 0 and n_in == s5_w + 4 * hg_w + 2 * d_model
    grid = (bsz // SEQS, seq // TT)
    cparams = pltpu.CompilerParams(vmem_limit_bytes=VMEM_LIMIT,
                                   dimension_semantics=("arbitrary", "arbitrary"))

    gps = jax.ShapeDtypeStruct((n_groups, S5_STATE), F32)
    ghps = jax.ShapeDtypeStruct((n_groups, S5_GROUP, S5_STATE), F32)
    lam2_re, lam2_im, bbt_re, bbt_im, lbt_re, lbt_im, cl_re, cl_im, cb, lb = pl.pallas_call(
        functools.partial(_prep_kernel, layer=0),
        out_shape=[gps] * 2 + [ghps] * 6
        + [jax.ShapeDtypeStruct((S5_GROUP, n_groups, S5_GROUP), F32),
           jax.ShapeDtypeStruct((1, hg_w), F32)],
        name="prep",
    )(s5_a_re[0], s5_a_im[0], s5_log_dt[0][:, None],
      jnp.swapaxes(s5_b_re[0], 1, 2), jnp.swapaxes(s5_b_im[0], 1, 2), s5_c_re[0], s5_c_im[0],
      hg_lb_logits)

    def tiles(v):
        return v.reshape((n_tiles, GROUPS_PER_TILE) + v.shape[1:])

    def bd(v):
        return _block_diag(tiles(v))

    def bd_t(v):
        return _block_diag(tiles(jnp.swapaxes(v, 1, 2)))
    lam2_re_t = jnp.broadcast_to(lam2_re.reshape(n_tiles, 1, STATE_TILE), (n_tiles, SEQS, STATE_TILE))
    lam2_im_t = jnp.broadcast_to(lam2_im.reshape(n_tiles, 1, STATE_TILE), (n_tiles, SEQS, STATE_TILE))
    bk = jnp.concatenate([jnp.concatenate([bd(lbt_re), bd(lbt_im)], axis=-1),
                          jnp.concatenate([bd(bbt_re), bd(bbt_im)], axis=-1)], axis=1).astype(BF16)
    ck = jnp.concatenate([jnp.concatenate([bd_t(s5_c_re[0]), bd_t(-s5_c_im[0])], axis=1),
                          jnp.concatenate([bd_t(cl_re), bd_t(-cl_im)], axis=1)], axis=-1).astype(BF16)
    gk = bd(jnp.swapaxes(cb, 0, 1)).astype(BF16)

    n_steps = grid[0] * grid[1]

    def row_slices(n_rows, n_cols):
        rows = CAST_ROWS_MIN * -(-n_rows // (n_steps * CAST_ROWS_MIN))
        while n_rows % rows:
            rows += CAST_ROWS_MIN
        return pl.BlockSpec((rows, n_cols),
                            lambda g, t: (jnp.minimum(g * grid[1] + t, n_rows // rows - 1), 0))
    up_slices = row_slices(d_model, 2 * d_ff)
    down_slices = row_slices(d_ff, d_model)
    x1, w_up_b, w_down_b = pl.pallas_call(
        functools.partial(_mixer_kernel, widths=(s5_w, hg_w, d_model)),
        grid=grid,
        in_specs=[
            _seq_spec(d_model, TT),
            _const_spec((1, d_model)),
            _const_spec((d_model, n_in)),
            _const_spec((n_tiles, SEQS, STATE_TILE)),
            _const_spec((n_tiles, SEQS, STATE_TILE)),
            _const_spec((n_tiles, 2 * LANES, 2 * STATE_TILE)),
            _const_spec((n_tiles, 2 * STATE_TILE, 2 * LANES)),
            _const_spec((n_tiles, LANES, LANES)),
            _const_spec((1, s5_w)),
            _const_spec((s5_w, s5_w)),
            _const_spec((1, s5_w)),
            _const_spec((1, hg_w)),
            _const_spec((1, hg_w)),
            _const_spec((s5_w, d_model)),
            _const_spec((hg_w, d_model)),
            _const_spec((d_model, d_model)),
            up_slices, down_slices],
        out_specs=[pl.BlockSpec((1, ROWS, d_model), lambda g, t: (g, t, 0)), up_slices, down_slices],
        out_shape=[jax.ShapeDtypeStruct((bsz // SEQS, seq * SEQS, d_model), F32),
                   jax.ShapeDtypeStruct((d_model, 2 * d_ff), BF16),
                   jax.ShapeDtypeStruct((d_ff, d_model), BF16)],
        scratch_shapes=[pltpu.VMEM((d_model // LANES, ROWS, LANES), F32),
                        pltpu.VMEM((ROWS // 2, n_tiles * 2 * STATE_TILE), F32),
                        pltpu.VMEM((n_tiles, 2, SEQS, STATE_TILE), F32),
                        pltpu.VMEM((n_tiles, SEQS, LANES), F32),
                        pltpu.VMEM((n_heads, HG_HEAD_DIM, SEQS * HG_HEAD_DIM), F32),
                        pltpu.VMEM((ROWS, hg_w), F32),
                        pltpu.VMEM((4, ROWS, hg_w), F32)],
        compiler_params=cparams,
        name="mixer",
    )(x, g_mix[0][None], w_in[0].astype(BF16), lam2_re_t, lam2_im_t, bk, ck, gk,
      s5_d[0][None], w_glu[0].astype(BF16), b_glu[0][None], lb, hg_norm_gain[0][None],
      w_pa[0].astype(BF16), w_pb[0].astype(BF16), w_out[0].astype(BF16), w_up[0], w_down[0])

    return pl.pallas_call(
        _ffn_kernel,
        grid=(bsz // SEQS, seq // FFN_TT),
        in_specs=[pl.BlockSpec((1, FFN_ROWS, d_model), lambda g, t: (g, t, 0)),
                  _const_spec((1, d_model)),
                  _const_spec((d_model, 2 * d_ff)),
                  _const_spec((CONV_W, 2 * d_ff)),
                  _const_spec((1, 2 * d_ff)),
                  _const_spec((d_ff, d_model)),
                  _const_spec((1, d_model))],
        out_specs=_seq_spec(d_model, FFN_TT),
        out_shape=jax.ShapeDtypeStruct((bsz, seq, d_model), F32),
        scratch_shapes=[pltpu.VMEM((2 * SEQS, 2 * d_ff), F32),
                        pltpu.VMEM((d_model // LANES, FFN_ROWS, LANES), F32),
                        pltpu.VMEM((FFN_ROWS, d_ff), BF16)],
        compiler_params=cparams,
        name="ffn",
    )(x1, g_ffn[0][None], w_up_b, w_conv[0], b_conv[0][None], w_down_b, g_final[None])
```

```python
import functools

import jax
import jax.numpy as jnp
from jax import lax
from jax.experimental import pallas as pl
from jax.experimental.pallas import tpu as pltpu

EPS = 1e-6
S5_GROUP = 16
S5_STATE = 64
HG_HEAD_DIM = 128
CONV_W = 3

LANES = 128
SEQS = 8
TT = 64
ROWS = SEQS * TT
GROUPS_PER_TILE = LANES // S5_GROUP
STATE_TILE = GROUPS_PER_TILE * S5_STATE
HG_FAST_DECAY_LIMIT = 60.0
CAST_ROWS_MIN = 16
FFN_TT = 128
FFN_ROWS = SEQS * FFN_TT
FFN_CHUNK = 256
FFN_OUT_SPLIT = 2
MIX_HEAD_CHUNKS = 4
VMEM_LIMIT = 56 * 1024 * 1024

F32 = jnp.float32
BF16 = jnp.bfloat16


def _dot(a, b):
    return jnp.dot(a, b, preferred_element_type=F32)


def _dot_t0(a, b):
    return lax.dot_general(a, b, (((0,), (0,)), ((), ())), preferred_element_type=F32)


def _dot_t1(a, b):
    return lax.dot_general(a, b, (((1,), (1,)), ((), ())), preferred_element_type=F32)


def _rmsnorm(x, gain):
    return x * lax.rsqrt(jnp.mean(x * x, axis=-1, keepdims=True) + EPS) * gain


def _sigmoid(x):
    return jax.nn.sigmoid(x)


def _const_spec(shape):
    nd = len(shape)
    return pl.BlockSpec(shape, lambda *_: (0,) * nd, pipeline_mode=pl.Buffered(1))


def _seq_spec(width, tokens):
    return pl.BlockSpec((SEQS, tokens, width), lambda g, t: (g, t, 0))


def _interleave(seq_ref, slab_scr, t0, nt):
    n_slabs = slab_scr.shape[0]
    for b in range(SEQS):
        for s in range(n_slabs):
            slab_scr[s, pl.ds(t0 * SEQS + b, nt, stride=SEQS), :] = (
                seq_ref[b, t0:t0 + nt, s * LANES:(s + 1) * LANES])
    return jnp.concatenate([slab_scr[s, t0 * SEQS:(t0 + nt) * SEQS, :] for s in range(n_slabs)], axis=-1)


def _deinterleave(val, slab_scr, seq_ref, row0=0):
    n_slabs = slab_scr.shape[0]
    rows = val.shape[0]
    for s in range(n_slabs):
        slab_scr[s, row0:row0 + rows, :] = val[:, s * LANES:(s + 1) * LANES]
    for b in range(SEQS):
        seq_ref[b, row0 // SEQS:(row0 + rows) // SEQS, :] = jnp.concatenate(
            [slab_scr[s, pl.ds(row0 + b, rows // SEQS, stride=SEQS), :] for s in range(n_slabs)],
            axis=-1)


def _prep_kernel(a_re_ref, a_im_ref, log_dt_ref, bt_re_ref, bt_im_ref, c_re_ref, c_im_ref, lbl_ref,
                 lam2_ref, bmat_ref, cmat_ref, cb_ref, lb_ref, *, layer):
    a_re = a_re_ref[...]
    a_im = a_im_ref[...]
    dt = jnp.exp(log_dt_ref[...])
    mag = jnp.exp(a_re * dt)
    ang = a_im * dt
    lb_re = mag * jnp.cos(ang)
    lb_im = mag * jnp.sin(ang)
    den = a_re * a_re + a_im * a_im
    n_re = lb_re - 1.0
    n_im = lb_im
    co_re = (n_re * a_re + n_im * a_im) / den
    co_im = (n_im * a_re - n_re * a_im) / den
    lam2_ref[0] = lb_re * lb_re - lb_im * lb_im
    lam2_ref[1] = 2.0 * lb_re * lb_im
    bt_re = bt_re_ref[...]
    bt_im = bt_im_ref[...]
    bbt_re = co_re[:, None, :] * bt_re - co_im[:, None, :] * bt_im
    bbt_im = co_re[:, None, :] * bt_im + co_im[:, None, :] * bt_re
    l_re = lb_re[:, None, :]
    l_im = lb_im[:, None, :]
    bmat_ref[0] = l_re * bbt_re - l_im * bbt_im
    bmat_ref[1] = l_re * bbt_im + l_im * bbt_re
    bmat_ref[2] = bbt_re
    bmat_ref[3] = bbt_im
    c_re = c_re_ref[...]
    c_im = c_im_ref[...]
    cmat_ref[0] = c_re
    cmat_ref[1] = -c_im
    cmat_ref[2] = c_re * l_re - c_im * l_im
    cmat_ref[3] = -(c_re * l_im + c_im * l_re)
    for k in range(bt_re.shape[1]):
        cb_ref[k] = jnp.sum(c_re * bbt_re[:, k:k + 1, :] - c_im * bbt_im[:, k:k + 1, :], axis=-1)
    logits = lbl_ref[...]
    m = jnp.max(logits, axis=0, keepdims=True)
    e = jnp.exp(logits - m)
    tot = jnp.sum(e, axis=0, keepdims=True)
    lb_ref[...] = jnp.sum(e[: layer + 1], axis=0, keepdims=True) / tot


def _s5_branch(za, lam2_ref, bk_ref, ck_ref, gk_ref, d_ref, bu_scr, st_scr, ycar_scr):
    n_tiles = bk_ref.shape[0]
    width = za.shape[-1]
    pairs = TT // 2
    prows = pairs * SEQS
    za3 = za.reshape(pairs, 2 * SEQS, width)
    u_even = za3[:, :SEQS, :].reshape(prows, width)
    u_odd = za3[:, SEQS:, :].reshape(prows, width)
    ue_bf = u_even.astype(BF16)
    uo_bf = u_odd.astype(BF16)
    for k in range(n_tiles):
        lt = slice(k * LANES, (k + 1) * LANES)
        bu_scr[:, k * 2 * STATE_TILE:(k + 1) * 2 * STATE_TILE] = _dot(
            jnp.concatenate([ue_bf[:, lt], uo_bf[:, lt]], axis=-1), bk_ref[k])
    for k in range(n_tiles):
        lr = lam2_ref[0, k]
        li = lam2_ref[1, k]
        c_re = k * 2 * STATE_TILE
        c_im = c_re + STATE_TILE
        xr = st_scr[k, 0]
        xi = st_scr[k, 1]
        for p in range(pairs):
            r = slice(p * SEQS, (p + 1) * SEQS)
            nr = lr * xr - li * xi + bu_scr[r, c_re:c_re + STATE_TILE]
            ni = lr * xi + li * xr + bu_scr[r, c_im:c_im + STATE_TILE]
            bu_scr[r, c_re:c_re + STATE_TILE] = nr
            bu_scr[r, c_im:c_im + STATE_TILE] = ni
            xr, xi = nr, ni
        st_scr[k, 0] = xr
        st_scr[k, 1] = xi
    y_even, y_odd = [], []
    for k in range(n_tiles):
        lt = slice(k * LANES, (k + 1) * LANES)
        xk = bu_scr[:, k * 2 * STATE_TILE:(k + 1) * 2 * STATE_TILE].astype(BF16)
        yy = _dot(xk, ck_ref[k])
        y_odd.append(yy[:, :LANES])
        nxt = yy[:, LANES:]
        y_even.append(jnp.concatenate([ycar_scr[k], nxt[:prows - SEQS]], axis=0)
                      + _dot(ue_bf[:, lt], gk_ref[k]))
        ycar_scr[k] = nxt[prows - SEQS:]
    d = d_ref[...]
    y_even = jnp.concatenate(y_even, axis=-1) + d * u_even
    y_odd = jnp.concatenate(y_odd, axis=-1) + d * u_odd
    return jnp.concatenate([y_even.reshape(pairs, SEQS, width), y_odd.reshape(pairs, SEQS, width)],
                           axis=1).reshape(ROWS, width)


def _hgrn_branch(zq, zf, zi, zg, lb_ref, gain_ref, hst_scr):
    n_heads = hst_scr.shape[0]
    width = zq.shape[-1]
    lb = lb_ref[...]
    f = lb + (1.0 - lb) * _sigmoid(zf)
    lf = jnp.log(f)
    kk = 1.0 - f
    acc = lf[0:SEQS]
    parts = [acc]
    for t in range(1, TT):
        acc = acc + lf[t * SEQS:(t + 1) * SEQS]
        parts.append(acc)
    bcum = jnp.concatenate(parts, axis=0)
    b_last = parts[-1]
    qa = zq * _sigmoid(zq) * (HG_HEAD_DIM ** -0.5)
    qh = qa * jnp.exp(bcum)
    kh = (kk * jnp.exp(-bcum)).astype(BF16)
    ke = (kk.reshape(TT, SEQS, width) * jnp.exp(b_last[None] - bcum.reshape(TT, SEQS, width))
          ).reshape(ROWS, width)
    dec = jnp.exp(b_last)
    half = ROWS // 2
    dlt = (lax.broadcasted_iota(jnp.int32, (half, half), 1)
           - lax.broadcasted_iota(jnp.int32, (half, half), 0))
    same_seq = (dlt & (SEQS - 1)) == 0
    causal = same_seq & (dlt >= 0)
    seq_of_row = lax.broadcasted_iota(jnp.int32, (ROWS, HG_HEAD_DIM), 0) & (SEQS - 1)
    seq_of_col = lax.broadcasted_iota(jnp.int32, (HG_HEAD_DIM, ROWS), 1) & (SEQS - 1)
    intra, inter = [], []
    for h in range(n_heads):
        sl = slice(h * HG_HEAD_DIM, (h + 1) * HG_HEAD_DIM)
        q_t = qh[:, sl].T
        i_t = zi[:, sl].T.astype(BF16)
        q_tb = q_t.astype(BF16)
        k_e, k_l = kh[:half, sl], kh[half:, sl]
        sc_ee = jnp.where(causal, _dot(k_e, q_tb[:, :half]), 0.0).astype(BF16)
        sc_el = jnp.where(same_seq, _dot(k_e, q_tb[:, half:]), 0.0).astype(BF16)
        sc_ll = jnp.where(causal, _dot(k_l, q_tb[:, half:]), 0.0).astype(BF16)
        intra_t = jnp.concatenate(
            [_dot(i_t[:, :half], sc_ee),
             _dot(i_t, jnp.concatenate([sc_el, sc_ll], axis=0))], axis=1)
        q_aug_t = jnp.concatenate([jnp.where(seq_of_col == b, q_t, 0.0) for b in range(SEQS)],
                                  axis=0).astype(BF16)
        k_aug = jnp.concatenate([jnp.where(seq_of_row == b, ke[:, sl], 0.0) for b in range(SEQS)],
                                axis=-1).astype(BF16)
        st = hst_scr[h]
        inter_t = _dot(st.astype(BF16), q_aug_t)
        dec_row = jnp.concatenate([dec[b:b + 1, sl] for b in range(SEQS)], axis=-1)
        hst_scr[h] = st * dec_row + _dot(i_t, k_aug)
        intra.append(intra_t.T)
        inter.append(inter_t.T)
    o_inter = jnp.concatenate(inter, axis=-1)
    out_gate = zg * _sigmoid(zg)
    yb = _hgrn_head_norm(jnp.concatenate(intra, axis=-1) + o_inter, gain_ref) * out_gate
    return yb, (qa, kk, zi, bcum, b_last, o_inter, out_gate)


def _hgrn_head_norm(o, gain_ref):
    gain = gain_ref[...]
    return jnp.concatenate(
        [_rmsnorm(o[:, h * HG_HEAD_DIM:(h + 1) * HG_HEAD_DIM], gain[:, h * HG_HEAD_DIM:(h + 1) * HG_HEAD_DIM])
         for h in range(o.shape[-1] // HG_HEAD_DIM)], axis=-1)


def _hgrn_intra_exact(qa, kk, zi, bcum, fb_scr, oi_scr):
    width = qa.shape[-1]
    n_heads = width // HG_HEAD_DIM
    fb_scr[0] = qa
    fb_scr[1] = kk
    fb_scr[2] = zi
    fb_scr[3] = bcum

    def rows(j, t):
        return fb_scr[j, pl.ds(pl.multiple_of(t * SEQS, SEQS), SEQS), :]

    def outer(t, carry):
        q_t = rows(0, t)
        b_t = rows(3, t)

        def inner(s, acc):
            decay = jnp.where(s <= t, jnp.exp(jnp.minimum(b_t - rows(3, s), 0.0)), 0.0)
            p = q_t * rows(1, s) * decay
            i_s = rows(2, s)
            return acc + jnp.concatenate(
                [jnp.sum(p[:, h * HG_HEAD_DIM:(h + 1) * HG_HEAD_DIM], axis=-1, keepdims=True)
                 * i_s[:, h * HG_HEAD_DIM:(h + 1) * HG_HEAD_DIM] for h in range(n_heads)], axis=-1)

        acc = lax.fori_loop(0, TT, inner, jnp.zeros((SEQS, width), F32))
        oi_scr[pl.ds(pl.multiple_of(t * SEQS, SEQS), SEQS), :] = acc
        return carry

    lax.fori_loop(0, TT, outer, 0)


def _mixer_kernel(x_ref, gmix_ref, win_ref, lam2_ref, bk_ref, ck_ref, gk_ref, d_ref,
                  wglu_ref, bglu_ref, lb_ref, gain_ref, wpa_ref, wpb_ref, wout_ref, wup_ref, wdown_ref,
                  out_ref, wup_bf_ref, wdown_bf_ref,
                  slab_scr, bu_scr, sst_scr, ycar_scr, hst_scr, oi_scr, fb_scr, *, widths):
    s5_w, hg_w, d_model = widths
    wup_bf_ref[...] = wup_ref[...].astype(BF16)
    wdown_bf_ref[...] = wdown_ref[...].astype(BF16)

    @pl.when(pl.program_id(1) == 0)
    def _():
        sst_scr[...] = jnp.zeros_like(sst_scr)
        ycar_scr[...] = jnp.zeros_like(ycar_scr)
        hst_scr[...] = jnp.zeros_like(hst_scr)

    nt = TT // MIX_HEAD_CHUNKS
    us = [_rmsnorm(_interleave(x_ref, slab_scr, c * nt, nt), gmix_ref[...]).astype(BF16)
          for c in range(MIX_HEAD_CHUNKS)]
    za = jnp.concatenate([_dot(uc, win_ref[:, :s5_w]) for uc in us], axis=0)
    u = jnp.concatenate(us, axis=0)

    def proj(lo, width):
        return _dot(u, win_ref[:, lo:lo + width])

    y_s5 = _s5_branch(za, lam2_ref, bk_ref, ck_ref, gk_ref, d_ref, bu_scr, sst_scr,
                      ycar_scr)
    o = s5_w
    yb, (qa, kk, zi, bcum, b_last, o_inter, out_gate) = _hgrn_branch(
        proj(o, hg_w), proj(o + hg_w, hg_w), proj(o + 2 * hg_w, hg_w), proj(o + 3 * hg_w, hg_w),
        lb_ref, gain_ref, hst_scr)
    o += 4 * hg_w
    gate_a = _sigmoid(proj(o, d_model))
    gate_b = _sigmoid(proj(o + d_model, d_model))

    ya = jax.nn.gelu(y_s5)
    ya = ya * _sigmoid(_dot(ya.astype(BF16), wglu_ref[...]) + bglu_ref[...])
    m_a = gate_a * _dot(ya.astype(BF16), wpa_ref[...])

    def finish(yb):
        m = m_a + gate_b * _dot(yb.astype(BF16), wpb_ref[...])
        x_res = jnp.concatenate([slab_scr[s] for s in range(slab_scr.shape[0])], axis=-1)
        out_ref[0] = x_res + _dot(m.astype(BF16), wout_ref[...])

    finish(yb)

    @pl.when(jnp.min(b_last) < -HG_FAST_DECAY_LIMIT)
    def _():
        _hgrn_intra_exact(qa, kk, zi, bcum, fb_scr, oi_scr)
        finish(_hgrn_head_norm(oi_scr[...] + o_inter, gain_ref) * out_gate)


def _conv_taps(h, prev, w, b):
    rows = h.shape[0]
    h1 = jnp.concatenate([prev[SEQS:], h[:rows - SEQS]], axis=0)
    h2 = jnp.concatenate([prev, h[:rows - 2 * SEQS]], axis=0)
    return h2 * w[0:1] + h1 * w[1:2] + h * w[2:3] + b


def _ffn_kernel(x_ref, gf_ref, wup_ref, wconv_ref, bconv_ref, wdown_ref, gfin_ref, out_ref,
                halo_scr, slab_scr, act_scr):
    d_ff = wdown_ref.shape[0]
    n_chunks = d_ff // FFN_CHUNK

    @pl.when(pl.program_id(1) == 0)
    def _():
        halo_scr[...] = jnp.zeros_like(halo_scr)

    x1 = x_ref[0]
    u = _rmsnorm(x1, gf_ref[...]).astype(BF16)
    for c in range(n_chunks):
        cols = []
        for part in range(2):
            j = part * n_chunks + c
            cs = slice(j * FFN_CHUNK, (j + 1) * FFN_CHUNK)
            h = _dot(u, wup_ref[:, cs])
            cols.append(_conv_taps(h, halo_scr[:, cs], wconv_ref[:, cs], bconv_ref[:, cs]))
            halo_scr[:, cs] = h[FFN_ROWS - 2 * SEQS:]
        gate, val = cols
        act_scr[:, c * FFN_CHUNK:(c + 1) * FFN_CHUNK] = (gate * _sigmoid(gate) * val).astype(BF16)
    blk = FFN_ROWS // FFN_OUT_SPLIT
    for i in range(FFN_OUT_SPLIT):
        rs = slice(i * blk, (i + 1) * blk)
        y = x1[rs] + _dot(act_scr[rs, :], wdown_ref[...])
        _deinterleave(_rmsnorm(y, gfin_ref[...]), slab_scr, out_ref, i * blk)


def _block_diag(vals):
    nt, g, a, b = vals.shape
    same = jnp.arange(g)[:, None, None, None] == jnp.arange(g)[None, None, :, None]
    return jnp.where(same, vals[:, :, :, None, :], 0).reshape(nt, g * a, g * b)


def kernel(x, g_mix, w_in, s5_a_re, s5_a_im, s5_log_dt, s5_b_re, s5_b_im, s5_c_re, s5_c_im, s5_d,
           w_glu, b_glu, hg_lb_logits, hg_norm_gain, w_pa, w_pb, w_out, g_ffn, w_up, w_conv,
           b_conv, w_down, g_final):
    depth = w_in.shape[0]
    assert depth == 1, "kernel is written for a single layer"
    bsz, seq, d_model = x.shape
    s5_w = s5_d.shape[-1]
    hg_w = hg_norm_gain.shape[-1]
    d_ff = w_down.shape[1]
    n_in = w_in.shape[-1]
    n_groups = s5_a_re.shape[1]
    n_tiles = s5_w // LANES
    n_heads = hg_w // HG_HEAD_DIM
    n_chunks = d_ff // FFN_CHUNK
    assert bsz % SEQS == 0 and seq % TT == 0 and seq % FFN_TT == 0 and s5_w % LANES == 0 and d_ff % FFN_CHUNK == 0
    assert d_model % LANES == 0 and n_in == s5_w + 4 * hg_w + 2 * d_model
    grid = (bsz // SEQS, seq // TT)
    cparams = pltpu.CompilerParams(vmem_limit_bytes=VMEM_LIMIT,
                                   dimension_semantics=("arbitrary", "arbitrary"))

    ghp = (n_groups, S5_GROUP, S5_STATE)
    lam2, bmat, cmat, cb, lb = pl.pallas_call(
        functools.partial(_prep_kernel, layer=0),
        out_shape=[jax.ShapeDtypeStruct((2, n_groups, S5_STATE), F32),
                   jax.ShapeDtypeStruct((4,) + ghp, F32),
                   jax.ShapeDtypeStruct((4,) + ghp, F32),
                   jax.ShapeDtypeStruct((S5_GROUP, n_groups, S5_GROUP), F32),
                   jax.ShapeDtypeStruct((1, hg_w), F32)],
        name="prep",
    )(s5_a_re[0], s5_a_im[0], s5_log_dt[0][:, None],
      jnp.swapaxes(s5_b_re[0], 1, 2), jnp.swapaxes(s5_b_im[0], 1, 2), s5_c_re[0], s5_c_im[0],
      hg_lb_logits)

    def bd(v):
        lead = v.shape[:-3]
        v = v.reshape((-1, GROUPS_PER_TILE) + v.shape[-2:])
        return _block_diag(v).reshape(lead + (n_tiles, GROUPS_PER_TILE * v.shape[-2],
                                              GROUPS_PER_TILE * v.shape[-1]))
    lam2_t = jnp.broadcast_to(lam2.reshape(2, n_tiles, 1, STATE_TILE), (2, n_tiles, SEQS, STATE_TILE))
    bk = (bd(bmat).reshape(2, 2, n_tiles, LANES, STATE_TILE).transpose(2, 0, 3, 1, 4)
          .reshape(n_tiles, 2 * LANES, 2 * STATE_TILE).astype(BF16))
    ck = (bd(jnp.swapaxes(cmat, 2, 3)).reshape(2, 2, n_tiles, STATE_TILE, LANES).transpose(2, 1, 3, 0, 4)
          .reshape(n_tiles, 2 * STATE_TILE, 2 * LANES).astype(BF16))
    gk = bd(jnp.swapaxes(cb, 0, 1)).astype(BF16)

    n_steps = grid[0] * grid[1]

    def row_slices(n_rows, n_cols):
        rows = CAST_ROWS_MIN * -(-n_rows // (n_steps * CAST_ROWS_MIN))
        while n_rows % rows:
            rows += CAST_ROWS_MIN
        return pl.BlockSpec((rows, n_cols),
                            lambda g, t: (jnp.minimum(g * grid[1] + t, n_rows // rows - 1), 0))
    up_slices = row_slices(d_model, 2 * d_ff)
    down_slices = row_slices(d_ff, d_model)
    x1, w_up_b, w_down_b = pl.pallas_call(
        functools.partial(_mixer_kernel, widths=(s5_w, hg_w, d_model)),
        grid=grid,
        in_specs=[
            _seq_spec(d_model, TT),
            _const_spec((1, d_model)),
            _const_spec((d_model, n_in)),
            _const_spec((2, n_tiles, SEQS, STATE_TILE)),
            _const_spec((n_tiles, 2 * LANES, 2 * STATE_TILE)),
            _const_spec((n_tiles, 2 * STATE_TILE, 2 * LANES)),
            _const_spec((n_tiles, LANES, LANES)),
            _const_spec((1, s5_w)),
            _const_spec((s5_w, s5_w)),
            _const_spec((1, s5_w)),
            _const_spec((1, hg_w)),
            _const_spec((1, hg_w)),
            _const_spec((s5_w, d_model)),
            _const_spec((hg_w, d_model)),
            _const_spec((d_model, d_model)),
            up_slices, down_slices],
        out_specs=[pl.BlockSpec((1, ROWS, d_model), lambda g, t: (g, t, 0)), up_slices, down_slices],
        out_shape=[jax.ShapeDtypeStruct((bsz // SEQS, seq * SEQS, d_model), F32),
                   jax.ShapeDtypeStruct((d_model, 2 * d_ff), BF16),
                   jax.ShapeDtypeStruct((d_ff, d_model), BF16)],
        scratch_shapes=[pltpu.VMEM((d_model // LANES, ROWS, LANES), F32),
                        pltpu.VMEM((ROWS // 2, n_tiles * 2 * STATE_TILE), F32),
                        pltpu.VMEM((n_tiles, 2, SEQS, STATE_TILE), F32),
                        pltpu.VMEM((n_tiles, SEQS, LANES), F32),
                        pltpu.VMEM((n_heads, HG_HEAD_DIM, SEQS * HG_HEAD_DIM), F32),
                        pltpu.VMEM((ROWS, hg_w), F32),
                        pltpu.VMEM((4, ROWS, hg_w), F32)],
        compiler_params=cparams,
        name="mixer",
    )(x, g_mix[0][None], w_in[0].astype(BF16), lam2_t, bk, ck, gk,
      s5_d[0][None], w_glu[0].astype(BF16), b_glu[0][None], lb, hg_norm_gain[0][None],
      w_pa[0].astype(BF16), w_pb[0].astype(BF16), w_out[0].astype(BF16), w_up[0], w_down[0])

    return pl.pallas_call(
        _ffn_kernel,
        grid=(bsz // SEQS, seq // FFN_TT),
        in_specs=[pl.BlockSpec((1, FFN_ROWS, d_model), lambda g, t: (g, t, 0)),
                  _const_spec((1, d_model)),
                  _const_spec((d_model, 2 * d_ff)),
                  _const_spec((CONV_W, 2 * d_ff)),
                  _const_spec((1, 2 * d_ff)),
                  _const_spec((d_ff, d_model)),
                  _const_spec((1, d_model))],
        out_specs=_seq_spec(d_model, FFN_TT),
        out_shape=jax.ShapeDtypeStruct((bsz, seq, d_model), F32),
        scratch_shapes=[pltpu.VMEM((2 * SEQS, 2 * d_ff), F32),
                        pltpu.VMEM((d_model // LANES, FFN_ROWS, LANES), F32),
                        pltpu.VMEM((FFN_ROWS, d_ff), BF16)],
        compiler_params=cparams,
        name="ffn",
    )(x1, g_ffn[0][None], w_up_b, w_conv[0], b_conv[0][None], w_down_b, g_final[None])
```

```python
import functools

import jax
import jax.numpy as jnp
from jax import lax
from jax.experimental import pallas as pl
from jax.experimental.pallas import tpu as pltpu

EPS = 1e-6
S5_GROUP = 16
S5_STATE = 64
HG_HEAD_DIM = 128
CONV_W = 3

LANES = 128
SEQS = 8
TT = 64
ROWS = SEQS * TT
GROUPS_PER_TILE = LANES // S5_GROUP
STATE_TILE = GROUPS_PER_TILE * S5_STATE
HG_FAST_DECAY_LIMIT = 60.0
CAST_ROWS_MIN = 16
FFN_TT = 128
FFN_ROWS = SEQS * FFN_TT
FFN_CHUNK = 256
FFN_OUT_SPLIT = 2
MIX_HEAD_CHUNKS = 4
V7X_VMEM_BYTES = 64 * 1024 * 1024
VMEM_LIMIT = V7X_VMEM_BYTES * 7 // 8

F32 = jnp.float32
BF16 = jnp.bfloat16


def _dot(a, b):
    return jnp.dot(a, b, preferred_element_type=F32)


def _dot_t0(a, b):
    return lax.dot_general(a, b, (((0,), (0,)), ((), ())), preferred_element_type=F32)


def _dot_t1(a, b):
    return lax.dot_general(a, b, (((1,), (1,)), ((), ())), preferred_element_type=F32)


def _rmsnorm(x, gain):
    return x * lax.rsqrt(jnp.mean(x * x, axis=-1, keepdims=True) + EPS) * gain


def _sigmoid(x):
    return jax.nn.sigmoid(x)


def _const_spec(shape):
    nd = len(shape)
    return pl.BlockSpec(shape, lambda *_: (0,) * nd, pipeline_mode=pl.Buffered(1))


def _seq_spec(width, tokens):
    return pl.BlockSpec((SEQS, tokens, width), lambda g, t: (g, t, 0))


def _interleave(seq_ref, slab_scr, t0, nt):
    n_slabs = slab_scr.shape[0]
    for b in range(SEQS):
        for s in range(n_slabs):
            slab_scr[s, pl.ds(t0 * SEQS + b, nt, stride=SEQS), :] = (
                seq_ref[b, t0:t0 + nt, s * LANES:(s + 1) * LANES])
    return jnp.concatenate([slab_scr[s, t0 * SEQS:(t0 + nt) * SEQS, :] for s in range(n_slabs)], axis=-1)


def _deinterleave(val, slab_scr, seq_ref, row0=0):
    n_slabs = slab_scr.shape[0]
    rows = val.shape[0]
    for s in range(n_slabs):
        slab_scr[s, row0:row0 + rows, :] = val[:, s * LANES:(s + 1) * LANES]
    for b in range(SEQS):
        seq_ref[b, row0 // SEQS:(row0 + rows) // SEQS, :] = jnp.concatenate(
            [slab_scr[s, pl.ds(row0 + b, rows // SEQS, stride=SEQS), :] for s in range(n_slabs)],
            axis=-1)


def _prep_kernel(a_re_ref, a_im_ref, log_dt_ref, bt_re_ref, bt_im_ref, c_re_ref, c_im_ref, lbl_ref,
                 lam2_re_ref, lam2_im_ref, bbt_re_ref, bbt_im_ref, lbt_re_ref, lbt_im_ref,
                 cl_re_ref, cl_im_ref, cb_ref, lb_ref, *, layer):
    a_re = a_re_ref[...]
    a_im = a_im_ref[...]
    dt = jnp.exp(log_dt_ref[...])
    mag = jnp.exp(a_re * dt)
    ang = a_im * dt
    lb_re = mag * jnp.cos(ang)
    lb_im = mag * jnp.sin(ang)
    den = a_re * a_re + a_im * a_im
    n_re = lb_re - 1.0
    n_im = lb_im
    co_re = (n_re * a_re + n_im * a_im) / den
    co_im = (n_im * a_re - n_re * a_im) / den
    lam2_re_ref[...] = lb_re * lb_re - lb_im * lb_im
    lam2_im_ref[...] = 2.0 * lb_re * lb_im
    bt_re = bt_re_ref[...]
    bt_im = bt_im_ref[...]
    bbt_re = co_re[:, None, :] * bt_re - co_im[:, None, :] * bt_im
    bbt_im = co_re[:, None, :] * bt_im + co_im[:, None, :] * bt_re
    bbt_re_ref[...] = bbt_re
    bbt_im_ref[...] = bbt_im
    l_re = lb_re[:, None, :]
    l_im = lb_im[:, None, :]
    lbt_re_ref[...] = l_re * bbt_re - l_im * bbt_im
    lbt_im_ref[...] = l_re * bbt_im + l_im * bbt_re
    c_re = c_re_ref[...]
    c_im = c_im_ref[...]
    cl_re_ref[...] = c_re * l_re - c_im * l_im
    cl_im_ref[...] = c_re * l_im + c_im * l_re
    for k in range(bt_re.shape[1]):
        cb_ref[k] = jnp.sum(c_re * bbt_re[:, k:k + 1, :] - c_im * bbt_im[:, k:k + 1, :], axis=-1)
    logits = lbl_ref[...]
    m = jnp.max(logits, axis=0, keepdims=True)
    e = jnp.exp(logits - m)
    tot = jnp.sum(e, axis=0, keepdims=True)
    lb_ref[...] = jnp.sum(e[: layer + 1], axis=0, keepdims=True) / tot


def _s5_branch(za, lam2_re_ref, lam2_im_ref, bk_ref, ck_ref, gk_ref, d_ref, bu_scr, st_scr, ycar_scr):
    n_tiles = bk_ref.shape[0]
    width = za.shape[-1]
    pairs = TT // 2
    prows = pairs * SEQS
    za3 = za.reshape(pairs, 2 * SEQS, width)
    u_even = za3[:, :SEQS, :].reshape(prows, width)
    u_odd = za3[:, SEQS:, :].reshape(prows, width)
    ue_bf = u_even.astype(BF16)
    uo_bf = u_odd.astype(BF16)
    for k in range(n_tiles):
        lt = slice(k * LANES, (k + 1) * LANES)
        bu_scr[:, k * 2 * STATE_TILE:(k + 1) * 2 * STATE_TILE] = _dot(
            jnp.concatenate([ue_bf[:, lt], uo_bf[:, lt]], axis=-1), bk_ref[k])
    for k in range(n_tiles):
        lr = lam2_re_ref[k]
        li = lam2_im_ref[k]
        c_re = k * 2 * STATE_TILE
        c_im = c_re + STATE_TILE
        xr = st_scr[k, 0]
        xi = st_scr[k, 1]
        for p in range(pairs):
            r = slice(p * SEQS, (p + 1) * SEQS)
            nr = lr * xr - li * xi + bu_scr[r, c_re:c_re + STATE_TILE]
            ni = lr * xi + li * xr + bu_scr[r, c_im:c_im + STATE_TILE]
            bu_scr[r, c_re:c_re + STATE_TILE] = nr
            bu_scr[r, c_im:c_im + STATE_TILE] = ni
            xr, xi = nr, ni
        st_scr[k, 0] = xr
        st_scr[k, 1] = xi
    y_even, y_odd = [], []
    for k in range(n_tiles):
        lt = slice(k * LANES, (k + 1) * LANES)
        xk = bu_scr[:, k * 2 * STATE_TILE:(k + 1) * 2 * STATE_TILE].astype(BF16)
        yy = _dot(xk, ck_ref[k])
        y_odd.append(yy[:, :LANES])
        nxt = yy[:, LANES:]
        y_even.append(jnp.concatenate([ycar_scr[k], nxt[:prows - SEQS]], axis=0)
                      + _dot(ue_bf[:, lt], gk_ref[k]))
        ycar_scr[k] = nxt[prows - SEQS:]
    d = d_ref[...]
    y_even = jnp.concatenate(y_even, axis=-1) + d * u_even
    y_odd = jnp.concatenate(y_odd, axis=-1) + d * u_odd
    return jnp.concatenate([y_even.reshape(pairs, SEQS, width), y_odd.reshape(pairs, SEQS, width)],
                           axis=1).reshape(ROWS, width)


def _hgrn_branch(zq, zf, zi, zg, lb_ref, gain_ref, hst_scr):
    n_heads = hst_scr.shape[0]
    width = zq.shape[-1]
    lb = lb_ref[...]
    f = lb + (1.0 - lb) * _sigmoid(zf)
    lf = jnp.log(f)
    kk = 1.0 - f
    acc = lf[0:SEQS]
    parts = [acc]
    for t in range(1, TT):
        acc = acc + lf[t * SEQS:(t + 1) * SEQS]
        parts.append(acc)
    bcum = jnp.concatenate(parts, axis=0)
    b_last = parts[-1]
    qa = zq * _sigmoid(zq) * (HG_HEAD_DIM ** -0.5)
    qh = qa * jnp.exp(bcum)
    kh = (kk * jnp.exp(-bcum)).astype(BF16)
    ke = (kk.reshape(TT, SEQS, width) * jnp.exp(b_last[None] - bcum.reshape(TT, SEQS, width))
          ).reshape(ROWS, width)
    dec = jnp.exp(b_last)
    half = ROWS // 2
    dlt = (lax.broadcasted_iota(jnp.int32, (half, half), 1)
           - lax.broadcasted_iota(jnp.int32, (half, half), 0))
    same_seq = (dlt & (SEQS - 1)) == 0
    causal = same_seq & (dlt >= 0)
    seq_of_row = lax.broadcasted_iota(jnp.int32, (ROWS, HG_HEAD_DIM), 0) & (SEQS - 1)
    seq_of_col = lax.broadcasted_iota(jnp.int32, (HG_HEAD_DIM, ROWS), 1) & (SEQS - 1)
    intra, inter = [], []
    for h in range(n_heads):
        sl = slice(h * HG_HEAD_DIM, (h + 1) * HG_HEAD_DIM)
        q_t = qh[:, sl].T
        i_t = zi[:, sl].T.astype(BF16)
        q_tb = q_t.astype(BF16)
        k_e, k_l = kh[:half, sl], kh[half:, sl]
        sc_ee = jnp.where(causal, _dot(k_e, q_tb[:, :half]), 0.0).astype(BF16)
        sc_el = jnp.where(same_seq, _dot(k_e, q_tb[:, half:]), 0.0).astype(BF16)
        sc_ll = jnp.where(causal, _dot(k_l, q_tb[:, half:]), 0.0).astype(BF16)
        intra_t = jnp.concatenate(
            [_dot(i_t[:, :half], sc_ee),
             _dot(i_t, jnp.concatenate([sc_el, sc_ll], axis=0))], axis=1)
        q_aug_t = jnp.concatenate([jnp.where(seq_of_col == b, q_t, 0.0) for b in range(SEQS)],
                                  axis=0).astype(BF16)
        k_aug = jnp.concatenate([jnp.where(seq_of_row == b, ke[:, sl], 0.0) for b in range(SEQS)],
                                axis=-1).astype(BF16)
        st = hst_scr[h]
        inter_t = _dot(st.astype(BF16), q_aug_t)
        dec_row = jnp.concatenate([dec[b:b + 1, sl] for b in range(SEQS)], axis=-1)
        hst_scr[h] = st * dec_row + _dot(i_t, k_aug)
        intra.append(intra_t.T)
        inter.append(inter_t.T)
    o_inter = jnp.concatenate(inter, axis=-1)
    out_gate = zg * _sigmoid(zg)
    yb = _hgrn_head_norm(jnp.concatenate(intra, axis=-1) + o_inter, gain_ref) * out_gate
    return yb, (qa, kk, zi, bcum, b_last, o_inter, out_gate)


def _hgrn_head_norm(o, gain_ref):
    gain = gain_ref[...]
    return jnp.concatenate(
        [_rmsnorm(o[:, h * HG_HEAD_DIM:(h + 1) * HG_HEAD_DIM], gain[:, h * HG_HEAD_DIM:(h + 1) * HG_HEAD_DIM])
         for h in range(o.shape[-1] // HG_HEAD_DIM)], axis=-1)


def _hgrn_intra_exact(qa, kk, zi, bcum, fb_scr, oi_scr):
    width = qa.shape[-1]
    n_heads = width // HG_HEAD_DIM
    fb_scr[0] = qa
    fb_scr[1] = kk
    fb_scr[2] = zi
    fb_scr[3] = bcum

    def rows(j, t):
        return fb_scr[j, pl.ds(pl.multiple_of(t * SEQS, SEQS), SEQS), :]

    def outer(t, carry):
        q_t = rows(0, t)
        b_t = rows(3, t)

        def inner(s, acc):
            decay = jnp.where(s <= t, jnp.exp(jnp.minimum(b_t - rows(3, s), 0.0)), 0.0)
            p = q_t * rows(1, s) * decay
            i_s = rows(2, s)
            return acc + jnp.concatenate(
                [jnp.sum(p[:, h * HG_HEAD_DIM:(h + 1) * HG_HEAD_DIM], axis=-1, keepdims=True)
                 * i_s[:, h * HG_HEAD_DIM:(h + 1) * HG_HEAD_DIM] for h in range(n_heads)], axis=-1)

        acc = lax.fori_loop(0, TT, inner, jnp.zeros((SEQS, width), F32))
        oi_scr[pl.ds(pl.multiple_of(t * SEQS, SEQS), SEQS), :] = acc
        return carry

    lax.fori_loop(0, TT, outer, 0)


def _mixer_kernel(x_ref, gmix_ref, win_ref, lam2_re_ref, lam2_im_ref, bk_ref, ck_ref, gk_ref, d_ref,
                  wglu_ref, bglu_ref, lb_ref, gain_ref, wpa_ref, wpb_ref, wout_ref, wup_ref, wdown_ref,
                  out_ref, wup_bf_ref, wdown_bf_ref,
                  slab_scr, bu_scr, sst_scr, ycar_scr, hst_scr, oi_scr, fb_scr, *, widths):
    s5_w, hg_w, d_model = widths
    wup_bf_ref[...] = wup_ref[...].astype(BF16)
    wdown_bf_ref[...] = wdown_ref[...].astype(BF16)

    @pl.when(pl.program_id(1) == 0)
    def _():
        sst_scr[...] = jnp.zeros_like(sst_scr)
        ycar_scr[...] = jnp.zeros_like(ycar_scr)
        hst_scr[...] = jnp.zeros_like(hst_scr)

    nt = TT // MIX_HEAD_CHUNKS
    us = [_rmsnorm(_interleave(x_ref, slab_scr, c * nt, nt), gmix_ref[...]).astype(BF16)
          for c in range(MIX_HEAD_CHUNKS)]
    za = jnp.concatenate([_dot(uc, win_ref[:, :s5_w]) for uc in us], axis=0)
    u = jnp.concatenate(us, axis=0)

    def proj(lo, width):
        return _dot(u, win_ref[:, lo:lo + width])

    y_s5 = _s5_branch(za, lam2_re_ref, lam2_im_ref, bk_ref, ck_ref, gk_ref, d_ref, bu_scr, sst_scr,
                      ycar_scr)
    o = s5_w
    yb, (qa, kk, zi, bcum, b_last, o_inter, out_gate) = _hgrn_branch(
        proj(o, hg_w), proj(o + hg_w, hg_w), proj(o + 2 * hg_w, hg_w), proj(o + 3 * hg_w, hg_w),
        lb_ref, gain_ref, hst_scr)
    o += 4 * hg_w
    gate_a = _sigmoid(proj(o, d_model))
    gate_b = _sigmoid(proj(o + d_model, d_model))

    ya = jax.nn.gelu(y_s5)
    ya = ya * _sigmoid(_dot(ya.astype(BF16), wglu_ref[...]) + bglu_ref[...])
    m_a = gate_a * _dot(ya.astype(BF16), wpa_ref[...])

    def finish(yb):
        m = m_a + gate_b * _dot(yb.astype(BF16), wpb_ref[...])
        x_res = jnp.concatenate([slab_scr[s] for s in range(slab_scr.shape[0])], axis=-1)
        out_ref[0] = x_res + _dot(m.astype(BF16), wout_ref[...])

    finish(yb)

    @pl.when(jnp.min(b_last) < -HG_FAST_DECAY_LIMIT)
    def _():
        _hgrn_intra_exact(qa, kk, zi, bcum, fb_scr, oi_scr)
        finish(_hgrn_head_norm(oi_scr[...] + o_inter, gain_ref) * out_gate)


def _conv_taps(h, prev, w, b):
    rows = h.shape[0]
    h1 = jnp.concatenate([prev[SEQS:], h[:rows - SEQS]], axis=0)
    h2 = jnp.concatenate([prev, h[:rows - 2 * SEQS]], axis=0)
    return h2 * w[0:1] + h1 * w[1:2] + h * w[2:3] + b


def _ffn_kernel(x_ref, gf_ref, wup_ref, wconv_ref, bconv_ref, wdown_ref, gfin_ref, out_ref,
                halo_scr, slab_scr, act_scr):
    d_ff = wdown_ref.shape[0]
    n_chunks = d_ff // FFN_CHUNK

    @pl.when(pl.program_id(1) == 0)
    def _():
        halo_scr[...] = jnp.zeros_like(halo_scr)

    x1 = x_ref[0]
    u = _rmsnorm(x1, gf_ref[...]).astype(BF16)
    for c in range(n_chunks):
        cols = []
        for part in range(2):
            j = part * n_chunks + c
            cs = slice(j * FFN_CHUNK, (j + 1) * FFN_CHUNK)
            h = _dot(u, wup_ref[:, cs])
            cols.append(_conv_taps(h, halo_scr[:, cs], wconv_ref[:, cs], bconv_ref[:, cs]))
            halo_scr[:, cs] = h[FFN_ROWS - 2 * SEQS:]
        gate, val = cols
        act_scr[:, c * FFN_CHUNK:(c + 1) * FFN_CHUNK] = (gate * _sigmoid(gate) * val).astype(BF16)
    blk = FFN_ROWS // FFN_OUT_SPLIT
    for i in range(FFN_OUT_SPLIT):
        rs = slice(i * blk, (i + 1) * blk)
        y = x1[rs] + _dot(act_scr[rs, :], wdown_ref[...])
        _deinterleave(_rmsnorm(y, gfin_ref[...]), slab_scr, out_ref, i * blk)


def _block_diag(vals):
    nt, g, a, b = vals.shape
    same = jnp.arange(g)[:, None, None, None] == jnp.arange(g)[None, None, :, None]
    return jnp.where(same, vals[:, :, :, None, :], 0).reshape(nt, g * a, g * b)


def kernel(x, g_mix, w_in, s5_a_re, s5_a_im, s5_log_dt, s5_b_re, s5_b_im, s5_c_re, s5_c_im, s5_d,
           w_glu, b_glu, hg_lb_logits, hg_norm_gain, w_pa, w_pb, w_out, g_ffn, w_up, w_conv,
           b_conv, w_down, g_final):
    depth = w_in.shape[0]
    assert depth == 1, "kernel is written for a single layer"
    bsz, seq, d_model = x.shape
    s5_w = s5_d.shape[-1]
    hg_w = hg_norm_gain.shape[-1]
    d_ff = w_down.shape[1]
    n_in = w_in.shape[-1]
    n_groups = s5_a_re.shape[1]
    n_tiles = s5_w // LANES
    n_heads = hg_w // HG_HEAD_DIM
    assert bsz % SEQS == 0 and seq % TT == 0 and seq % FFN_TT == 0
    assert s5_w % LANES == 0 and d_model % LANES == 0 and d_ff % FFN_CHUNK == 0
    assert hg_w % HG_HEAD_DIM == 0 and n_in == s5_w + 4 * hg_w + 2 * d_model
    grid = (bsz // SEQS, seq // TT)
    cparams = pltpu.CompilerParams(vmem_limit_bytes=VMEM_LIMIT,
                                   dimension_semantics=("arbitrary", "arbitrary"))

    gps = jax.ShapeDtypeStruct((n_groups, S5_STATE), F32)
    ghps = jax.ShapeDtypeStruct((n_groups, S5_GROUP, S5_STATE), F32)
    lam2_re, lam2_im, bbt_re, bbt_im, lbt_re, lbt_im, cl_re, cl_im, cb, lb = pl.pallas_call(
        functools.partial(_prep_kernel, layer=0),
        out_shape=[gps] * 2 + [ghps] * 6
        + [jax.ShapeDtypeStruct((S5_GROUP, n_groups, S5_GROUP), F32),
           jax.ShapeDtypeStruct((1, hg_w), F32)],
        name="prep",
    )(s5_a_re[0], s5_a_im[0], s5_log_dt[0][:, None],
      jnp.swapaxes(s5_b_re[0], 1, 2), jnp.swapaxes(s5_b_im[0], 1, 2), s5_c_re[0], s5_c_im[0],
      hg_lb_logits)

    def tiles(v):
        return v.reshape((n_tiles, GROUPS_PER_TILE) + v.shape[1:])

    def bd(v):
        return _block_diag(tiles(v))

    def bd_t(v):
        return _block_diag(tiles(jnp.swapaxes(v, 1, 2)))
    lam2_re_t = jnp.broadcast_to(lam2_re.reshape(n_tiles, 1, STATE_TILE), (n_tiles, SEQS, STATE_TILE))
    lam2_im_t = jnp.broadcast_to(lam2_im.reshape(n_tiles, 1, STATE_TILE), (n_tiles, SEQS, STATE_TILE))
    bk = jnp.concatenate([jnp.concatenate([bd(lbt_re), bd(lbt_im)], axis=-1),
                          jnp.concatenate([bd(bbt_re), bd(bbt_im)], axis=-1)], axis=1).astype(BF16)
    ck = jnp.concatenate([jnp.concatenate([bd_t(s5_c_re[0]), bd_t(-s5_c_im[0])], axis=1),
                          jnp.concatenate([bd_t(cl_re), bd_t(-cl_im)], axis=1)], axis=-1).astype(BF16)
    gk = bd(jnp.swapaxes(cb, 0, 1)).astype(BF16)

    n_steps = grid[0] * grid[1]

    def row_slices(n_rows, n_cols):
        rows = CAST_ROWS_MIN * -(-n_rows // (n_steps * CAST_ROWS_MIN))
        while n_rows % rows:
            rows += CAST_ROWS_MIN
        return pl.BlockSpec((rows, n_cols),
                            lambda g, t: (jnp.minimum(g * grid[1] + t, n_rows // rows - 1), 0))
    up_slices = row_slices(d_model, 2 * d_ff)
    down_slices = row_slices(d_ff, d_model)
    x1, w_up_b, w_down_b = pl.pallas_call(
        functools.partial(_mixer_kernel, widths=(s5_w, hg_w, d_model)),
        grid=grid,
        in_specs=[
            _seq_spec(d_model, TT),
            _const_spec((1, d_model)),
            _const_spec((d_model, n_in)),
            _const_spec((n_tiles, SEQS, STATE_TILE)),
            _const_spec((n_tiles, SEQS, STATE_TILE)),
            _const_spec((n_tiles, 2 * LANES, 2 * STATE_TILE)),
            _const_spec((n_tiles, 2 * STATE_TILE, 2 * LANES)),
            _const_spec((n_tiles, LANES, LANES)),
            _const_spec((1, s5_w)),
            _const_spec((s5_w, s5_w)),
            _const_spec((1, s5_w)),
            _const_spec((1, hg_w)),
            _const_spec((1, hg_w)),
            _const_spec((s5_w, d_model)),
            _const_spec((hg_w, d_model)),
            _const_spec((d_model, d_model)),
            up_slices, down_slices],
        out_specs=[pl.BlockSpec((1, ROWS, d_model), lambda g, t: (g, t, 0)), up_slices, down_slices],
        out_shape=[jax.ShapeDtypeStruct((bsz // SEQS, seq * SEQS, d_model), F32),
                   jax.ShapeDtypeStruct((d_model, 2 * d_ff), BF16),
                   jax.ShapeDtypeStruct((d_ff, d_model), BF16)],
        scratch_shapes=[pltpu.VMEM((d_model // LANES, ROWS, LANES), F32),
                        pltpu.VMEM((ROWS // 2, n_tiles * 2 * STATE_TILE), F32),
                        pltpu.VMEM((n_tiles, 2, SEQS, STATE_TILE), F32),
                        pltpu.VMEM((n_tiles, SEQS, LANES), F32),
                        pltpu.VMEM((n_heads, HG_HEAD_DIM, SEQS * HG_HEAD_DIM), F32),
                        pltpu.VMEM((ROWS, hg_w), F32),
                        pltpu.VMEM((4, ROWS, hg_w), F32)],
        compiler_params=cparams,
        name="mixer",
    )(x, g_mix[0][None], w_in[0].astype(BF16), lam2_re_t, lam2_im_t, bk, ck, gk,
      s5_d[0][None], w_glu[0].astype(BF16), b_glu[0][None], lb, hg_norm_gain[0][None],
      w_pa[0].astype(BF16), w_pb[0].astype(BF16), w_out[0].astype(BF16), w_up[0], w_down[0])

    return pl.pallas_call(
        _ffn_kernel,
        grid=(bsz // SEQS, seq // FFN_TT),
        in_specs=[pl.BlockSpec((1, FFN_ROWS, d_model), lambda g, t: (g, t, 0)),
                  _const_spec((1, d_model)),
                  _const_spec((d_model, 2 * d_ff)),
                  _const_spec((CONV_W, 2 * d_ff)),
                  _const_spec((1, 2 * d_ff)),
                  _const_spec((d_ff, d_model)),
                  _const_spec((1, d_model))],
        out_specs=_seq_spec(d_model, FFN_TT),
        out_shape=jax.ShapeDtypeStruct((bsz, seq, d_model), F32),
        scratch_shapes=[pltpu.VMEM((2 * SEQS, 2 * d_ff), F32),
                        pltpu.VMEM((d_model // LANES, FFN_ROWS, LANES), F32),
                        pltpu.VMEM((FFN_ROWS, d_ff), BF16)],
        compiler_params=cparams,
        name="ffn",
    )(x1, g_ffn[0][None], w_up_b, w_conv[0], b_conv[0][None], w_down_b, g_final[None])
```

```python
import functools

import jax
import jax.numpy as jnp
from jax import lax
from jax.experimental import pallas as pl
from jax.experimental.pallas import tpu as pltpu

EPS = 1e-6
S5_GROUP = 16
S5_STATE = 64
HG_HEAD_DIM = 128
CONV_W = 3

LANES = 128
SEQS = 8
TT = 64
ROWS = SEQS * TT
GROUPS_PER_TILE = LANES // S5_GROUP
STATE_TILE = GROUPS_PER_TILE * S5_STATE
HG_FAST_DECAY_LIMIT = 60.0
CAST_ROWS_MIN = 16
FFN_TT = 128
FFN_ROWS = SEQS * FFN_TT
FFN_CHUNK = 256
FFN_OUT_SPLIT = 2
MIX_HEAD_CHUNKS = 4
V7X_VMEM_BYTES = 64 * 1024 * 1024
VMEM_LIMIT = V7X_VMEM_BYTES * 7 // 8

F32 = jnp.float32
BF16 = jnp.bfloat16


def _dot(a, b):
    return jnp.dot(a, b, preferred_element_type=F32)


def _dot_t0(a, b):
    return lax.dot_general(a, b, (((0,), (0,)), ((), ())), preferred_element_type=F32)


def _dot_t1(a, b):
    return lax.dot_general(a, b, (((1,), (1,)), ((), ())), preferred_element_type=F32)


def _rmsnorm(x, gain):
    return x * lax.rsqrt(jnp.mean(x * x, axis=-1, keepdims=True) + EPS) * gain


def _sigmoid(x):
    return jax.nn.sigmoid(x)


def _const_spec(shape):
    nd = len(shape)
    return pl.BlockSpec(shape, lambda *_: (0,) * nd, pipeline_mode=pl.Buffered(1))


def _seq_spec(width, tokens):
    return pl.BlockSpec((SEQS, tokens, width), lambda g, t: (g, t, 0))


def _interleave(seq_ref, slab_scr, t0, nt):
    n_slabs = slab_scr.shape[0]
    for b in range(SEQS):
        for s in range(n_slabs):
            slab_scr[s, pl.ds(t0 * SEQS + b, nt, stride=SEQS), :] = (
                seq_ref[b, t0:t0 + nt, s * LANES:(s + 1) * LANES])
    return jnp.concatenate([slab_scr[s, t0 * SEQS:(t0 + nt) * SEQS, :] for s in range(n_slabs)], axis=-1)


def _deinterleave(val, slab_scr, seq_ref, row0=0):
    n_slabs = slab_scr.shape[0]
    rows = val.shape[0]
    for s in range(n_slabs):
        slab_scr[s, row0:row0 + rows, :] = val[:, s * LANES:(s + 1) * LANES]
    for b in range(SEQS):
        seq_ref[b, row0 // SEQS:(row0 + rows) // SEQS, :] = jnp.concatenate(
            [slab_scr[s, pl.ds(row0 + b, rows // SEQS, stride=SEQS), :] for s in range(n_slabs)],
            axis=-1)


def _prep_kernel(a_re_ref, a_im_ref, log_dt_ref, bt_re_ref, bt_im_ref, c_re_ref, c_im_ref, lbl_ref,
                 lam2_re_ref, lam2_im_ref, bbt_re_ref, bbt_im_ref, lbt_re_ref, lbt_im_ref,
                 cl_re_ref, cl_im_ref, cb_ref, lb_ref, *, layer):
    a_re = a_re_ref[...]
    a_im = a_im_ref[...]
    dt = jnp.exp(log_dt_ref[...])
    mag = jnp.exp(a_re * dt)
    ang = a_im * dt
    lb_re = mag * jnp.cos(ang)
    lb_im = mag * jnp.sin(ang)
    den = a_re * a_re + a_im * a_im
    n_re = lb_re - 1.0
    n_im = lb_im
    co_re = (n_re * a_re + n_im * a_im) / den
    co_im = (n_im * a_re - n_re * a_im) / den
    lam2_re_ref[...] = lb_re * lb_re - lb_im * lb_im
    lam2_im_ref[...] = 2.0 * lb_re * lb_im
    bt_re = bt_re_ref[...]
    bt_im = bt_im_ref[...]
    bbt_re = co_re[:, None, :] * bt_re - co_im[:, None, :] * bt_im
    bbt_im = co_re[:, None, :] * bt_im + co_im[:, None, :] * bt_re
    bbt_re_ref[...] = bbt_re
    bbt_im_ref[...] = bbt_im
    l_re = lb_re[:, None, :]
    l_im = lb_im[:, None, :]
    lbt_re_ref[...] = l_re * bbt_re - l_im * bbt_im
    lbt_im_ref[...] = l_re * bbt_im + l_im * bbt_re
    c_re = c_re_ref[...]
    c_im = c_im_ref[...]
    cl_re_ref[...] = c_re * l_re - c_im * l_im
    cl_im_ref[...] = c_re * l_im + c_im * l_re
    for k in range(bt_re.shape[1]):
        cb_ref[k] = jnp.sum(c_re * bbt_re[:, k:k + 1, :] - c_im * bbt_im[:, k:k + 1, :], axis=-1)
    logits = lbl_ref[...]
    m = jnp.max(logits, axis=0, keepdims=True)
    e = jnp.exp(logits - m)
    tot = jnp.sum(e, axis=0, keepdims=True)
    lb_ref[...] = jnp.sum(e[: layer + 1], axis=0, keepdims=True) / tot


def _s5_branch(za, lam2_re_ref, lam2_im_ref, bk_ref, ck_ref, gk_ref, d_ref, bu_scr, st_scr, ycar_scr):
    n_tiles = bk_ref.shape[0]
    width = za.shape[-1]
    pairs = TT // 2
    prows = pairs * SEQS
    za3 = za.reshape(pairs, 2 * SEQS, width)
    u_even = za3[:, :SEQS, :].reshape(prows, width)
    u_odd = za3[:, SEQS:, :].reshape(prows, width)
    ue_bf = u_even.astype(BF16)
    uo_bf = u_odd.astype(BF16)
    for k in range(n_tiles):
        lt = slice(k * LANES, (k + 1) * LANES)
        bu_scr[:, k * 2 * STATE_TILE:(k + 1) * 2 * STATE_TILE] = _dot(
            jnp.concatenate([ue_bf[:, lt], uo_bf[:, lt]], axis=-1), bk_ref[k])
    for k in range(n_tiles):
        lr = lam2_re_ref[k]
        li = lam2_im_ref[k]
        c_re = k * 2 * STATE_TILE
        c_im = c_re + STATE_TILE
        xr = st_scr[k, 0]
        xi = st_scr[k, 1]
        for p in range(pairs):
            r = slice(p * SEQS, (p + 1) * SEQS)
            nr = lr * xr - li * xi + bu_scr[r, c_re:c_re + STATE_TILE]
            ni = lr * xi + li * xr + bu_scr[r, c_im:c_im + STATE_TILE]
            bu_scr[r, c_re:c_re + STATE_TILE] = nr
            bu_scr[r, c_im:c_im + STATE_TILE] = ni
            xr, xi = nr, ni
        st_scr[k, 0] = xr
        st_scr[k, 1] = xi
    y_even, y_odd = [], []
    for k in range(n_tiles):
        lt = slice(k * LANES, (k + 1) * LANES)
        xk = bu_scr[:, k * 2 * STATE_TILE:(k + 1) * 2 * STATE_TILE].astype(BF16)
        yy = _dot(xk, ck_ref[k])
        y_odd.append(yy[:, :LANES])
        nxt = yy[:, LANES:]
        y_even.append(jnp.concatenate([ycar_scr[k], nxt[:prows - SEQS]], axis=0)
                      + _dot(ue_bf[:, lt], gk_ref[k]))
        ycar_scr[k] = nxt[prows - SEQS:]
    d = d_ref[...]
    y_even = jnp.concatenate(y_even, axis=-1) + d * u_even
    y_odd = jnp.concatenate(y_odd, axis=-1) + d * u_odd
    return jnp.concatenate([y_even.reshape(pairs, SEQS, width), y_odd.reshape(pairs, SEQS, width)],
                           axis=1).reshape(ROWS, width)


def _hgrn_branch(zq, zf, zi, zg, lb_ref, gain_ref, hst_scr):
    n_heads = hst_scr.shape[0]
    width = zq.shape[-1]
    lb = lb_ref[...]
    f = lb + (1.0 - lb) * _sigmoid(zf)
    lf = jnp.log(f)
    kk = 1.0 - f
    acc = lf[0:SEQS]
    parts = [acc]
    for t in range(1, TT):
        acc = acc + lf[t * SEQS:(t + 1) * SEQS]
        parts.append(acc)
    bcum = jnp.concatenate(parts, axis=0)
    b_last = parts[-1]
    qa = zq * _sigmoid(zq) * (HG_HEAD_DIM ** -0.5)
    qh = qa * jnp.exp(bcum)
    kh = (kk * jnp.exp(-bcum)).astype(BF16)
    ke = (kk.reshape(TT, SEQS, width) * jnp.exp(b_last[None] - bcum.reshape(TT, SEQS, width))
          ).reshape(ROWS, width)
    dec = jnp.exp(b_last)
    half = ROWS // 2
    dlt = (lax.broadcasted_iota(jnp.int32, (half, half), 1)
           - lax.broadcasted_iota(jnp.int32, (half, half), 0))
    same_seq = (dlt & (SEQS - 1)) == 0
    causal = same_seq & (dlt >= 0)
    seq_of_row = lax.broadcasted_iota(jnp.int32, (ROWS, HG_HEAD_DIM), 0) & (SEQS - 1)
    seq_of_col = lax.broadcasted_iota(jnp.int32, (HG_HEAD_DIM, ROWS), 1) & (SEQS - 1)
    intra, inter = [], []
    for h in range(n_heads):
        sl = slice(h * HG_HEAD_DIM, (h + 1) * HG_HEAD_DIM)
        q_t = qh[:, sl].T
        i_t = zi[:, sl].T.astype(BF16)
        q_tb = q_t.astype(BF16)
        k_e, k_l = kh[:half, sl], kh[half:, sl]
        sc_ee = jnp.where(causal, _dot(k_e, q_tb[:, :half]), 0.0).astype(BF16)
        sc_el = jnp.where(same_seq, _dot(k_e, q_tb[:, half:]), 0.0).astype(BF16)
        sc_ll = jnp.where(causal, _dot(k_l, q_tb[:, half:]), 0.0).astype(BF16)
        intra_t = jnp.concatenate(
            [_dot(i_t[:, :half], sc_ee),
             _dot(i_t, jnp.concatenate([sc_el, sc_ll], axis=0))], axis=1)
        q_aug_t = jnp.concatenate([jnp.where(seq_of_col == b, q_t, 0.0) for b in range(SEQS)],
                                  axis=0).astype(BF16)
        k_aug = jnp.concatenate([jnp.where(seq_of_row == b, ke[:, sl], 0.0) for b in range(SEQS)],
                                axis=-1).astype(BF16)
        st = hst_scr[h]
        inter_t = _dot(st.astype(BF16), q_aug_t)
        dec_row = jnp.concatenate([dec[b:b + 1, sl] for b in range(SEQS)], axis=-1)
        hst_scr[h] = st * dec_row + _dot(i_t, k_aug)
        intra.append(intra_t.T)
        inter.append(inter_t.T)
    o_inter = jnp.concatenate(inter, axis=-1)
    out_gate = zg * _sigmoid(zg)
    yb = _hgrn_head_norm(jnp.concatenate(intra, axis=-1) + o_inter, gain_ref) * out_gate
    return yb, (qa, kk, zi, bcum, b_last, o_inter, out_gate)


def _hgrn_head_norm(o, gain_ref):
    gain = gain_ref[...]
    return jnp.concatenate(
        [_rmsnorm(o[:, h * HG_HEAD_DIM:(h + 1) * HG_HEAD_DIM], gain[:, h * HG_HEAD_DIM:(h + 1) * HG_HEAD_DIM])
         for h in range(o.shape[-1] // HG_HEAD_DIM)], axis=-1)


def _hgrn_intra_exact(qa, kk, zi, bcum, fb_scr, oi_scr):
    width = qa.shape[-1]
    n_heads = width // HG_HEAD_DIM
    fb_scr[0] = qa
    fb_scr[1] = kk
    fb_scr[2] = zi
    fb_scr[3] = bcum

    def rows(j, t):
        return fb_scr[j, pl.ds(pl.multiple_of(t * SEQS, SEQS), SEQS), :]

    def outer(t, carry):
        q_t = rows(0, t)
        b_t = rows(3, t)

        def inner(s, acc):
            decay = jnp.where(s <= t, jnp.exp(jnp.minimum(b_t - rows(3, s), 0.0)), 0.0)
            p = q_t * rows(1, s) * decay
            i_s = rows(2, s)
            return acc + jnp.concatenate(
                [jnp.sum(p[:, h * HG_HEAD_DIM:(h + 1) * HG_HEAD_DIM], axis=-1, keepdims=True)
                 * i_s[:, h * HG_HEAD_DIM:(h + 1) * HG_HEAD_DIM] for h in range(n_heads)], axis=-1)

        acc = lax.fori_loop(0, TT, inner, jnp.zeros((SEQS, width), F32))
        oi_scr[pl.ds(pl.multiple_of(t * SEQS, SEQS), SEQS), :] = acc
        return carry

    lax.fori_loop(0, TT, outer, 0)


def _mixer_kernel(x_ref, gmix_ref, win_ref, lam2_re_ref, lam2_im_ref, bk_ref, ck_ref, gk_ref, d_ref,
                  wglu_ref, bglu_ref, lb_ref, gain_ref, wpa_ref, wpb_ref, wout_ref, wup_ref, wdown_ref,
                  out_ref, wup_bf_ref, wdown_bf_ref,
                  slab_scr, bu_scr, sst_scr, ycar_scr, hst_scr, oi_scr, fb_scr, *, widths):
    s5_w, hg_w, d_model = widths
    wup_bf_ref[...] = wup_ref[...].astype(BF16)
    wdown_bf_ref[...] = wdown_ref[...].astype(BF16)

    @pl.when(pl.program_id(1) == 0)
    def _():
        sst_scr[...] = jnp.zeros_like(sst_scr)
        ycar_scr[...] = jnp.zeros_like(ycar_scr)
        hst_scr[...] = jnp.zeros_like(hst_scr)

    nt = TT // MIX_HEAD_CHUNKS
    o = s5_w
    us = [_rmsnorm(_interleave(x_ref, slab_scr, c * nt, nt), gmix_ref[...]).astype(BF16)
          for c in range(MIX_HEAD_CHUNKS)]
    zf = jnp.concatenate([_dot(uc, win_ref[:, o + hg_w:o + 2 * hg_w]) for uc in us], axis=0)
    u = jnp.concatenate(us, axis=0)

    def proj(lo, width):
        return _dot(u, win_ref[:, lo:lo + width])

    yb, (qa, kk, zi, bcum, b_last, o_inter, out_gate) = _hgrn_branch(
        proj(o, hg_w), zf, proj(o + 2 * hg_w, hg_w), proj(o + 3 * hg_w, hg_w),
        lb_ref, gain_ref, hst_scr)
    y_s5 = _s5_branch(proj(0, s5_w), lam2_re_ref, lam2_im_ref, bk_ref, ck_ref, gk_ref, d_ref, bu_scr,
                      sst_scr, ycar_scr)
    o += 4 * hg_w
    gate_a = _sigmoid(proj(o, d_model))
    gate_b = _sigmoid(proj(o + d_model, d_model))

    ya = jax.nn.gelu(y_s5)
    ya = ya * _sigmoid(_dot(ya.astype(BF16), wglu_ref[...]) + bglu_ref[...])
    m_a = gate_a * _dot(ya.astype(BF16), wpa_ref[...])

    def finish(yb):
        m = m_a + gate_b * _dot(yb.astype(BF16), wpb_ref[...])
        x_res = jnp.concatenate([slab_scr[s] for s in range(slab_scr.shape[0])], axis=-1)
        out_ref[0] = x_res + _dot(m.astype(BF16), wout_ref[...])

    finish(yb)

    @pl.when(jnp.min(b_last) < -HG_FAST_DECAY_LIMIT)
    def _():
        _hgrn_intra_exact(qa, kk, zi, bcum, fb_scr, oi_scr)
        finish(_hgrn_head_norm(oi_scr[...] + o_inter, gain_ref) * out_gate)


def _conv_taps(h, prev, w, b):
    rows = h.shape[0]
    h1 = jnp.concatenate([prev[SEQS:], h[:rows - SEQS]], axis=0)
    h2 = jnp.concatenate([prev, h[:rows - 2 * SEQS]], axis=0)
    return h2 * w[0:1] + h1 * w[1:2] + h * w[2:3] + b


def _ffn_kernel(x_ref, gf_ref, wup_ref, wconv_ref, bconv_ref, wdown_ref, gfin_ref, out_ref,
                halo_scr, slab_scr, act_scr):
    d_ff = wdown_ref.shape[0]
    n_chunks = d_ff // FFN_CHUNK

    @pl.when(pl.program_id(1) == 0)
    def _():
        halo_scr[...] = jnp.zeros_like(halo_scr)

    x1 = x_ref[0]
    u = _rmsnorm(x1, gf_ref[...]).astype(BF16)
    for c in range(n_chunks):
        cols = []
        for part in range(2):
            j = part * n_chunks + c
            cs = slice(j * FFN_CHUNK, (j + 1) * FFN_CHUNK)
            h = _dot(u, wup_ref[:, cs])
            cols.append(_conv_taps(h, halo_scr[:, cs], wconv_ref[:, cs], bconv_ref[:, cs]))
            halo_scr[:, cs] = h[FFN_ROWS - 2 * SEQS:]
        gate, val = cols
        act_scr[:, c * FFN_CHUNK:(c + 1) * FFN_CHUNK] = (gate * _sigmoid(gate) * val).astype(BF16)
    blk = FFN_ROWS // FFN_OUT_SPLIT
    for i in range(FFN_OUT_SPLIT):
        rs = slice(i * blk, (i + 1) * blk)
        y = x1[rs] + _dot(act_scr[rs, :], wdown_ref[...])
        _deinterleave(_rmsnorm(y, gfin_ref[...]), slab_scr, out_ref, i * blk)


def _block_diag(vals):
    nt, g, a, b = vals.shape
    same = jnp.arange(g)[:, None, None, None] == jnp.arange(g)[None, None, :, None]
    return jnp.where(same, vals[:, :, :, None, :], 0).reshape(nt, g * a, g * b)


def kernel(x, g_mix, w_in, s5_a_re, s5_a_im, s5_log_dt, s5_b_re, s5_b_im, s5_c_re, s5_c_im, s5_d,
           w_glu, b_glu, hg_lb_logits, hg_norm_gain, w_pa, w_pb, w_out, g_ffn, w_up, w_conv,
           b_conv, w_down, g_final):
    depth = w_in.shape[0]
    assert depth == 1, "kernel is written for a single layer"
    bsz, seq, d_model = x.shape
    s5_w = s5_d.shape[-1]
    hg_w = hg_norm_gain.shape[-1]
    d_ff = w_down.shape[1]
    n_in = w_in.shape[-1]
    n_groups = s5_a_re.shape[1]
    n_tiles = s5_w // LANES
    n_heads = hg_w // HG_HEAD_DIM
    assert bsz % SEQS == 0 and seq % TT == 0 and seq % FFN_TT == 0
    assert s5_w % LANES == 0 and d_model % LANES == 0 and d_ff % FFN_CHUNK == 0
    assert hg_w % HG_HEAD_DIM == 0 and n_in == s5_w + 4 * hg_w + 2 * d_model
    grid = (bsz // SEQS, seq // TT)
    cparams = pltpu.CompilerParams(vmem_limit_bytes=VMEM_LIMIT,
                                   dimension_semantics=("arbitrary", "arbitrary"))

    gps = jax.ShapeDtypeStruct((n_groups, S5_STATE), F32)
    ghps = jax.ShapeDtypeStruct((n_groups, S5_GROUP, S5_STATE), F32)
    lam2_re, lam2_im, bbt_re, bbt_im, lbt_re, lbt_im, cl_re, cl_im, cb, lb = pl.pallas_call(
        functools.partial(_prep_kernel, layer=0),
        out_shape=[gps] * 2 + [ghps] * 6
        + [jax.ShapeDtypeStruct((S5_GROUP, n_groups, S5_GROUP), F32),
           jax.ShapeDtypeStruct((1, hg_w), F32)],
        name="prep",
    )(s5_a_re[0], s5_a_im[0], s5_log_dt[0][:, None],
      jnp.swapaxes(s5_b_re[0], 1, 2), jnp.swapaxes(s5_b_im[0], 1, 2), s5_c_re[0], s5_c_im[0],
      hg_lb_logits)

    def tiles(v):
        return v.reshape((n_tiles, GROUPS_PER_TILE) + v.shape[1:])

    def bd(v):
        return _block_diag(tiles(v))

    def bd_t(v):
        return _block_diag(tiles(jnp.swapaxes(v, 1, 2)))
    lam2_re_t = jnp.broadcast_to(lam2_re.reshape(n_tiles, 1, STATE_TILE), (n_tiles, SEQS, STATE_TILE))
    lam2_im_t = jnp.broadcast_to(lam2_im.reshape(n_tiles, 1, STATE_TILE), (n_tiles, SEQS, STATE_TILE))
    bk = jnp.concatenate([jnp.concatenate([bd(lbt_re), bd(lbt_im)], axis=-1),
                          jnp.concatenate([bd(bbt_re), bd(bbt_im)], axis=-1)], axis=1).astype(BF16)
    ck = jnp.concatenate([jnp.concatenate([bd_t(s5_c_re[0]), bd_t(-s5_c_im[0])], axis=1),
                          jnp.concatenate([bd_t(cl_re), bd_t(-cl_im)], axis=1)], axis=-1).astype(BF16)
    gk = bd(jnp.swapaxes(cb, 0, 1)).astype(BF16)

    n_steps = grid[0] * grid[1]

    def row_slices(n_rows, n_cols):
        rows = CAST_ROWS_MIN * -(-n_rows // (n_steps * CAST_ROWS_MIN))
        while n_rows % rows:
            rows += CAST_ROWS_MIN
        return pl.BlockSpec((rows, n_cols),
                            lambda g, t: (jnp.minimum(g * grid[1] + t, n_rows // rows - 1), 0))
    up_slices = row_slices(d_model, 2 * d_ff)
    down_slices = row_slices(d_ff, d_model)
    x1, w_up_b, w_down_b = pl.pallas_call(
        functools.partial(_mixer_kernel, widths=(s5_w, hg_w, d_model)),
        grid=grid,
        in_specs=[
            _seq_spec(d_model, TT),
            _const_spec((1, d_model)),
            _const_spec((d_model, n_in)),
            _const_spec((n_tiles, SEQS, STATE_TILE)),
            _const_spec((n_tiles, SEQS, STATE_TILE)),
            _const_spec((n_tiles, 2 * LANES, 2 * STATE_TILE)),
            _const_spec((n_tiles, 2 * STATE_TILE, 2 * LANES)),
            _const_spec((n_tiles, LANES, LANES)),
            _const_spec((1, s5_w)),
            _const_spec((s5_w, s5_w)),
            _const_spec((1, s5_w)),
            _const_spec((1, hg_w)),
            _const_spec((1, hg_w)),
            _const_spec((s5_w, d_model)),
            _const_spec((hg_w, d_model)),
            _const_spec((d_model, d_model)),
            up_slices, down_slices],
        out_specs=[pl.BlockSpec((1, ROWS, d_model), lambda g, t: (g, t, 0)), up_slices, down_slices],
        out_shape=[jax.ShapeDtypeStruct((bsz // SEQS, seq * SEQS, d_model), F32),
                   jax.ShapeDtypeStruct((d_model, 2 * d_ff), BF16),
                   jax.ShapeDtypeStruct((d_ff, d_model), BF16)],
        scratch_shapes=[pltpu.VMEM((d_model // LANES, ROWS, LANES), F32),
                        pltpu.VMEM((ROWS // 2, n_tiles * 2 * STATE_TILE), F32),
                        pltpu.VMEM((n_tiles, 2, SEQS, STATE_TILE), F32),
                        pltpu.VMEM((n_tiles, SEQS, LANES), F32),
                        pltpu.VMEM((n_heads, HG_HEAD_DIM, SEQS * HG_HEAD_DIM), F32),
                        pltpu.VMEM((ROWS, hg_w), F32),
                        pltpu.VMEM((4, ROWS, hg_w), F32)],
        compiler_params=cparams,
        name="mixer",
    )(x, g_mix[0][None], w_in[0].astype(BF16), lam2_re_t, lam2_im_t, bk, ck, gk,
      s5_d[0][None], w_glu[0].astype(BF16), b_glu[0][None], lb, hg_norm_gain[0][None],
      w_pa[0].astype(BF16), w_pb[0].astype(BF16), w_out[0].astype(BF16), w_up[0], w_down[0])

    return pl.pallas_call(
        _ffn_kernel,
        grid=(bsz // SEQS, seq // FFN_TT),
        in_specs=[pl.BlockSpec((1, FFN_ROWS, d_model), lambda g, t: (g, t, 0)),
                  _const_spec((1, d_model)),
                  _const_spec((d_model, 2 * d_ff)),
                  _const_spec((CONV_W, 2 * d_ff)),
                  _const_spec((1, 2 * d_ff)),
                  _const_spec((d_ff, d_model)),
                  _const_spec((1, d_model))],
        out_specs=_seq_spec(d_model, FFN_TT),
        out_shape=jax.ShapeDtypeStruct((bsz, seq, d_model), F32),
        scratch_shapes=[pltpu.VMEM((2 * SEQS, 2 * d_ff), F32),
                        pltpu.VMEM((d_model // LANES, FFN_ROWS, LANES), F32),
                        pltpu.VMEM((FFN_ROWS, d_ff), BF16)],
        compiler_params=cparams,
        name="ffn",
    )(x1, g_ffn[0][None], w_up_b, w_conv[0], b_conv[0][None], w_down_b, g_final[None])
```

```python
import functools

import jax
import jax.numpy as jnp
from jax import lax
from jax.experimental import pallas as pl
from jax.experimental.pallas import tpu as pltpu

EPS = 1e-6
S5_GROUP = 16
S5_STATE = 64
HG_HEAD_DIM = 128
CONV_W = 3

LANES = 128
SEQS = 8
TT = 64
ROWS = SEQS * TT
GROUPS_PER_TILE = LANES // S5_GROUP
STATE_TILE = GROUPS_PER_TILE * S5_STATE
HG_FAST_DECAY_LIMIT = 60.0
CAST_ROWS_MIN = 16
FFN_TT = 128
FFN_ROWS = SEQS * FFN_TT
FFN_CHUNK = 256
FFN_OUT_SPLIT = 2
MIX_HEAD_CHUNKS = 4
V7X_VMEM_BYTES = 64 * 1024 * 1024
VMEM_LIMIT = V7X_VMEM_BYTES * 7 // 8

F32 = jnp.float32
BF16 = jnp.bfloat16


def _dot(a, b):
    return jnp.dot(a, b, preferred_element_type=F32)


def _dot_t0(a, b):
    return lax.dot_general(a, b, (((0,), (0,)), ((), ())), preferred_element_type=F32)


def _dot_t1(a, b):
    return lax.dot_general(a, b, (((1,), (1,)), ((), ())), preferred_element_type=F32)


def _rmsnorm(x, gain):
    return x * lax.rsqrt(jnp.mean(x * x, axis=-1, keepdims=True) + EPS) * gain


def _sigmoid(x):
    return 0.5 * jnp.tanh(0.5 * x) + 0.5


def _const_spec(shape):
    nd = len(shape)
    return pl.BlockSpec(shape, lambda *_: (0,) * nd, pipeline_mode=pl.Buffered(1))


def _seq_spec(width, tokens):
    return pl.BlockSpec((SEQS, tokens, width), lambda g, t: (g, t, 0))


def _interleave(seq_ref, slab_scr, t0, nt):
    n_slabs = slab_scr.shape[0]
    for b in range(SEQS):
        for s in range(n_slabs):
            slab_scr[s, pl.ds(t0 * SEQS + b, nt, stride=SEQS), :] = (
                seq_ref[b, t0:t0 + nt, s * LANES:(s + 1) * LANES])
    return jnp.concatenate([slab_scr[s, t0 * SEQS:(t0 + nt) * SEQS, :] for s in range(n_slabs)], axis=-1)


def _deinterleave(val, slab_scr, seq_ref, row0=0):
    n_slabs = slab_scr.shape[0]
    rows = val.shape[0]
    for s in range(n_slabs):
        slab_scr[s, row0:row0 + rows, :] = val[:, s * LANES:(s + 1) * LANES]
    for b in range(SEQS):
        seq_ref[b, row0 // SEQS:(row0 + rows) // SEQS, :] = jnp.concatenate(
            [slab_scr[s, pl.ds(row0 + b, rows // SEQS, stride=SEQS), :] for s in range(n_slabs)],
            axis=-1)


def _prep_kernel(a_re_ref, a_im_ref, log_dt_ref, bt_re_ref, bt_im_ref, c_re_ref, c_im_ref, lbl_ref,
                 lam2_re_ref, lam2_im_ref, bbt_re_ref, bbt_im_ref, lbt_re_ref, lbt_im_ref,
                 cl_re_ref, cl_im_ref, cb_ref, lb_ref, *, layer):
    a_re = a_re_ref[...]
    a_im = a_im_ref[...]
    dt = jnp.exp(log_dt_ref[...])
    mag = jnp.exp(a_re * dt)
    ang = a_im * dt
    lb_re = mag * jnp.cos(ang)
    lb_im = mag * jnp.sin(ang)
    den = a_re * a_re + a_im * a_im
    n_re = lb_re - 1.0
    n_im = lb_im
    co_re = (n_re * a_re + n_im * a_im) / den
    co_im = (n_im * a_re - n_re * a_im) / den
    lam2_re_ref[...] = lb_re * lb_re - lb_im * lb_im
    lam2_im_ref[...] = 2.0 * lb_re * lb_im
    bt_re = bt_re_ref[...]
    bt_im = bt_im_ref[...]
    bbt_re = co_re[:, None, :] * bt_re - co_im[:, None, :] * bt_im
    bbt_im = co_re[:, None, :] * bt_im + co_im[:, None, :] * bt_re
    bbt_re_ref[...] = bbt_re
    bbt_im_ref[...] = bbt_im
    l_re = lb_re[:, None, :]
    l_im = lb_im[:, None, :]
    lbt_re_ref[...] = l_re * bbt_re - l_im * bbt_im
    lbt_im_ref[...] = l_re * bbt_im + l_im * bbt_re
    c_re = c_re_ref[...]
    c_im = c_im_ref[...]
    cl_re_ref[...] = c_re * l_re - c_im * l_im
    cl_im_ref[...] = c_re * l_im + c_im * l_re
    for k in range(bt_re.shape[1]):
        cb_ref[k] = jnp.sum(c_re * bbt_re[:, k:k + 1, :] - c_im * bbt_im[:, k:k + 1, :], axis=-1)
    logits = lbl_ref[...]
    m = jnp.max(logits, axis=0, keepdims=True)
    e = jnp.exp(logits - m)
    tot = jnp.sum(e, axis=0, keepdims=True)
    lb_ref[...] = jnp.sum(e[: layer + 1], axis=0, keepdims=True) / tot


def _s5_branch(za, lam2_re_ref, lam2_im_ref, bk_ref, ck_ref, gk_ref, d_ref, bu_scr, st_scr, ycar_scr):
    n_tiles = bk_ref.shape[0]
    width = za.shape[-1]
    pairs = TT // 2
    prows = pairs * SEQS
    za3 = za.reshape(pairs, 2 * SEQS, width)
    u_even = za3[:, :SEQS, :].reshape(prows, width)
    u_odd = za3[:, SEQS:, :].reshape(prows, width)
    ue_bf = u_even.astype(BF16)
    uo_bf = u_odd.astype(BF16)
    for k in range(n_tiles):
        lt = slice(k * LANES, (k + 1) * LANES)
        bu_scr[:, k * 2 * STATE_TILE:(k + 1) * 2 * STATE_TILE] = _dot(
            jnp.concatenate([ue_bf[:, lt], uo_bf[:, lt]], axis=-1), bk_ref[k])
    for k in range(n_tiles):
        lr = lam2_re_ref[k]
        li = lam2_im_ref[k]
        c_re = k * 2 * STATE_TILE
        c_im = c_re + STATE_TILE
        xr = st_scr[k, 0]
        xi = st_scr[k, 1]
        for p in range(pairs):
            r = slice(p * SEQS, (p + 1) * SEQS)
            nr = lr * xr - li * xi + bu_scr[r, c_re:c_re + STATE_TILE]
            ni = lr * xi + li * xr + bu_scr[r, c_im:c_im + STATE_TILE]
            bu_scr[r, c_re:c_re + STATE_TILE] = nr
            bu_scr[r, c_im:c_im + STATE_TILE] = ni
            xr, xi = nr, ni
        st_scr[k, 0] = xr
        st_scr[k, 1] = xi
    y_even, y_odd = [], []
    for k in range(n_tiles):
        lt = slice(k * LANES, (k + 1) * LANES)
        xk = bu_scr[:, k * 2 * STATE_TILE:(k + 1) * 2 * STATE_TILE].astype(BF16)
        yy = _dot(xk, ck_ref[k])
        y_odd.append(yy[:, :LANES])
        nxt = yy[:, LANES:]
        y_even.append(jnp.concatenate([ycar_scr[k], nxt[:prows - SEQS]], axis=0)
                      + _dot(ue_bf[:, lt], gk_ref[k]))
        ycar_scr[k] = nxt[prows - SEQS:]
    d = d_ref[...]
    y_even = jnp.concatenate(y_even, axis=-1) + d * u_even
    y_odd = jnp.concatenate(y_odd, axis=-1) + d * u_odd
    return jnp.concatenate([y_even.reshape(pairs, SEQS, width), y_odd.reshape(pairs, SEQS, width)],
                           axis=1).reshape(ROWS, width)


def _hgrn_branch(zq, zf, zi, zg, lb_ref, gain_ref, hst_scr):
    n_heads = hst_scr.shape[0]
    width = zq.shape[-1]
    lb = lb_ref[...]
    f = lb + (1.0 - lb) * _sigmoid(zf)
    lf = jnp.log(f)
    kk = 1.0 - f
    acc = lf[0:SEQS]
    parts = [acc]
    for t in range(1, TT):
        acc = acc + lf[t * SEQS:(t + 1) * SEQS]
        parts.append(acc)
    bcum = jnp.concatenate(parts, axis=0)
    b_last = parts[-1]
    qa = zq * _sigmoid(zq) * (HG_HEAD_DIM ** -0.5)
    qh = qa * jnp.exp(bcum)
    kh = (kk * jnp.exp(-bcum)).astype(BF16)
    ke = (kk.reshape(TT, SEQS, width) * jnp.exp(b_last[None] - bcum.reshape(TT, SEQS, width))
          ).reshape(ROWS, width)
    dec = jnp.exp(b_last)
    half = ROWS // 2
    dlt = (lax.broadcasted_iota(jnp.int32, (half, half), 1)
           - lax.broadcasted_iota(jnp.int32, (half, half), 0))
    same_seq = (dlt & (SEQS - 1)) == 0
    causal = same_seq & (dlt >= 0)
    seq_of_row = lax.broadcasted_iota(jnp.int32, (ROWS, HG_HEAD_DIM), 0) & (SEQS - 1)
    seq_of_col = lax.broadcasted_iota(jnp.int32, (HG_HEAD_DIM, ROWS), 1) & (SEQS - 1)
    intra, inter = [], []
    for h in range(n_heads):
        sl = slice(h * HG_HEAD_DIM, (h + 1) * HG_HEAD_DIM)
        q_t = qh[:, sl].T
        i_t = zi[:, sl].T.astype(BF16)
        q_tb = q_t.astype(BF16)
        k_e, k_l = kh[:half, sl], kh[half:, sl]
        sc_ee = jnp.where(causal, _dot(k_e, q_tb[:, :half]), 0.0).astype(BF16)
        sc_el = jnp.where(same_seq, _dot(k_e, q_tb[:, half:]), 0.0).astype(BF16)
        sc_ll = jnp.where(causal, _dot(k_l, q_tb[:, half:]), 0.0).astype(BF16)
        intra_t = jnp.concatenate(
            [_dot(i_t[:, :half], sc_ee),
             _dot(i_t, jnp.concatenate([sc_el, sc_ll], axis=0))], axis=1)
        q_aug_t = jnp.concatenate([jnp.where(seq_of_col == b, q_t, 0.0) for b in range(SEQS)],
                                  axis=0).astype(BF16)
        k_aug = jnp.concatenate([jnp.where(seq_of_row == b, ke[:, sl], 0.0) for b in range(SEQS)],
                                axis=-1).astype(BF16)
        st = hst_scr[h]
        inter_t = _dot(st.astype(BF16), q_aug_t)
        dec_row = jnp.concatenate([dec[b:b + 1, sl] for b in range(SEQS)], axis=-1)
        hst_scr[h] = st * dec_row + _dot(i_t, k_aug)
        intra.append(intra_t.T)
        inter.append(inter_t.T)
    o_inter = jnp.concatenate(inter, axis=-1)
    out_gate = zg * _sigmoid(zg)
    yb = _hgrn_head_norm(jnp.concatenate(intra, axis=-1) + o_inter, gain_ref) * out_gate
    return yb, (qa, kk, zi, bcum, b_last, o_inter, out_gate)


def _hgrn_head_norm(o, gain_ref):
    gain = gain_ref[...]
    return jnp.concatenate(
        [_rmsnorm(o[:, h * HG_HEAD_DIM:(h + 1) * HG_HEAD_DIM], gain[:, h * HG_HEAD_DIM:(h + 1) * HG_HEAD_DIM])
         for h in range(o.shape[-1] // HG_HEAD_DIM)], axis=-1)


def _hgrn_intra_exact(qa, kk, zi, bcum, fb_scr, oi_scr):
    width = qa.shape[-1]
    n_heads = width // HG_HEAD_DIM
    fb_scr[0] = qa
    fb_scr[1] = kk
    fb_scr[2] = zi
    fb_scr[3] = bcum

    def rows(j, t):
        return fb_scr[j, pl.ds(pl.multiple_of(t * SEQS, SEQS), SEQS), :]

    def outer(t, carry):
        q_t = rows(0, t)
        b_t = rows(3, t)

        def inner(s, acc):
            decay = jnp.where(s <= t, jnp.exp(jnp.minimum(b_t - rows(3, s), 0.0)), 0.0)
            p = q_t * rows(1, s) * decay
            i_s = rows(2, s)
            return acc + jnp.concatenate(
                [jnp.sum(p[:, h * HG_HEAD_DIM:(h + 1) * HG_HEAD_DIM], axis=-1, keepdims=True)
                 * i_s[:, h * HG_HEAD_DIM:(h + 1) * HG_HEAD_DIM] for h in range(n_heads)], axis=-1)

        acc = lax.fori_loop(0, TT, inner, jnp.zeros((SEQS, width), F32))
        oi_scr[pl.ds(pl.multiple_of(t * SEQS, SEQS), SEQS), :] = acc
        return carry

    lax.fori_loop(0, TT, outer, 0)


def _mixer_kernel(x_ref, gmix_ref, win_ref, lam2_re_ref, lam2_im_ref, bk_ref, ck_ref, gk_ref, d_ref,
                  wglu_ref, bglu_ref, lb_ref, gain_ref, wpa_ref, wpb_ref, wout_ref, wup_ref, wdown_ref,
                  out_ref, wup_bf_ref, wdown_bf_ref,
                  slab_scr, bu_scr, sst_scr, ycar_scr, hst_scr, oi_scr, fb_scr, *, widths):
    s5_w, hg_w, d_model = widths
    wup_bf_ref[...] = wup_ref[...].astype(BF16)
    wdown_bf_ref[...] = wdown_ref[...].astype(BF16)

    @pl.when(pl.program_id(1) == 0)
    def _():
        sst_scr[...] = jnp.zeros_like(sst_scr)
        ycar_scr[...] = jnp.zeros_like(ycar_scr)
        hst_scr[...] = jnp.zeros_like(hst_scr)

    nt = TT // MIX_HEAD_CHUNKS
    o = s5_w
    us = [_rmsnorm(_interleave(x_ref, slab_scr, c * nt, nt), gmix_ref[...]).astype(BF16)
          for c in range(MIX_HEAD_CHUNKS)]
    zf = jnp.concatenate([_dot(uc, win_ref[:, o + hg_w:o + 2 * hg_w]) for uc in us], axis=0)
    u = jnp.concatenate(us, axis=0)

    def proj(lo, width):
        return _dot(u, win_ref[:, lo:lo + width])

    yb, (qa, kk, zi, bcum, b_last, o_inter, out_gate) = _hgrn_branch(
        proj(o, hg_w), zf, proj(o + 2 * hg_w, hg_w), proj(o + 3 * hg_w, hg_w),
        lb_ref, gain_ref, hst_scr)
    y_s5 = _s5_branch(proj(0, s5_w), lam2_re_ref, lam2_im_ref, bk_ref, ck_ref, gk_ref, d_ref, bu_scr,
                      sst_scr, ycar_scr)
    o += 4 * hg_w
    gate_a = _sigmoid(proj(o, d_model))
    gate_b = _sigmoid(proj(o + d_model, d_model))

    ya = jax.nn.gelu(y_s5)
    ya = ya * _sigmoid(_dot(ya.astype(BF16), wglu_ref[...]) + bglu_ref[...])
    m_a = gate_a * _dot(ya.astype(BF16), wpa_ref[...])

    def finish(yb):
        m = m_a + gate_b * _dot(yb.astype(BF16), wpb_ref[...])
        x_res = jnp.concatenate([slab_scr[s] for s in range(slab_scr.shape[0])], axis=-1)
        out_ref[0] = x_res + _dot(m.astype(BF16), wout_ref[...])

    finish(yb)

    @pl.when(jnp.min(b_last) < -HG_FAST_DECAY_LIMIT)
    def _():
        _hgrn_intra_exact(qa, kk, zi, bcum, fb_scr, oi_scr)
        finish(_hgrn_head_norm(oi_scr[...] + o_inter, gain_ref) * out_gate)


def _conv_taps(h, prev, w, b):
    rows = h.shape[0]
    h1 = jnp.concatenate([prev[SEQS:], h[:rows - SEQS]], axis=0)
    h2 = jnp.concatenate([prev, h[:rows - 2 * SEQS]], axis=0)
    return h2 * w[0:1] + h1 * w[1:2] + h * w[2:3] + b


def _ffn_kernel(x_ref, gf_ref, wup_ref, wconv_ref, bconv_ref, wdown_ref, gfin_ref, out_ref,
                halo_scr, slab_scr, act_scr):
    d_ff = wdown_ref.shape[0]
    n_chunks = d_ff // FFN_CHUNK

    @pl.when(pl.program_id(1) == 0)
    def _():
        halo_scr[...] = jnp.zeros_like(halo_scr)

    x1 = x_ref[0]
    u = _rmsnorm(x1, gf_ref[...]).astype(BF16)
    for c in range(n_chunks):
        cols = []
        for part in range(2):
            j = part * n_chunks + c
            cs = slice(j * FFN_CHUNK, (j + 1) * FFN_CHUNK)
            h = _dot(u, wup_ref[:, cs])
            cols.append(_conv_taps(h, halo_scr[:, cs], wconv_ref[:, cs], bconv_ref[:, cs]))
            halo_scr[:, cs] = h[FFN_ROWS - 2 * SEQS:]
        gate, val = cols
        act_scr[:, c * FFN_CHUNK:(c + 1) * FFN_CHUNK] = (gate * _sigmoid(gate) * val).astype(BF16)
    blk = FFN_ROWS // FFN_OUT_SPLIT
    for i in range(FFN_OUT_SPLIT):
        rs = slice(i * blk, (i + 1) * blk)
        y = x1[rs] + _dot(act_scr[rs, :], wdown_ref[...])
        _deinterleave(_rmsnorm(y, gfin_ref[...]), slab_scr, out_ref, i * blk)


def _block_diag(vals):
    nt, g, a, b = vals.shape
    same = jnp.arange(g)[:, None, None, None] == jnp.arange(g)[None, None, :, None]
    return jnp.where(same, vals[:, :, :, None, :], 0).reshape(nt, g * a, g * b)


def kernel(x, g_mix, w_in, s5_a_re, s5_a_im, s5_log_dt, s5_b_re, s5_b_im, s5_c_re, s5_c_im, s5_d,
           w_glu, b_glu, hg_lb_logits, hg_norm_gain, w_pa, w_pb, w_out, g_ffn, w_up, w_conv,
           b_conv, w_down, g_final):
    depth = w_in.shape[0]
    assert depth == 1, "kernel is written for a single layer"
    bsz, seq, d_model = x.shape
    s5_w = s5_d.shape[-1]
    hg_w = hg_norm_gain.shape[-1]
    d_ff = w_down.shape[1]
    n_in = w_in.shape[-1]
    n_groups = s5_a_re.shape[1]
    n_tiles = s5_w // LANES
    n_heads = hg_w // HG_HEAD_DIM
    assert bsz % SEQS == 0 and seq % TT == 0 and seq % FFN_TT == 0
    assert s5_w % LANES == 0 and d_model % LANES == 0 and d_ff % FFN_CHUNK == 0
    assert hg_w % HG_HEAD_DIM == 0 and n_in == s5_w + 4 * hg_w + 2 * d_model
    grid = (bsz // SEQS, seq // TT)
    cparams = pltpu.CompilerParams(vmem_limit_bytes=VMEM_LIMIT,
                                   dimension_semantics=("arbitrary", "arbitrary"))

    gps = jax.ShapeDtypeStruct((n_groups, S5_STATE), F32)
    ghps = jax.ShapeDtypeStruct((n_groups, S5_GROUP, S5_STATE), F32)
    lam2_re, lam2_im, bbt_re, bbt_im, lbt_re, lbt_im, cl_re, cl_im, cb, lb = pl.pallas_call(
        functools.partial(_prep_kernel, layer=0),
        out_shape=[gps] * 2 + [ghps] * 6
        + [jax.ShapeDtypeStruct((S5_GROUP, n_groups, S5_GROUP), F32),
           jax.ShapeDtypeStruct((1, hg_w), F32)],
        name="prep",
    )(s5_a_re[0], s5_a_im[0], s5_log_dt[0][:, None],
      jnp.swapaxes(s5_b_re[0], 1, 2), jnp.swapaxes(s5_b_im[0], 1, 2), s5_c_re[0], s5_c_im[0],
      hg_lb_logits)

    def tiles(v):
        return v.reshape((n_tiles, GROUPS_PER_TILE) + v.shape[1:])

    def bd(v):
        return _block_diag(tiles(v))

    def bd_t(v):
        return _block_diag(tiles(jnp.swapaxes(v, 1, 2)))
    lam2_re_t = jnp.broadcast_to(lam2_re.reshape(n_tiles, 1, STATE_TILE), (n_tiles, SEQS, STATE_TILE))
    lam2_im_t = jnp.broadcast_to(lam2_im.reshape(n_tiles, 1, STATE_TILE), (n_tiles, SEQS, STATE_TILE))
    bk = jnp.concatenate([jnp.concatenate([bd(lbt_re), bd(lbt_im)], axis=-1),
                          jnp.concatenate([bd(bbt_re), bd(bbt_im)], axis=-1)], axis=1).astype(BF16)
    ck = jnp.concatenate([jnp.concatenate([bd_t(s5_c_re[0]), bd_t(-s5_c_im[0])], axis=1),
                          jnp.concatenate([bd_t(cl_re), bd_t(-cl_im)], axis=1)], axis=-1).astype(BF16)
    gk = bd(jnp.swapaxes(cb, 0, 1)).astype(BF16)

    n_steps = grid[0] * grid[1]

    def row_slices(n_rows, n_cols):
        rows = CAST_ROWS_MIN * -(-n_rows // (n_steps * CAST_ROWS_MIN))
        while n_rows % rows:
            rows += CAST_ROWS_MIN
        return pl.BlockSpec((rows, n_cols),
                            lambda g, t: (jnp.minimum(g * grid[1] + t, n_rows // rows - 1), 0))
    up_slices = row_slices(d_model, 2 * d_ff)
    down_slices = row_slices(d_ff, d_model)
    x1, w_up_b, w_down_b = pl.pallas_call(
        functools.partial(_mixer_kernel, widths=(s5_w, hg_w, d_model)),
        grid=grid,
        in_specs=[
            _seq_spec(d_model, TT),
            _const_spec((1, d_model)),
            _const_spec((d_model, n_in)),
            _const_spec((n_tiles, SEQS, STATE_TILE)),
            _const_spec((n_tiles, SEQS, STATE_TILE)),
            _const_spec((n_tiles, 2 * LANES, 2 * STATE_TILE)),
            _const_spec((n_tiles, 2 * STATE_TILE, 2 * LANES)),
            _const_spec((n_tiles, LANES, LANES)),
            _const_spec((1, s5_w)),
            _const_spec((s5_w, s5_w)),
            _const_spec((1, s5_w)),
            _const_spec((1, hg_w)),
            _const_spec((1, hg_w)),
            _const_spec((s5_w, d_model)),
            _const_spec((hg_w, d_model)),
            _const_spec((d_model, d_model)),
            up_slices, down_slices],
        out_specs=[pl.BlockSpec((1, ROWS, d_model), lambda g, t: (g, t, 0)), up_slices, down_slices],
        out_shape=[jax.ShapeDtypeStruct((bsz // SEQS, seq * SEQS, d_model), F32),
                   jax.ShapeDtypeStruct((d_model, 2 * d_ff), BF16),
                   jax.ShapeDtypeStruct((d_ff, d_model), BF16)],
        scratch_shapes=[pltpu.VMEM((d_model // LANES, ROWS, LANES), F32),
                        pltpu.VMEM((ROWS // 2, n_tiles * 2 * STATE_TILE), F32),
                        pltpu.VMEM((n_tiles, 2, SEQS, STATE_TILE), F32),
                        pltpu.VMEM((n_tiles, SEQS, LANES), F32),
                        pltpu.VMEM((n_heads, HG_HEAD_DIM, SEQS * HG_HEAD_DIM), F32),
                        pltpu.VMEM((ROWS, hg_w), F32),
                        pltpu.VMEM((4, ROWS, hg_w), F32)],
        compiler_params=cparams,
        name="mixer",
    )(x, g_mix[0][None], w_in[0].astype(BF16), lam2_re_t, lam2_im_t, bk, ck, gk,
      s5_d[0][None], w_glu[0].astype(BF16), b_glu[0][None], lb, hg_norm_gain[0][None],
      w_pa[0].astype(BF16), w_pb[0].astype(BF16), w_out[0].astype(BF16), w_up[0], w_down[0])

    return pl.pallas_call(
        _ffn_kernel,
        grid=(bsz // SEQS, seq // FFN_TT),
        in_specs=[pl.BlockSpec((1, FFN_ROWS, d_model), lambda g, t: (g, t, 0)),
                  _const_spec((1, d_model)),
                  _const_spec((d_model, 2 * d_ff)),
                  _const_spec((CONV_W, 2 * d_ff)),
                  _const_spec((1, 2 * d_ff)),
                  _const_spec((d_ff, d_model)),
                  _const_spec((1, d_model))],
        out_specs=_seq_spec(d_model, FFN_TT),
        out_shape=jax.ShapeDtypeStruct((bsz, seq, d_model), F32),
        scratch_shapes=[pltpu.VMEM((2 * SEQS, 2 * d_ff), F32),
                        pltpu.VMEM((d_model // LANES, FFN_ROWS, LANES), F32),
                        pltpu.VMEM((FFN_ROWS, d_ff), BF16)],
        compiler_params=cparams,
        name="ffn",
    )(x1, g_ffn[0][None], w_up_b, w_conv[0], b_conv[0][None], w_down_b, g_final[None])
```

```python
import functools

import jax
import jax.numpy as jnp
from jax import lax
from jax.experimental import pallas as pl
from jax.experimental.pallas import tpu as pltpu

EPS = 1e-6
S5_GROUP = 16
S5_STATE = 64
HG_HEAD_DIM = 128
CONV_W = 3

LANES = 128
SEQS = 8
TT = 64
ROWS = SEQS * TT
GROUPS_PER_TILE = LANES // S5_GROUP
STATE_TILE = GROUPS_PER_TILE * S5_STATE
HG_FAST_DECAY_LIMIT = 60.0
CAST_ROWS_MIN = 16
FFN_TT = 128
FFN_ROWS = SEQS * FFN_TT
FFN_CHUNK = 256
FFN_OUT_SPLIT = 2
MIX_HEAD_CHUNKS = 4
V7X_VMEM_BYTES = 64 * 1024 * 1024
VMEM_LIMIT = V7X_VMEM_BYTES * 7 // 8

F32 = jnp.float32
BF16 = jnp.bfloat16


def _dot(a, b):
    return jnp.dot(a, b, preferred_element_type=F32)


def _dot_t0(a, b):
    return lax.dot_general(a, b, (((0,), (0,)), ((), ())), preferred_element_type=F32)


def _dot_t1(a, b):
    return lax.dot_general(a, b, (((1,), (1,)), ((), ())), preferred_element_type=F32)


def _rmsnorm(x, gain):
    return x * lax.rsqrt(jnp.mean(x * x, axis=-1, keepdims=True) + EPS) * gain


def _sigmoid(x):
    return jax.nn.sigmoid(x)


def _const_spec(shape):
    nd = len(shape)
    return pl.BlockSpec(shape, lambda *_: (0,) * nd, pipeline_mode=pl.Buffered(1))


def _seq_spec(width, tokens):
    return pl.BlockSpec((SEQS, tokens, width), lambda g, t: (g, t, 0))


def _interleave(seq_ref, slab_scr, t0, nt):
    n_slabs = slab_scr.shape[0]
    for b in range(SEQS):
        for s in range(n_slabs):
            slab_scr[s, pl.ds(t0 * SEQS + b, nt, stride=SEQS), :] = (
                seq_ref[b, t0:t0 + nt, s * LANES:(s + 1) * LANES])
    return jnp.concatenate([slab_scr[s, t0 * SEQS:(t0 + nt) * SEQS, :] for s in range(n_slabs)], axis=-1)


def _deinterleave(val, slab_scr, seq_ref, row0=0):
    n_slabs = slab_scr.shape[0]
    rows = val.shape[0]
    for s in range(n_slabs):
        slab_scr[s, row0:row0 + rows, :] = val[:, s * LANES:(s + 1) * LANES]
    for b in range(SEQS):
        seq_ref[b, row0 // SEQS:(row0 + rows) // SEQS, :] = jnp.concatenate(
            [slab_scr[s, pl.ds(row0 + b, rows // SEQS, stride=SEQS), :] for s in range(n_slabs)],
            axis=-1)


def _prep_kernel(a_re_ref, a_im_ref, log_dt_ref, bt_re_ref, bt_im_ref, c_re_ref, c_im_ref, lbl_ref,
                 lam2_re_ref, lam2_im_ref, bbt_re_ref, bbt_im_ref, lbt_re_ref, lbt_im_ref,
                 cl_re_ref, cl_im_ref, cb_ref, lb_ref, *, layer):
    a_re = a_re_ref[...]
    a_im = a_im_ref[...]
    dt = jnp.exp(log_dt_ref[...])
    mag = jnp.exp(a_re * dt)
    ang = a_im * dt
    lb_re = mag * jnp.cos(ang)
    lb_im = mag * jnp.sin(ang)
    den = a_re * a_re + a_im * a_im
    n_re = lb_re - 1.0
    n_im = lb_im
    co_re = (n_re * a_re + n_im * a_im) / den
    co_im = (n_im * a_re - n_re * a_im) / den
    lam2_re_ref[...] = lb_re * lb_re - lb_im * lb_im
    lam2_im_ref[...] = 2.0 * lb_re * lb_im
    bt_re = bt_re_ref[...]
    bt_im = bt_im_ref[...]
    bbt_re = co_re[:, None, :] * bt_re - co_im[:, None, :] * bt_im
    bbt_im = co_re[:, None, :] * bt_im + co_im[:, None, :] * bt_re
    bbt_re_ref[...] = bbt_re
    bbt_im_ref[...] = bbt_im
    l_re = lb_re[:, None, :]
    l_im = lb_im[:, None, :]
    lbt_re_ref[...] = l_re * bbt_re - l_im * bbt_im
    lbt_im_ref[...] = l_re * bbt_im + l_im * bbt_re
    c_re = c_re_ref[...]
    c_im = c_im_ref[...]
    cl_re_ref[...] = c_re * l_re - c_im * l_im
    cl_im_ref[...] = c_re * l_im + c_im * l_re
    for k in range(bt_re.shape[1]):
        cb_ref[k] = jnp.sum(c_re * bbt_re[:, k:k + 1, :] - c_im * bbt_im[:, k:k + 1, :], axis=-1)
    logits = lbl_ref[...]
    m = jnp.max(logits, axis=0, keepdims=True)
    e = jnp.exp(logits - m)
    tot = jnp.sum(e, axis=0, keepdims=True)
    lb_ref[...] = jnp.sum(e[: layer + 1], axis=0, keepdims=True) / tot


def _s5_branch(za, lam2_re_ref, lam2_im_ref, bk_ref, ck_ref, gk_ref, d_ref, bu_scr, st_scr, ycar_scr):
    n_tiles = bk_ref.shape[0]
    width = za.shape[-1]
    pairs = TT // 2
    prows = pairs * SEQS
    za3 = za.reshape(pairs, 2 * SEQS, width)
    u_even = za3[:, :SEQS, :].reshape(prows, width)
    u_odd = za3[:, SEQS:, :].reshape(prows, width)
    ue_bf = u_even.astype(BF16)
    uo_bf = u_odd.astype(BF16)
    for k in range(n_tiles):
        lt = slice(k * LANES, (k + 1) * LANES)
        bu_scr[:, k * 2 * STATE_TILE:(k + 1) * 2 * STATE_TILE] = _dot(
            jnp.concatenate([ue_bf[:, lt], uo_bf[:, lt]], axis=-1), bk_ref[k])
    for k in range(n_tiles):
        lr = lam2_re_ref[k]
        li = lam2_im_ref[k]
        c_re = k * 2 * STATE_TILE
        c_im = c_re + STATE_TILE
        xr = st_scr[k, 0]
        xi = st_scr[k, 1]
        for p in range(pairs):
            r = slice(p * SEQS, (p + 1) * SEQS)
            nr = lr * xr - li * xi + bu_scr[r, c_re:c_re + STATE_TILE]
            ni = lr * xi + li * xr + bu_scr[r, c_im:c_im + STATE_TILE]
            bu_scr[r, c_re:c_re + STATE_TILE] = nr
            bu_scr[r, c_im:c_im + STATE_TILE] = ni
            xr, xi = nr, ni
        st_scr[k, 0] = xr
        st_scr[k, 1] = xi
    y_even, y_odd = [], []
    for k in range(n_tiles):
        lt = slice(k * LANES, (k + 1) * LANES)
        xk = bu_scr[:, k * 2 * STATE_TILE:(k + 1) * 2 * STATE_TILE].astype(BF16)
        yy = _dot(xk, ck_ref[k])
        y_odd.append(yy[:, :LANES])
        nxt = yy[:, LANES:]
        y_even.append(jnp.concatenate([ycar_scr[k], nxt[:prows - SEQS]], axis=0)
                      + _dot(ue_bf[:, lt], gk_ref[k]))
        ycar_scr[k] = nxt[prows - SEQS:]
    d = d_ref[...]
    y_even = jnp.concatenate(y_even, axis=-1) + d * u_even
    y_odd = jnp.concatenate(y_odd, axis=-1) + d * u_odd
    return jnp.concatenate([y_even.reshape(pairs, SEQS, width), y_odd.reshape(pairs, SEQS, width)],
                           axis=1).reshape(ROWS, width)


def _hgrn_branch(zq, zf, zi, zg, lb_ref, gain_ref, hst_scr):
    n_heads = hst_scr.shape[0]
    width = zq.shape[-1]
    lb = lb_ref[...]
    f = lb + (1.0 - lb) * _sigmoid(zf)
    lf = jnp.log(f)
    kk = 1.0 - f
    acc = lf[0:SEQS]
    parts = [acc]
    for t in range(1, TT):
        acc = acc + lf[t * SEQS:(t + 1) * SEQS]
        parts.append(acc)
    bcum = jnp.concatenate(parts, axis=0)
    b_last = parts[-1]
    qa = zq * _sigmoid(zq) * (HG_HEAD_DIM ** -0.5)
    qh = qa * jnp.exp(bcum)
    kh = (kk * jnp.exp(-bcum)).astype(BF16)
    ke = (kk.reshape(TT, SEQS, width) * jnp.exp(b_last[None] - bcum.reshape(TT, SEQS, width))
          ).reshape(ROWS, width)
    dec = jnp.exp(b_last)
    half = ROWS // 2
    dlt = (lax.broadcasted_iota(jnp.int32, (half, half), 1)
           - lax.broadcasted_iota(jnp.int32, (half, half), 0))
    same_seq = (dlt & (SEQS - 1)) == 0
    causal = same_seq & (dlt >= 0)
    seq_of_row = lax.broadcasted_iota(jnp.int32, (ROWS, HG_HEAD_DIM), 0) & (SEQS - 1)
    seq_of_col = lax.broadcasted_iota(jnp.int32, (HG_HEAD_DIM, ROWS), 1) & (SEQS - 1)
    intra, inter = [], []
    for h in range(n_heads):
        sl = slice(h * HG_HEAD_DIM, (h + 1) * HG_HEAD_DIM)
        q_t = qh[:, sl].T
        i_t = zi[:, sl].T.astype(BF16)
        q_tb = q_t.astype(BF16)
        k_e, k_l = kh[:half, sl], kh[half:, sl]
        sc_ee = jnp.where(causal, _dot(k_e, q_tb[:, :half]), 0.0).astype(BF16)
        sc_el = jnp.where(same_seq, _dot(k_e, q_tb[:, half:]), 0.0).astype(BF16)
        sc_ll = jnp.where(causal, _dot(k_l, q_tb[:, half:]), 0.0).astype(BF16)
        intra_t = jnp.concatenate(
            [_dot(i_t[:, :half], sc_ee),
             _dot(i_t, jnp.concatenate([sc_el, sc_ll], axis=0))], axis=1)
        q_aug_t = jnp.concatenate([jnp.where(seq_of_col == b, q_t, 0.0) for b in range(SEQS)],
                                  axis=0).astype(BF16)
        k_aug = jnp.concatenate([jnp.where(seq_of_row == b, ke[:, sl], 0.0) for b in range(SEQS)],
                                axis=-1).astype(BF16)
        st = hst_scr[h]
        inter_t = _dot(st.astype(BF16), q_aug_t)
        dec_row = jnp.concatenate([dec[b:b + 1, sl] for b in range(SEQS)], axis=-1)
        hst_scr[h] = st * dec_row + _dot(i_t, k_aug)
        intra.append(intra_t.T)
        inter.append(inter_t.T)
    o_inter = jnp.concatenate(inter, axis=-1)
    out_gate = zg * _sigmoid(zg)
    yb = _hgrn_head_norm(jnp.concatenate(intra, axis=-1) + o_inter, gain_ref) * out_gate
    return yb, (qa, kk, zi, bcum, b_last, o_inter, out_gate)


def _hgrn_head_norm(o, gain_ref):
    gain = gain_ref[...]
    return jnp.concatenate(
        [_rmsnorm(o[:, h * HG_HEAD_DIM:(h + 1) * HG_HEAD_DIM], gain[:, h * HG_HEAD_DIM:(h + 1) * HG_HEAD_DIM])
         for h in range(o.shape[-1] // HG_HEAD_DIM)], axis=-1)


def _hgrn_intra_exact(qa, kk, zi, bcum, fb_scr, oi_scr):
    width = qa.shape[-1]
    n_heads = width // HG_HEAD_DIM
    fb_scr[0] = qa
    fb_scr[1] = kk
    fb_scr[2] = zi
    fb_scr[3] = bcum

    def rows(j, t):
        return fb_scr[j, pl.ds(pl.multiple_of(t * SEQS, SEQS), SEQS), :]

    def outer(t, carry):
        q_t = rows(0, t)
        b_t = rows(3, t)

        def inner(s, acc):
            decay = jnp.where(s <= t, jnp.exp(jnp.minimum(b_t - rows(3, s), 0.0)), 0.0)
            p = q_t * rows(1, s) * decay
            i_s = rows(2, s)
            return acc + jnp.concatenate(
                [jnp.sum(p[:, h * HG_HEAD_DIM:(h + 1) * HG_HEAD_DIM], axis=-1, keepdims=True)
                 * i_s[:, h * HG_HEAD_DIM:(h + 1) * HG_HEAD_DIM] for h in range(n_heads)], axis=-1)

        acc = lax.fori_loop(0, TT, inner, jnp.zeros((SEQS, width), F32))
        oi_scr[pl.ds(pl.multiple_of(t * SEQS, SEQS), SEQS), :] = acc
        return carry

    lax.fori_loop(0, TT, outer, 0)


def _mixer_kernel(x_ref, gmix_ref, win_ref, lam2_re_ref, lam2_im_ref, bk_ref, ck_ref, gk_ref, d_ref,
                  wglu_ref, bglu_ref, lb_ref, gain_ref, wpa_ref, wpb_ref, wout_ref, wup_ref, wdown_ref,
                  out_ref, wup_bf_ref, wdown_bf_ref,
                  slab_scr, bu_scr, sst_scr, ycar_scr, hst_scr, oi_scr, fb_scr, *, widths):
    s5_w, hg_w, d_model = widths
    wup_bf_ref[...] = wup_ref[...].astype(BF16)
    wdown_bf_ref[...] = wdown_ref[...].astype(BF16)

    @pl.when(pl.program_id(1) == 0)
    def _():
        sst_scr[...] = jnp.zeros_like(sst_scr)
        ycar_scr[...] = jnp.zeros_like(ycar_scr)
        hst_scr[...] = jnp.zeros_like(hst_scr)

    nt = TT // MIX_HEAD_CHUNKS
    o = s5_w
    us = [_rmsnorm(_interleave(x_ref, slab_scr, c * nt, nt), gmix_ref[...]).astype(BF16)
          for c in range(MIX_HEAD_CHUNKS)]
    zf = jnp.concatenate([_dot(uc, win_ref[:, o + hg_w:o + 2 * hg_w]) for uc in us], axis=0)
    u = jnp.concatenate(us, axis=0)

    def proj(lo, width):
        return _dot(u, win_ref[:, lo:lo + width])

    yb, (qa, kk, zi, bcum, b_last, o_inter, out_gate) = _hgrn_branch(
        proj(o, hg_w), zf, proj(o + 2 * hg_w, hg_w), proj(o + 3 * hg_w, hg_w),
        lb_ref, gain_ref, hst_scr)
    y_s5 = _s5_branch(proj(0, s5_w), lam2_re_ref, lam2_im_ref, bk_ref, ck_ref, gk_ref, d_ref, bu_scr,
                      sst_scr, ycar_scr)
    o += 4 * hg_w
    gate_a = _sigmoid(proj(o, d_model))
    gate_b = _sigmoid(proj(o + d_model, d_model))

    ya = jax.nn.gelu(y_s5)
    ya = ya * _sigmoid(_dot(ya.astype(BF16), wglu_ref[...]) + bglu_ref[...])
    m_a = gate_a * _dot(ya.astype(BF16), wpa_ref[...])

    def finish(yb):
        m = m_a + gate_b * _dot(yb.astype(BF16), wpb_ref[...])
        x_res = jnp.concatenate([slab_scr[s] for s in range(slab_scr.shape[0])], axis=-1)
        out_ref[0] = x_res + _dot(m.astype(BF16), wout_ref[...])

    finish(yb)

    @pl.when(jnp.min(b_last) < -HG_FAST_DECAY_LIMIT)
    def _():
        _hgrn_intra_exact(qa, kk, zi, bcum, fb_scr, oi_scr)
        finish(_hgrn_head_norm(oi_scr[...] + o_inter, gain_ref) * out_gate)


def _conv_taps(h, prev, w, b):
    rows = h.shape[0]
    h1 = jnp.concatenate([prev[SEQS:], h[:rows - SEQS]], axis=0)
    h2 = jnp.concatenate([prev, h[:rows - 2 * SEQS]], axis=0)
    return h2 * w[0:1] + h1 * w[1:2] + h * w[2:3] + b


def _ffn_kernel(x_ref, gf_ref, wup_ref, wconv_ref, bconv_ref, wdown_ref, gfin_ref, out_ref,
                halo_scr, slab_scr, act_scr):
    d_ff = wdown_ref.shape[0]
    n_chunks = d_ff // FFN_CHUNK

    @pl.when(pl.program_id(1) == 0)
    def _():
        halo_scr[...] = jnp.zeros_like(halo_scr)

    x1 = x_ref[0]
    u = _rmsnorm(x1, gf_ref[...]).astype(BF16)
    for c in range(n_chunks):
        cols = []
        for part in range(2):
            j = part * n_chunks + c
            cs = slice(j * FFN_CHUNK, (j + 1) * FFN_CHUNK)
            h = _dot(u, wup_ref[:, cs])
            cols.append(_conv_taps(h, halo_scr[:, cs], wconv_ref[:, cs], bconv_ref[:, cs]))
            halo_scr[:, cs] = h[FFN_ROWS - 2 * SEQS:]
        gate, val = cols
        act_scr[:, c * FFN_CHUNK:(c + 1) * FFN_CHUNK] = (gate * _sigmoid(gate) * val).astype(BF16)
    blk = FFN_ROWS // FFN_OUT_SPLIT
    for i in range(FFN_OUT_SPLIT):
        rs = slice(i * blk, (i + 1) * blk)
        y = x1[rs] + _dot(act_scr[rs, :], wdown_ref[...])
        _deinterleave(_rmsnorm(y, gfin_ref[...]), slab_scr, out_ref, i * blk)


def _block_diag(vals):
    nt, g, a, b = vals.shape
    same = jnp.arange(g)[:, None, None, None] == jnp.arange(g)[None, None, :, None]
    return jnp.where(same, vals[:, :, :, None, :], 0).reshape(nt, g * a, g * b)


def kernel(x, g_mix, w_in, s5_a_re, s5_a_im, s5_log_dt, s5_b_re, s5_b_im, s5_c_re, s5_c_im, s5_d,
           w_glu, b_glu, hg_lb_logits, hg_norm_gain, w_pa, w_pb, w_out, g_ffn, w_up, w_conv,
           b_conv, w_down, g_final):
    depth = w_in.shape[0]
    assert depth == 1, "kernel is written for a single layer"
    bsz, seq, d_model = x.shape
    s5_w = s5_d.shape[-1]
    hg_w = hg_norm_gain.shape[-1]
    d_ff = w_down.shape[1]
    n_in = w_in.shape[-1]
    n_groups = s5_a_re.shape[1]
    n_tiles = s5_w // LANES
    n_heads = hg_w // HG_HEAD_DIM
    assert bsz % SEQS == 0 and seq % TT == 0 and seq % FFN_TT == 0
    assert s5_w % LANES == 0 and d_model % LANES == 0 and d_ff % FFN_CHUNK == 0
    assert hg_w % HG_HEAD_DIM == 0 and n_in == s5_w + 4 * hg_w + 2 * d_model
    grid = (bsz // SEQS, seq // TT)
    cparams = pltpu.CompilerParams(vmem_limit_bytes=VMEM_LIMIT,
                                   dimension_semantics=("arbitrary", "arbitrary"))

    gps = jax.ShapeDtypeStruct((n_groups, S5_STATE), F32)
    ghps = jax.ShapeDtypeStruct((n_groups, S5_GROUP, S5_STATE), F32)
    lam2_re, lam2_im, bbt_re, bbt_im, lbt_re, lbt_im, cl_re, cl_im, cb, lb = pl.pallas_call(
        functools.partial(_prep_kernel, layer=0),
        out_shape=[gps] * 2 + [ghps] * 6
        + [jax.ShapeDtypeStruct((S5_GROUP, n_groups, S5_GROUP), F32),
           jax.ShapeDtypeStruct((1, hg_w), F32)],
        name="prep",
    )(s5_a_re[0], s5_a_im[0], s5_log_dt[0][:, None],
      jnp.swapaxes(s5_b_re[0], 1, 2), jnp.swapaxes(s5_b_im[0], 1, 2), s5_c_re[0], s5_c_im[0],
      hg_lb_logits)

    def tiles(v):
        return v.reshape((n_tiles, GROUPS_PER_TILE) + v.shape[1:])

    def bd(v):
        return _block_diag(tiles(v))

    def bd_t(v):
        return _block_diag(tiles(jnp.swapaxes(v, 1, 2)))
    lam2_re_t = jnp.broadcast_to(lam2_re.reshape(n_tiles, 1, STATE_TILE), (n_tiles, SEQS, STATE_TILE))
    lam2_im_t = jnp.broadcast_to(lam2_im.reshape(n_tiles, 1, STATE_TILE), (n_tiles, SEQS, STATE_TILE))
    bk = jnp.concatenate([jnp.concatenate([bd(lbt_re), bd(lbt_im)], axis=-1),
                          jnp.concatenate([bd(bbt_re), bd(bbt_im)], axis=-1)], axis=1).astype(BF16)
    ck = jnp.concatenate([jnp.concatenate([bd_t(s5_c_re[0]), bd_t(-s5_c_im[0])], axis=1),
                          jnp.concatenate([bd_t(cl_re), bd_t(-cl_im)], axis=1)], axis=-1).astype(BF16)
    gk = bd(jnp.swapaxes(cb, 0, 1)).astype(BF16)

    n_steps = grid[0] * grid[1]

    def row_slices(n_rows, n_cols):
        rows = CAST_ROWS_MIN * -(-n_rows // (n_steps * CAST_ROWS_MIN))
        while n_rows % rows:
            rows += CAST_ROWS_MIN
        return pl.BlockSpec((rows, n_cols),
                            lambda g, t: (jnp.minimum(g * grid[1] + t, n_rows // rows - 1), 0))
    up_slices = row_slices(d_model, 2 * d_ff)
    down_slices = row_slices(d_ff, d_model)
    x1, w_up_b, w_down_b = pl.pallas_call(
        functools.partial(_mixer_kernel, widths=(s5_w, hg_w, d_model)),
        grid=grid,
        in_specs=[
            _seq_spec(d_model, TT),
            _const_spec((1, d_model)),
            _const_spec((d_model, n_in)),
            _const_spec((n_tiles, SEQS, STATE_TILE)),
            _const_spec((n_tiles, SEQS, STATE_TILE)),
            _const_spec((n_tiles, 2 * LANES, 2 * STATE_TILE)),
            _const_spec((n_tiles, 2 * STATE_TILE, 2 * LANES)),
            _const_spec((n_tiles, LANES, LANES)),
            _const_spec((1, s5_w)),
            _const_spec((s5_w, s5_w)),
            _const_spec((1, s5_w)),
            _const_spec((1, hg_w)),
            _const_spec((1, hg_w)),
            _const_spec((s5_w, d_model)),
            _const_spec((hg_w, d_model)),
            _const_spec((d_model, d_model)),
            up_slices, down_slices],
        out_specs=[pl.BlockSpec((1, ROWS, d_model), lambda g, t: (g, t, 0)), up_slices, down_slices],
        out_shape=[jax.ShapeDtypeStruct((bsz // SEQS, seq * SEQS, d_model), F32),
                   jax.ShapeDtypeStruct((d_model, 2 * d_ff), BF16),
                   jax.ShapeDtypeStruct((d_ff, d_model), BF16)],
        scratch_shapes=[pltpu.VMEM((d_model // LANES, ROWS, LANES), F32),
                        pltpu.VMEM((ROWS // 2, n_tiles * 2 * STATE_TILE), F32),
                        pltpu.VMEM((n_tiles, 2, SEQS, STATE_TILE), F32),
                        pltpu.VMEM((n_tiles, SEQS, LANES), F32),
                        pltpu.VMEM((n_heads, HG_HEAD_DIM, SEQS * HG_HEAD_DIM), F32),
                        pltpu.VMEM((ROWS, hg_w), F32),
                        pltpu.VMEM((4, ROWS, hg_w), F32)],
        compiler_params=pltpu.CompilerParams(
            vmem_limit_bytes=VMEM_LIMIT, dimension_semantics=("arbitrary", "arbitrary"),
            allow_input_fusion=[i in (2, 9, 13, 14, 15) for i in range(18)]),
        name="mixer",
    )(x, g_mix[0][None], w_in[0].astype(BF16), lam2_re_t, lam2_im_t, bk, ck, gk,
      s5_d[0][None], w_glu[0].astype(BF16), b_glu[0][None], lb, hg_norm_gain[0][None],
      w_pa[0].astype(BF16), w_pb[0].astype(BF16), w_out[0].astype(BF16), w_up[0], w_down[0])

    return pl.pallas_call(
        _ffn_kernel,
        grid=(bsz // SEQS, seq // FFN_TT),
        in_specs=[pl.BlockSpec((1, FFN_ROWS, d_model), lambda g, t: (g, t, 0)),
                  _const_spec((1, d_model)),
                  _const_spec((d_model, 2 * d_ff)),
                  _const_spec((CONV_W, 2 * d_ff)),
                  _const_spec((1, 2 * d_ff)),
                  _const_spec((d_ff, d_model)),
                  _const_spec((1, d_model))],
        out_specs=_seq_spec(d_model, FFN_TT),
        out_shape=jax.ShapeDtypeStruct((bsz, seq, d_model), F32),
        scratch_shapes=[pltpu.VMEM((2 * SEQS, 2 * d_ff), F32),
                        pltpu.VMEM((d_model // LANES, FFN_ROWS, LANES), F32),
                        pltpu.VMEM((FFN_ROWS, d_ff), BF16)],
        compiler_params=cparams,
        name="ffn",
    )(x1, g_ffn[0][None], w_up_b, w_conv[0], b_conv[0][None], w_down_b, g_final[None])
```

```python
import functools

import jax
import jax.numpy as jnp
from jax import lax
from jax.experimental import pallas as pl
from jax.experimental.pallas import tpu as pltpu

EPS = 1e-6
S5_GROUP = 16
S5_STATE = 64
HG_HEAD_DIM = 128
CONV_W = 3

LANES = 128
SEQS = 8
TT = 64
ROWS = SEQS * TT
GROUPS_PER_TILE = LANES // S5_GROUP
STATE_TILE = GROUPS_PER_TILE * S5_STATE
HG_FAST_DECAY_LIMIT = 60.0
CAST_ROWS_MIN = 16
FFN_TT = 128
FFN_ROWS = SEQS * FFN_TT
FFN_CHUNK = 256
FFN_OUT_SPLIT = 2
MIX_HEAD_CHUNKS = 4
V7X_VMEM_BYTES = 64 * 1024 * 1024
VMEM_LIMIT = V7X_VMEM_BYTES * 7 // 8

F32 = jnp.float32
BF16 = jnp.bfloat16


def _dot(a, b):
    return jnp.dot(a, b, preferred_element_type=F32)


def _dot_t0(a, b):
    return lax.dot_general(a, b, (((0,), (0,)), ((), ())), preferred_element_type=F32)


def _dot_t1(a, b):
    return lax.dot_general(a, b, (((1,), (1,)), ((), ())), preferred_element_type=F32)


def _rmsnorm(x, gain):
    return x * lax.rsqrt(jnp.mean(x * x, axis=-1, keepdims=True) + EPS) * gain


def _sigmoid(x):
    return jax.nn.sigmoid(x)


def _const_spec(shape):
    nd = len(shape)
    return pl.BlockSpec(shape, lambda *_: (0,) * nd, pipeline_mode=pl.Buffered(1))


def _seq_spec(width, tokens):
    return pl.BlockSpec((SEQS, tokens, width), lambda g, t: (g, t, 0))


def _interleave(seq_ref, slab_scr, t0, nt):
    n_slabs = slab_scr.shape[0]
    for b in range(SEQS):
        for s in range(n_slabs):
            slab_scr[s, pl.ds(t0 * SEQS + b, nt, stride=SEQS), :] = (
                seq_ref[b, t0:t0 + nt, s * LANES:(s + 1) * LANES])
    return jnp.concatenate([slab_scr[s, t0 * SEQS:(t0 + nt) * SEQS, :] for s in range(n_slabs)], axis=-1)


def _deinterleave(val, slab_scr, seq_ref, row0=0):
    n_slabs = slab_scr.shape[0]
    rows = val.shape[0]
    for s in range(n_slabs):
        slab_scr[s, row0:row0 + rows, :] = val[:, s * LANES:(s + 1) * LANES]
    for b in range(SEQS):
        seq_ref[b, row0 // SEQS:(row0 + rows) // SEQS, :] = jnp.concatenate(
            [slab_scr[s, pl.ds(row0 + b, rows // SEQS, stride=SEQS), :] for s in range(n_slabs)],
            axis=-1)


def _zoh_pole(a_re, a_im, log_dt):
    dt = jnp.exp(log_dt)
    mag = jnp.exp(a_re * dt)
    ang = a_im * dt
    return mag * jnp.cos(ang), mag * jnp.sin(ang)


def _prep_kernel(a_re_ref, a_im_ref, log_dt_ref, a_re_col_ref, a_im_col_ref, log_dt_col_ref,
                 bt_re_ref, bt_im_ref, c_re_ref, c_im_ref, ct_re_ref, ct_im_ref, lbl_ref,
                 lam2_re_ref, lam2_im_ref, bk_ref, ck_ref, gk_ref, lb_ref, *, layer):
    a_re = a_re_ref[...]
    a_im = a_im_ref[...]
    lb_re, lb_im = _zoh_pole(a_re, a_im, log_dt_ref[...])
    den = a_re * a_re + a_im * a_im
    n_re = lb_re - 1.0
    n_im = lb_im
    co_re = (n_re * a_re + n_im * a_im) / den
    co_im = (n_im * a_re - n_re * a_im) / den
    lam2_re = lb_re * lb_re - lb_im * lb_im
    lam2_im = 2.0 * lb_re * lb_im
    bt_re = bt_re_ref[...]
    bt_im = bt_im_ref[...]
    bbt_re = co_re[:, None, :] * bt_re - co_im[:, None, :] * bt_im
    bbt_im = co_re[:, None, :] * bt_im + co_im[:, None, :] * bt_re
    l_re = lb_re[:, None, :]
    l_im = lb_im[:, None, :]
    lbt_re = l_re * bbt_re - l_im * bbt_im
    lbt_im = l_re * bbt_im + l_im * bbt_re
    c_re = c_re_ref[...]
    c_im = c_im_ref[...]
    lc_re, lc_im = _zoh_pole(a_re_col_ref[...], a_im_col_ref[...], log_dt_col_ref[...])
    ct_re = ct_re_ref[...]
    ct_im = ct_im_ref[...]
    clt_re = ct_re * lc_re - ct_im * lc_im
    clt_im = ct_re * lc_im + ct_im * lc_re
    n_groups, n_h, n_p = bt_re.shape
    bk_ref[...] = jnp.zeros_like(bk_ref)
    ck_ref[...] = jnp.zeros_like(ck_ref)
    gk_ref[...] = jnp.zeros_like(gk_ref)
    for g in range(n_groups):
        k, gl = divmod(g, GROUPS_PER_TILE)
        rows = slice(gl * n_h, (gl + 1) * n_h)
        for half, (m_re, m_im) in enumerate(((lbt_re, lbt_im), (bbt_re, bbt_im))):
            r = slice(half * LANES + gl * n_h, half * LANES + (gl + 1) * n_h)
            bk_ref[k, r, gl * n_p:(gl + 1) * n_p] = m_re[g]
            bk_ref[k, r, STATE_TILE + gl * n_p:STATE_TILE + (gl + 1) * n_p] = m_im[g]
        for half, (m_re, m_im) in enumerate(((ct_re, ct_im), (clt_re, clt_im))):
            c = slice(half * LANES + gl * n_h, half * LANES + (gl + 1) * n_h)
            ck_ref[k, gl * n_p:(gl + 1) * n_p, c] = m_re[g]
            ck_ref[k, STATE_TILE + gl * n_p:STATE_TILE + (gl + 1) * n_p, c] = -m_im[g]
        gk_ref[k, rows, rows] = jnp.sum(bbt_re[g][:, None, :] * c_re[g][None, :, :]
                                        - bbt_im[g][:, None, :] * c_im[g][None, :, :], axis=-1)
    for k in range(n_groups // GROUPS_PER_TILE):
        for ref, lam2 in ((lam2_re_ref, lam2_re), (lam2_im_ref, lam2_im)):
            row = jnp.concatenate([lam2[g:g + 1, :] for g in range(k * GROUPS_PER_TILE,
                                                                    (k + 1) * GROUPS_PER_TILE)], axis=-1)
            ref[k] = jnp.broadcast_to(row, (SEQS, STATE_TILE))
    logits = lbl_ref[...]
    m = jnp.max(logits, axis=0, keepdims=True)
    e = jnp.exp(logits - m)
    tot = jnp.sum(e, axis=0, keepdims=True)
    lb_ref[...] = jnp.sum(e[: layer + 1], axis=0, keepdims=True) / tot


def _s5_branch(za, lam2_re_ref, lam2_im_ref, bk_ref, ck_ref, gk_ref, d_ref, bu_scr, st_scr, ycar_scr):
    n_tiles = bk_ref.shape[0]
    width = za.shape[-1]
    pairs = TT // 2
    prows = pairs * SEQS
    za3 = za.reshape(pairs, 2 * SEQS, width)
    u_even = za3[:, :SEQS, :].reshape(prows, width)
    u_odd = za3[:, SEQS:, :].reshape(prows, width)
    ue_bf = u_even.astype(BF16)
    uo_bf = u_odd.astype(BF16)
    for k in range(n_tiles):
        lt = slice(k * LANES, (k + 1) * LANES)
        bu_scr[:, k * 2 * STATE_TILE:(k + 1) * 2 * STATE_TILE] = _dot(
            jnp.concatenate([ue_bf[:, lt], uo_bf[:, lt]], axis=-1), bk_ref[k])
    for k in range(n_tiles):
        lr = lam2_re_ref[k]
        li = lam2_im_ref[k]
        c_re = k * 2 * STATE_TILE
        c_im = c_re + STATE_TILE
        xr = st_scr[k, 0]
        xi = st_scr[k, 1]
        for p in range(pairs):
            r = slice(p * SEQS, (p + 1) * SEQS)
            nr = lr * xr - li * xi + bu_scr[r, c_re:c_re + STATE_TILE]
            ni = lr * xi + li * xr + bu_scr[r, c_im:c_im + STATE_TILE]
            bu_scr[r, c_re:c_re + STATE_TILE] = nr
            bu_scr[r, c_im:c_im + STATE_TILE] = ni
            xr, xi = nr, ni
        st_scr[k, 0] = xr
        st_scr[k, 1] = xi
    y_even, y_odd = [], []
    for k in range(n_tiles):
        lt = slice(k * LANES, (k + 1) * LANES)
        xk = bu_scr[:, k * 2 * STATE_TILE:(k + 1) * 2 * STATE_TILE].astype(BF16)
        yy = _dot(xk, ck_ref[k])
        y_odd.append(yy[:, :LANES])
        nxt = yy[:, LANES:]
        y_even.append(jnp.concatenate([ycar_scr[k], nxt[:prows - SEQS]], axis=0)
                      + _dot(ue_bf[:, lt], gk_ref[k]))
        ycar_scr[k] = nxt[prows - SEQS:]
    d = d_ref[...]
    y_even = jnp.concatenate(y_even, axis=-1) + d * u_even
    y_odd = jnp.concatenate(y_odd, axis=-1) + d * u_odd
    return jnp.concatenate([y_even.reshape(pairs, SEQS, width), y_odd.reshape(pairs, SEQS, width)],
                           axis=1).reshape(ROWS, width)


def _hgrn_branch(zq, zf, zi, zg, lb_ref, gain_ref, hst_scr):
    n_heads = hst_scr.shape[0]
    width = zq.shape[-1]
    lb = lb_ref[...]
    f = lb + (1.0 - lb) * _sigmoid(zf)
    lf = jnp.log(f)
    kk = 1.0 - f
    acc = lf[0:SEQS]
    parts = [acc]
    for t in range(1, TT):
        acc = acc + lf[t * SEQS:(t + 1) * SEQS]
        parts.append(acc)
    bcum = jnp.concatenate(parts, axis=0)
    b_last = parts[-1]
    qa = zq * _sigmoid(zq) * (HG_HEAD_DIM ** -0.5)
    qh = qa * jnp.exp(bcum)
    kh = (kk * jnp.exp(-bcum)).astype(BF16)
    ke = (kk.reshape(TT, SEQS, width) * jnp.exp(b_last[None] - bcum.reshape(TT, SEQS, width))
          ).reshape(ROWS, width)
    dec = jnp.exp(b_last)
    half = ROWS // 2
    dlt = (lax.broadcasted_iota(jnp.int32, (half, half), 1)
           - lax.broadcasted_iota(jnp.int32, (half, half), 0))
    same_seq = (dlt & (SEQS - 1)) == 0
    causal = same_seq & (dlt >= 0)
    seq_of_row = lax.broadcasted_iota(jnp.int32, (ROWS, HG_HEAD_DIM), 0) & (SEQS - 1)
    seq_of_col = lax.broadcasted_iota(jnp.int32, (HG_HEAD_DIM, ROWS), 1) & (SEQS - 1)
    intra, inter = [], []
    for h in range(n_heads):
        sl = slice(h * HG_HEAD_DIM, (h + 1) * HG_HEAD_DIM)
        q_t = qh[:, sl].T
        i_t = zi[:, sl].T.astype(BF16)
        q_tb = q_t.astype(BF16)
        k_e, k_l = kh[:half, sl], kh[half:, sl]
        sc_ee = jnp.where(causal, _dot(k_e, q_tb[:, :half]), 0.0).astype(BF16)
        sc_el = jnp.where(same_seq, _dot(k_e, q_tb[:, half:]), 0.0).astype(BF16)
        sc_ll = jnp.where(causal, _dot(k_l, q_tb[:, half:]), 0.0).astype(BF16)
        intra_t = jnp.concatenate(
            [_dot(i_t[:, :half], sc_ee),
             _dot(i_t, jnp.concatenate([sc_el, sc_ll], axis=0))], axis=1)
        q_aug_t = jnp.concatenate([jnp.where(seq_of_col == b, q_t, 0.0) for b in range(SEQS)],
                                  axis=0).astype(BF16)
        k_aug = jnp.concatenate([jnp.where(seq_of_row == b, ke[:, sl], 0.0) for b in range(SEQS)],
                                axis=-1).astype(BF16)
        st = hst_scr[h]
        inter_t = _dot(st.astype(BF16), q_aug_t)
        dec_row = jnp.concatenate([dec[b:b + 1, sl] for b in range(SEQS)], axis=-1)
        hst_scr[h] = st * dec_row + _dot(i_t, k_aug)
        intra.append(intra_t.T)
        inter.append(inter_t.T)
    o_inter = jnp.concatenate(inter, axis=-1)
    out_gate = zg * _sigmoid(zg)
    yb = _hgrn_head_norm(jnp.concatenate(intra, axis=-1) + o_inter, gain_ref) * out_gate
    return yb, (qa, kk, zi, bcum, b_last, o_inter, out_gate)


def _hgrn_head_norm(o, gain_ref):
    gain = gain_ref[...]
    return jnp.concatenate(
        [_rmsnorm(o[:, h * HG_HEAD_DIM:(h + 1) * HG_HEAD_DIM], gain[:, h * HG_HEAD_DIM:(h + 1) * HG_HEAD_DIM])
         for h in range(o.shape[-1] // HG_HEAD_DIM)], axis=-1)


def _hgrn_intra_exact(qa, kk, zi, bcum, fb_scr, oi_scr):
    width = qa.shape[-1]
    n_heads = width // HG_HEAD_DIM
    fb_scr[0] = qa
    fb_scr[1] = kk
    fb_scr[2] = zi
    fb_scr[3] = bcum

    def rows(j, t):
        return fb_scr[j, pl.ds(pl.multiple_of(t * SEQS, SEQS), SEQS), :]

    def outer(t, carry):
        q_t = rows(0, t)
        b_t = rows(3, t)

        def inner(s, acc):
            decay = jnp.where(s <= t, jnp.exp(jnp.minimum(b_t - rows(3, s), 0.0)), 0.0)
            p = q_t * rows(1, s) * decay
            i_s = rows(2, s)
            return acc + jnp.concatenate(
                [jnp.sum(p[:, h * HG_HEAD_DIM:(h + 1) * HG_HEAD_DIM], axis=-1, keepdims=True)
                 * i_s[:, h * HG_HEAD_DIM:(h + 1) * HG_HEAD_DIM] for h in range(n_heads)], axis=-1)

        acc = lax.fori_loop(0, TT, inner, jnp.zeros((SEQS, width), F32))
        oi_scr[pl.ds(pl.multiple_of(t * SEQS, SEQS), SEQS), :] = acc
        return carry

    lax.fori_loop(0, TT, outer, 0)


def _mixer_kernel(x_ref, gmix_ref, win_ref, lam2_re_ref, lam2_im_ref, bk_ref, ck_ref, gk_ref, d_ref,
                  wglu_ref, bglu_ref, lb_ref, gain_ref, wpa_ref, wpb_ref, wout_ref, wup_ref, wdown_ref,
                  out_ref, wup_bf_ref, wdown_bf_ref,
                  slab_scr, bu_scr, sst_scr, ycar_scr, hst_scr, oi_scr, fb_scr, *, widths):
    s5_w, hg_w, d_model = widths
    wup_bf_ref[...] = wup_ref[...].astype(BF16)
    wdown_bf_ref[...] = wdown_ref[...].astype(BF16)

    @pl.when(pl.program_id(1) == 0)
    def _():
        sst_scr[...] = jnp.zeros_like(sst_scr)
        ycar_scr[...] = jnp.zeros_like(ycar_scr)
        hst_scr[...] = jnp.zeros_like(hst_scr)

    nt = TT // MIX_HEAD_CHUNKS
    o = s5_w
    us = [_rmsnorm(_interleave(x_ref, slab_scr, c * nt, nt), gmix_ref[...]).astype(BF16)
          for c in range(MIX_HEAD_CHUNKS)]
    zf = jnp.concatenate([_dot(uc, win_ref[:, o + hg_w:o + 2 * hg_w]) for uc in us], axis=0)
    u = jnp.concatenate(us, axis=0)

    def proj(lo, width):
        return _dot(u, win_ref[:, lo:lo + width])

    yb, (qa, kk, zi, bcum, b_last, o_inter, out_gate) = _hgrn_branch(
        proj(o, hg_w), zf, proj(o + 2 * hg_w, hg_w), proj(o + 3 * hg_w, hg_w),
        lb_ref, gain_ref, hst_scr)
    y_s5 = _s5_branch(proj(0, s5_w), lam2_re_ref, lam2_im_ref, bk_ref, ck_ref, gk_ref, d_ref, bu_scr,
                      sst_scr, ycar_scr)
    o += 4 * hg_w
    gate_a = _sigmoid(proj(o, d_model))
    gate_b = _sigmoid(proj(o + d_model, d_model))

    ya = jax.nn.gelu(y_s5)
    ya = ya * _sigmoid(_dot(ya.astype(BF16), wglu_ref[...]) + bglu_ref[...])
    m_a = gate_a * _dot(ya.astype(BF16), wpa_ref[...])

    def finish(yb):
        m = m_a + gate_b * _dot(yb.astype(BF16), wpb_ref[...])
        x_res = jnp.concatenate([slab_scr[s] for s in range(slab_scr.shape[0])], axis=-1)
        out_ref[0] = x_res + _dot(m.astype(BF16), wout_ref[...])

    finish(yb)

    @pl.when(jnp.min(b_last) < -HG_FAST_DECAY_LIMIT)
    def _():
        _hgrn_intra_exact(qa, kk, zi, bcum, fb_scr, oi_scr)
        finish(_hgrn_head_norm(oi_scr[...] + o_inter, gain_ref) * out_gate)


def _conv_taps(h, prev, w, b):
    rows = h.shape[0]
    h1 = jnp.concatenate([prev[SEQS:], h[:rows - SEQS]], axis=0)
    h2 = jnp.concatenate([prev, h[:rows - 2 * SEQS]], axis=0)
    return h2 * w[0:1] + h1 * w[1:2] + h * w[2:3] + b


def _ffn_kernel(x_ref, gf_ref, wup_ref, wconv_ref, bconv_ref, wdown_ref, gfin_ref, out_ref,
                halo_scr, slab_scr, act_scr):
    d_ff = wdown_ref.shape[0]
    n_chunks = d_ff // FFN_CHUNK

    @pl.when(pl.program_id(1) == 0)
    def _():
        halo_scr[...] = jnp.zeros_like(halo_scr)

    x1 = x_ref[0]
    u = _rmsnorm(x1, gf_ref[...]).astype(BF16)
    for c in range(n_chunks):
        cols = []
        for part in range(2):
            j = part * n_chunks + c
            cs = slice(j * FFN_CHUNK, (j + 1) * FFN_CHUNK)
            h = _dot(u, wup_ref[:, cs])
            cols.append(_conv_taps(h, halo_scr[:, cs], wconv_ref[:, cs], bconv_ref[:, cs]))
            halo_scr[:, cs] = h[FFN_ROWS - 2 * SEQS:]
        gate, val = cols
        act_scr[:, c * FFN_CHUNK:(c + 1) * FFN_CHUNK] = (gate * _sigmoid(gate) * val).astype(BF16)
    blk = FFN_ROWS // FFN_OUT_SPLIT
    for i in range(FFN_OUT_SPLIT):
        rs = slice(i * blk, (i + 1) * blk)
        y = x1[rs] + _dot(act_scr[rs, :], wdown_ref[...])
        _deinterleave(_rmsnorm(y, gfin_ref[...]), slab_scr, out_ref, i * blk)


def kernel(x, g_mix, w_in, s5_a_re, s5_a_im, s5_log_dt, s5_b_re, s5_b_im, s5_c_re, s5_c_im, s5_d,
           w_glu, b_glu, hg_lb_logits, hg_norm_gain, w_pa, w_pb, w_out, g_ffn, w_up, w_conv,
           b_conv, w_down, g_final):
    depth = w_in.shape[0]
    assert depth == 1, "kernel is written for a single layer"
    bsz, seq, d_model = x.shape
    s5_w = s5_d.shape[-1]
    hg_w = hg_norm_gain.shape[-1]
    d_ff = w_down.shape[1]
    n_in = w_in.shape[-1]
    n_groups = s5_a_re.shape[1]
    n_tiles = s5_w // LANES
    n_heads = hg_w // HG_HEAD_DIM
    assert bsz % SEQS == 0 and seq % TT == 0 and seq % FFN_TT == 0
    assert s5_w % LANES == 0 and d_model % LANES == 0 and d_ff % FFN_CHUNK == 0
    assert hg_w % HG_HEAD_DIM == 0 and n_in == s5_w + 4 * hg_w + 2 * d_model
    grid = (bsz // SEQS, seq // TT)
    cparams = pltpu.CompilerParams(vmem_limit_bytes=VMEM_LIMIT,
                                   dimension_semantics=("arbitrary", "arbitrary"))

    lam_t = jax.ShapeDtypeStruct((n_tiles, SEQS, STATE_TILE), F32)
    lam2_re_t, lam2_im_t, bk, ck, gk, lb = pl.pallas_call(
        functools.partial(_prep_kernel, layer=0),
        out_shape=[lam_t, lam_t,
                   jax.ShapeDtypeStruct((n_tiles, 2 * LANES, 2 * STATE_TILE), F32),
                   jax.ShapeDtypeStruct((n_tiles, 2 * STATE_TILE, 2 * LANES), F32),
                   jax.ShapeDtypeStruct((n_tiles, LANES, LANES), F32),
                   jax.ShapeDtypeStruct((1, hg_w), F32)],
        name="prep",
    )(s5_a_re[0], s5_a_im[0], s5_log_dt[0][:, None],
      s5_a_re[0][:, :, None], s5_a_im[0][:, :, None], s5_log_dt[0][:, None, None],
      jnp.swapaxes(s5_b_re[0], 1, 2), jnp.swapaxes(s5_b_im[0], 1, 2), s5_c_re[0], s5_c_im[0],
      jnp.swapaxes(s5_c_re[0], 1, 2), jnp.swapaxes(s5_c_im[0], 1, 2), hg_lb_logits)
    bk = bk.astype(BF16)
    ck = ck.astype(BF16)
    gk = gk.astype(BF16)

    n_steps = grid[0] * grid[1]

    def row_slices(n_rows, n_cols):
        rows = CAST_ROWS_MIN * -(-n_rows // (n_steps * CAST_ROWS_MIN))
        while n_rows % rows:
            rows += CAST_ROWS_MIN
        return pl.BlockSpec((rows, n_cols),
                            lambda g, t: (jnp.minimum(g * grid[1] + t, n_rows // rows - 1), 0))
    up_slices = row_slices(d_model, 2 * d_ff)
    down_slices = row_slices(d_ff, d_model)
    x1, w_up_b, w_down_b = pl.pallas_call(
        functools.partial(_mixer_kernel, widths=(s5_w, hg_w, d_model)),
        grid=grid,
        in_specs=[
            _seq_spec(d_model, TT),
            _const_spec((1, d_model)),
            _const_spec((d_model, n_in)),
            _const_spec((n_tiles, SEQS, STATE_TILE)),
            _const_spec((n_tiles, SEQS, STATE_TILE)),
            _const_spec((n_tiles, 2 * LANES, 2 * STATE_TILE)),
            _const_spec((n_tiles, 2 * STATE_TILE, 2 * LANES)),
            _const_spec((n_tiles, LANES, LANES)),
            _const_spec((1, s5_w)),
            _const_spec((s5_w, s5_w)),
            _const_spec((1, s5_w)),
            _const_spec((1, hg_w)),
            _const_spec((1, hg_w)),
            _const_spec((s5_w, d_model)),
            _const_spec((hg_w, d_model)),
            _const_spec((d_model, d_model)),
            up_slices, down_slices],
        out_specs=[pl.BlockSpec((1, ROWS, d_model), lambda g, t: (g, t, 0)), up_slices, down_slices],
        out_shape=[jax.ShapeDtypeStruct((bsz // SEQS, seq * SEQS, d_model), F32),
                   jax.ShapeDtypeStruct((d_model, 2 * d_ff), BF16),
                   jax.ShapeDtypeStruct((d_ff, d_model), BF16)],
        scratch_shapes=[pltpu.VMEM((d_model // LANES, ROWS, LANES), F32),
                        pltpu.VMEM((ROWS // 2, n_tiles * 2 * STATE_TILE), F32),
                        pltpu.VMEM((n_tiles, 2, SEQS, STATE_TILE), F32),
                        pltpu.VMEM((n_tiles, SEQS, LANES), F32),
                        pltpu.VMEM((n_heads, HG_HEAD_DIM, SEQS * HG_HEAD_DIM), F32),
                        pltpu.VMEM((ROWS, hg_w), F32),
                        pltpu.VMEM((4, ROWS, hg_w), F32)],
        compiler_params=cparams,
        name="mixer",
    )(x, g_mix[0][None], w_in[0].astype(BF16), lam2_re_t, lam2_im_t, bk, ck, gk,
      s5_d[0][None], w_glu[0].astype(BF16), b_glu[0][None], lb, hg_norm_gain[0][None],
      w_pa[0].astype(BF16), w_pb[0].astype(BF16), w_out[0].astype(BF16), w_up[0], w_down[0])

    return pl.pallas_call(
        _ffn_kernel,
        grid=(bsz // SEQS, seq // FFN_TT),
        in_specs=[pl.BlockSpec((1, FFN_ROWS, d_model), lambda g, t: (g, t, 0)),
                  _const_spec((1, d_model)),
                  _const_spec((d_model, 2 * d_ff)),
                  _const_spec((CONV_W, 2 * d_ff)),
                  _const_spec((1, 2 * d_ff)),
                  _const_spec((d_ff, d_model)),
                  _const_spec((1, d_model))],
        out_specs=_seq_spec(d_model, FFN_TT),
        out_shape=jax.ShapeDtypeStruct((bsz, seq, d_model), F32),
        scratch_shapes=[pltpu.VMEM((2 * SEQS, 2 * d_ff), F32),
                        pltpu.VMEM((d_model // LANES, FFN_ROWS, LANES), F32),
                        pltpu.VMEM((FFN_ROWS, d_ff), BF16)],
        compiler_params=cparams,
        name="ffn",
    )(x1, g_ffn[0][None], w_up_b, w_conv[0], b_conv[0][None], w_down_b, g_final[None])
```

```python
import functools

import jax
import jax.numpy as jnp
from jax import lax
from jax.experimental import pallas as pl
from jax.experimental.pallas import tpu as pltpu

EPS = 1e-6
S5_GROUP = 16
S5_STATE = 64
HG_HEAD_DIM = 128
CONV_W = 3

LANES = 128
SEQS = 8
TT = 64
ROWS = SEQS * TT
GROUPS_PER_TILE = LANES // S5_GROUP
STATE_TILE = GROUPS_PER_TILE * S5_STATE
HG_FAST_DECAY_LIMIT = 60.0
CAST_ROWS_MIN = 16
FFN_TT = 128
FFN_ROWS = SEQS * FFN_TT
FFN_CHUNK = 256
FFN_OUT_SPLIT = 2
MIX_HEAD_CHUNKS = 4
V7X_VMEM_BYTES = 64 * 1024 * 1024
VMEM_LIMIT = V7X_VMEM_BYTES * 7 // 8

F32 = jnp.float32
BF16 = jnp.bfloat16


def _dot(a, b):
    return jnp.dot(a, b, preferred_element_type=F32)


def _rmsnorm(x, gain):
    return x * lax.rsqrt(jnp.mean(x * x, axis=-1, keepdims=True) + EPS) * gain


def _sigmoid(x):
    return jax.nn.sigmoid(x)


def _const_spec(shape):
    nd = len(shape)
    return pl.BlockSpec(shape, lambda *_: (0,) * nd, pipeline_mode=pl.Buffered(1))


def _seq_spec(width, tokens):
    return pl.BlockSpec((SEQS, tokens, width), lambda g, t: (g, t, 0))


def _interleave(seq_ref, slab_scr, t0, nt):
    n_slabs = slab_scr.shape[0]
    for b in range(SEQS):
        for s in range(n_slabs):
            slab_scr[s, pl.ds(t0 * SEQS + b, nt, stride=SEQS), :] = (
                seq_ref[b, t0:t0 + nt, s * LANES:(s + 1) * LANES])
    return jnp.concatenate([slab_scr[s, t0 * SEQS:(t0 + nt) * SEQS, :] for s in range(n_slabs)], axis=-1)


def _deinterleave(val, slab_scr, seq_ref, row0=0):
    n_slabs = slab_scr.shape[0]
    rows = val.shape[0]
    for s in range(n_slabs):
        slab_scr[s, row0:row0 + rows, :] = val[:, s * LANES:(s + 1) * LANES]
    for b in range(SEQS):
        seq_ref[b, row0 // SEQS:(row0 + rows) // SEQS, :] = jnp.concatenate(
            [slab_scr[s, pl.ds(row0 + b, rows // SEQS, stride=SEQS), :] for s in range(n_slabs)],
            axis=-1)


def _prep_kernel(a_re_ref, a_im_ref, log_dt_ref, bt_re_ref, bt_im_ref, c_re_ref, c_im_ref, lbl_ref,
                 lam2_re_ref, lam2_im_ref, bk_ref, gk_ref, cl_re_ref, cl_im_ref, lb_ref, *, layer):
    a_re = a_re_ref[...]
    a_im = a_im_ref[...]
    dt = jnp.exp(log_dt_ref[...])
    mag = jnp.exp(a_re * dt)
    ang = a_im * dt
    lb_re = mag * jnp.cos(ang)
    lb_im = mag * jnp.sin(ang)
    den = a_re * a_re + a_im * a_im
    n_re = lb_re - 1.0
    n_im = lb_im
    co_re = (n_re * a_re + n_im * a_im) / den
    co_im = (n_im * a_re - n_re * a_im) / den
    lam2_re = lb_re * lb_re - lb_im * lb_im
    lam2_im = 2.0 * lb_re * lb_im
    bt_re = bt_re_ref[...]
    bt_im = bt_im_ref[...]
    bbt_re = co_re[:, None, :] * bt_re - co_im[:, None, :] * bt_im
    bbt_im = co_re[:, None, :] * bt_im + co_im[:, None, :] * bt_re
    l_re = lb_re[:, None, :]
    l_im = lb_im[:, None, :]
    lbt_re = l_re * bbt_re - l_im * bbt_im
    lbt_im = l_re * bbt_im + l_im * bbt_re
    c_re = c_re_ref[...]
    c_im = c_im_ref[...]
    cl_re_ref[...] = c_re * l_re - c_im * l_im
    cl_im_ref[...] = c_re * l_im + c_im * l_re
    n_groups, n_h, n_p = bt_re.shape
    bk_ref[...] = jnp.zeros_like(bk_ref)
    gk_ref[...] = jnp.zeros_like(gk_ref)
    for g in range(n_groups):
        k, gl = divmod(g, GROUPS_PER_TILE)
        rows = slice(gl * n_h, (gl + 1) * n_h)
        for half, (m_re, m_im) in enumerate(((lbt_re, lbt_im), (bbt_re, bbt_im))):
            r = slice(half * LANES + gl * n_h, half * LANES + (gl + 1) * n_h)
            bk_ref[k, r, gl * n_p:(gl + 1) * n_p] = m_re[g]
            bk_ref[k, r, STATE_TILE + gl * n_p:STATE_TILE + (gl + 1) * n_p] = m_im[g]
        gk_ref[k, rows, rows] = jnp.sum(bbt_re[g][:, None, :] * c_re[g][None, :, :]
                                        - bbt_im[g][:, None, :] * c_im[g][None, :, :], axis=-1)
    for k in range(n_groups // GROUPS_PER_TILE):
        for ref, lam2 in ((lam2_re_ref, lam2_re), (lam2_im_ref, lam2_im)):
            row = jnp.concatenate([lam2[g:g + 1, :] for g in range(k * GROUPS_PER_TILE,
                                                                    (k + 1) * GROUPS_PER_TILE)], axis=-1)
            ref[k] = jnp.broadcast_to(row, (SEQS, STATE_TILE))
    logits = lbl_ref[...]
    m = jnp.max(logits, axis=0, keepdims=True)
    e = jnp.exp(logits - m)
    tot = jnp.sum(e, axis=0, keepdims=True)
    lb_ref[...] = jnp.sum(e[: layer + 1], axis=0, keepdims=True) / tot


def _s5_branch(za, lam2_re_ref, lam2_im_ref, bk_ref, ck_ref, gk_ref, d_ref, bu_scr, st_scr, ycar_scr):
    n_tiles = bk_ref.shape[0]
    width = za.shape[-1]
    pairs = TT // 2
    prows = pairs * SEQS
    za3 = za.reshape(pairs, 2 * SEQS, width)
    u_even = za3[:, :SEQS, :].reshape(prows, width)
    u_odd = za3[:, SEQS:, :].reshape(prows, width)
    ue_bf = u_even.astype(BF16)
    uo_bf = u_odd.astype(BF16)
    for k in range(n_tiles):
        lt = slice(k * LANES, (k + 1) * LANES)
        bu_scr[:, k * 2 * STATE_TILE:(k + 1) * 2 * STATE_TILE] = _dot(
            jnp.concatenate([ue_bf[:, lt], uo_bf[:, lt]], axis=-1), bk_ref[k])
    for k in range(n_tiles):
        lr = lam2_re_ref[k]
        li = lam2_im_ref[k]
        c_re = k * 2 * STATE_TILE
        c_im = c_re + STATE_TILE
        xr = st_scr[k, 0]
        xi = st_scr[k, 1]
        for p in range(pairs):
            r = slice(p * SEQS, (p + 1) * SEQS)
            nr = lr * xr - li * xi + bu_scr[r, c_re:c_re + STATE_TILE]
            ni = lr * xi + li * xr + bu_scr[r, c_im:c_im + STATE_TILE]
            bu_scr[r, c_re:c_re + STATE_TILE] = nr
            bu_scr[r, c_im:c_im + STATE_TILE] = ni
            xr, xi = nr, ni
        st_scr[k, 0] = xr
        st_scr[k, 1] = xi
    y_even, y_odd = [], []
    for k in range(n_tiles):
        lt = slice(k * LANES, (k + 1) * LANES)
        xk = bu_scr[:, k * 2 * STATE_TILE:(k + 1) * 2 * STATE_TILE].astype(BF16)
        yy = _dot(xk, ck_ref[k])
        y_odd.append(yy[:, :LANES])
        nxt = yy[:, LANES:]
        y_even.append(jnp.concatenate([ycar_scr[k], nxt[:prows - SEQS]], axis=0)
                      + _dot(ue_bf[:, lt], gk_ref[k]))
        ycar_scr[k] = nxt[prows - SEQS:]
    d = d_ref[...]
    y_even = jnp.concatenate(y_even, axis=-1) + d * u_even
    y_odd = jnp.concatenate(y_odd, axis=-1) + d * u_odd
    return jnp.concatenate([y_even.reshape(pairs, SEQS, width), y_odd.reshape(pairs, SEQS, width)],
                           axis=1).reshape(ROWS, width)


def _hgrn_branch(zq, zf, zi, zg, lb_ref, gain_ref, hst_scr):
    n_heads = hst_scr.shape[0]
    width = zq.shape[-1]
    lb = lb_ref[...]
    f = lb + (1.0 - lb) * _sigmoid(zf)
    lf = jnp.log(f)
    kk = 1.0 - f
    acc = lf[0:SEQS]
    parts = [acc]
    for t in range(1, TT):
        acc = acc + lf[t * SEQS:(t + 1) * SEQS]
        parts.append(acc)
    bcum = jnp.concatenate(parts, axis=0)
    b_last = parts[-1]
    qa = zq * _sigmoid(zq) * (HG_HEAD_DIM ** -0.5)
    qh = qa * jnp.exp(bcum)
    kh = (kk * jnp.exp(-bcum)).astype(BF16)
    ke = (kk.reshape(TT, SEQS, width) * jnp.exp(b_last[None] - bcum.reshape(TT, SEQS, width))
          ).reshape(ROWS, width)
    dec = jnp.exp(b_last)
    half = ROWS // 2
    dlt = (lax.broadcasted_iota(jnp.int32, (half, half), 1)
           - lax.broadcasted_iota(jnp.int32, (half, half), 0))
    same_seq = (dlt & (SEQS - 1)) == 0
    causal = same_seq & (dlt >= 0)
    seq_of_row = lax.broadcasted_iota(jnp.int32, (ROWS, HG_HEAD_DIM), 0) & (SEQS - 1)
    seq_of_col = lax.broadcasted_iota(jnp.int32, (HG_HEAD_DIM, ROWS), 1) & (SEQS - 1)
    intra, inter = [], []
    for h in range(n_heads):
        sl = slice(h * HG_HEAD_DIM, (h + 1) * HG_HEAD_DIM)
        q_t = qh[:, sl].T
        i_t = zi[:, sl].T.astype(BF16)
        q_tb = q_t.astype(BF16)
        k_e, k_l = kh[:half, sl], kh[half:, sl]
        sc_ee = jnp.where(causal, _dot(k_e, q_tb[:, :half]), 0.0).astype(BF16)
        sc_el = jnp.where(same_seq, _dot(k_e, q_tb[:, half:]), 0.0).astype(BF16)
        sc_ll = jnp.where(causal, _dot(k_l, q_tb[:, half:]), 0.0).astype(BF16)
        intra_t = jnp.concatenate(
            [_dot(i_t[:, :half], sc_ee),
             _dot(i_t, jnp.concatenate([sc_el, sc_ll], axis=0))], axis=1)
        q_aug_t = jnp.concatenate([jnp.where(seq_of_col == b, q_t, 0.0) for b in range(SEQS)],
                                  axis=0).astype(BF16)
        k_aug = jnp.concatenate([jnp.where(seq_of_row == b, ke[:, sl], 0.0) for b in range(SEQS)],
                                axis=-1).astype(BF16)
        st = hst_scr[h]
        inter_t = _dot(st.astype(BF16), q_aug_t)
        dec_row = jnp.concatenate([dec[b:b + 1, sl] for b in range(SEQS)], axis=-1)
        hst_scr[h] = st * dec_row + _dot(i_t, k_aug)
        intra.append(intra_t.T)
        inter.append(inter_t.T)
    o_inter = jnp.concatenate(inter, axis=-1)
    out_gate = zg * _sigmoid(zg)
    yb = _hgrn_head_norm(jnp.concatenate(intra, axis=-1) + o_inter, gain_ref) * out_gate
    return yb, (qa, kk, zi, bcum, b_last, o_inter, out_gate)


def _hgrn_head_norm(o, gain_ref):
    gain = gain_ref[...]
    heads = [slice(h * HG_HEAD_DIM, (h + 1) * HG_HEAD_DIM) for h in range(o.shape[-1] // HG_HEAD_DIM)]
    return jnp.concatenate([_rmsnorm(o[:, sl], gain[:, sl]) for sl in heads], axis=-1)


def _hgrn_intra_exact(qa, kk, zi, bcum, fb_scr, oi_scr):
    width = qa.shape[-1]
    n_heads = width // HG_HEAD_DIM
    fb_scr[0] = qa
    fb_scr[1] = kk
    fb_scr[2] = zi
    fb_scr[3] = bcum

    def rows(j, t):
        return fb_scr[j, pl.ds(pl.multiple_of(t * SEQS, SEQS), SEQS), :]

    def outer(t, carry):
        q_t = rows(0, t)
        b_t = rows(3, t)

        def inner(s, acc):
            decay = jnp.where(s <= t, jnp.exp(jnp.minimum(b_t - rows(3, s), 0.0)), 0.0)
            p = q_t * rows(1, s) * decay
            i_s = rows(2, s)
            return acc + jnp.concatenate(
                [jnp.sum(p[:, h * HG_HEAD_DIM:(h + 1) * HG_HEAD_DIM], axis=-1, keepdims=True)
                 * i_s[:, h * HG_HEAD_DIM:(h + 1) * HG_HEAD_DIM] for h in range(n_heads)], axis=-1)

        acc = lax.fori_loop(0, TT, inner, jnp.zeros((SEQS, width), F32))
        oi_scr[pl.ds(pl.multiple_of(t * SEQS, SEQS), SEQS), :] = acc
        return carry

    lax.fori_loop(0, TT, outer, 0)


def _mixer_kernel(x_ref, gmix_ref, win_ref, lam2_re_ref, lam2_im_ref, bk_ref, ck_ref, gk_ref, d_ref,
                  wglu_ref, bglu_ref, lb_ref, gain_ref, wpa_ref, wpb_ref, wout_ref, wup_ref, wdown_ref,
                  out_ref, wup_bf_ref, wdown_bf_ref,
                  slab_scr, bu_scr, sst_scr, ycar_scr, hst_scr, oi_scr, fb_scr, *, widths):
    s5_w, hg_w, d_model = widths
    wup_bf_ref[...] = wup_ref[...].astype(BF16)
    wdown_bf_ref[...] = wdown_ref[...].astype(BF16)

    @pl.when(pl.program_id(1) == 0)
    def _():
        sst_scr[...] = jnp.zeros_like(sst_scr)
        ycar_scr[...] = jnp.zeros_like(ycar_scr)
        hst_scr[...] = jnp.zeros_like(hst_scr)

    nt = TT // MIX_HEAD_CHUNKS
    o = s5_w
    us = [_rmsnorm(_interleave(x_ref, slab_scr, c * nt, nt), gmix_ref[...]).astype(BF16)
          for c in range(MIX_HEAD_CHUNKS)]
    zf = jnp.concatenate([_dot(uc, win_ref[:, o + hg_w:o + 2 * hg_w]) for uc in us], axis=0)
    zq = jnp.concatenate([_dot(uc, win_ref[:, o:o + hg_w]) for uc in us], axis=0)
    u = jnp.concatenate(us, axis=0)

    def proj(lo, width):
        return _dot(u, win_ref[:, lo:lo + width])

    yb, (qa, kk, zi, bcum, b_last, o_inter, out_gate) = _hgrn_branch(
        zq, zf, proj(o + 2 * hg_w, hg_w), proj(o + 3 * hg_w, hg_w),
        lb_ref, gain_ref, hst_scr)
    y_s5 = _s5_branch(proj(0, s5_w), lam2_re_ref, lam2_im_ref, bk_ref, ck_ref, gk_ref, d_ref, bu_scr,
                      sst_scr, ycar_scr)
    o += 4 * hg_w
    gate_a = _sigmoid(proj(o, d_model))
    gate_b = _sigmoid(proj(o + d_model, d_model))

    ya = jax.nn.gelu(y_s5)
    ya = ya * _sigmoid(_dot(ya.astype(BF16), wglu_ref[...]) + bglu_ref[...])
    m_a = gate_a * _dot(ya.astype(BF16), wpa_ref[...])

    def finish(yb):
        m = m_a + gate_b * _dot(yb.astype(BF16), wpb_ref[...])
        x_res = jnp.concatenate([slab_scr[s] for s in range(slab_scr.shape[0])], axis=-1)
        out_ref[0] = x_res + _dot(m.astype(BF16), wout_ref[...])

    finish(yb)

    @pl.when(jnp.min(b_last) < -HG_FAST_DECAY_LIMIT)
    def _():
        _hgrn_intra_exact(qa, kk, zi, bcum, fb_scr, oi_scr)
        finish(_hgrn_head_norm(oi_scr[...] + o_inter, gain_ref) * out_gate)


def _conv_taps(h, prev, w, b):
    rows = h.shape[0]
    h1 = jnp.concatenate([prev[SEQS:], h[:rows - SEQS]], axis=0)
    h2 = jnp.concatenate([prev, h[:rows - 2 * SEQS]], axis=0)
    return h2 * w[0:1] + h1 * w[1:2] + h * w[2:3] + b


def _ffn_kernel(x_ref, gf_ref, wup_ref, wconv_ref, bconv_ref, wdown_ref, gfin_ref, out_ref,
                halo_scr, slab_scr, act_scr):
    d_ff = wdown_ref.shape[0]
    n_chunks = d_ff // FFN_CHUNK

    @pl.when(pl.program_id(1) == 0)
    def _():
        halo_scr[...] = jnp.zeros_like(halo_scr)

    x1 = x_ref[0]
    u = _rmsnorm(x1, gf_ref[...]).astype(BF16)
    for c in range(n_chunks):
        cols = []
        for part in range(2):
            j = part * n_chunks + c
            cs = slice(j * FFN_CHUNK, (j + 1) * FFN_CHUNK)
            h = _dot(u, wup_ref[:, cs])
            cols.append(_conv_taps(h, halo_scr[:, cs], wconv_ref[:, cs], bconv_ref[:, cs]))
            halo_scr[:, cs] = h[FFN_ROWS - 2 * SEQS:]
        gate, val = cols
        act_scr[:, c * FFN_CHUNK:(c + 1) * FFN_CHUNK] = (gate * _sigmoid(gate) * val).astype(BF16)
    blk = FFN_ROWS // FFN_OUT_SPLIT
    for i in range(FFN_OUT_SPLIT):
        rs = slice(i * blk, (i + 1) * blk)
        y = x1[rs] + _dot(act_scr[rs, :], wdown_ref[...])
        _deinterleave(_rmsnorm(y, gfin_ref[...]), slab_scr, out_ref, i * blk)


def _block_diag(vals):
    nt, g, a, b = vals.shape
    same = jnp.arange(g)[:, None, None, None] == jnp.arange(g)[None, None, :, None]
    return jnp.where(same, vals[:, :, :, None, :], 0).reshape(nt, g * a, g * b)


def kernel(x, g_mix, w_in, s5_a_re, s5_a_im, s5_log_dt, s5_b_re, s5_b_im, s5_c_re, s5_c_im, s5_d,
           w_glu, b_glu, hg_lb_logits, hg_norm_gain, w_pa, w_pb, w_out, g_ffn, w_up, w_conv,
           b_conv, w_down, g_final):
    depth = w_in.shape[0]
    assert depth == 1, "kernel is written for a single layer"
    bsz, seq, d_model = x.shape
    s5_w = s5_d.shape[-1]
    hg_w = hg_norm_gain.shape[-1]
    d_ff = w_down.shape[1]
    n_in = w_in.shape[-1]
    n_groups = s5_a_re.shape[1]
    n_tiles = s5_w // LANES
    n_heads = hg_w // HG_HEAD_DIM
    assert bsz % SEQS == 0 and seq % TT == 0 and seq % FFN_TT == 0
    assert s5_w % LANES == 0 and d_model % LANES == 0 and d_ff % FFN_CHUNK == 0
    assert hg_w % HG_HEAD_DIM == 0 and n_in == s5_w + 4 * hg_w + 2 * d_model
    grid = (bsz // SEQS, seq // TT)
    cparams = pltpu.CompilerParams(vmem_limit_bytes=VMEM_LIMIT,
                                   dimension_semantics=("arbitrary", "arbitrary"))

    ghps = jax.ShapeDtypeStruct((n_groups, S5_GROUP, S5_STATE), F32)
    lam_t = jax.ShapeDtypeStruct((n_tiles, SEQS, STATE_TILE), F32)
    lam2_re_t, lam2_im_t, bk, gk, cl_re, cl_im, lb = pl.pallas_call(
        functools.partial(_prep_kernel, layer=0),
        out_shape=[lam_t, lam_t,
                   jax.ShapeDtypeStruct((n_tiles, 2 * LANES, 2 * STATE_TILE), F32),
                   jax.ShapeDtypeStruct((n_tiles, LANES, LANES), F32),
                   ghps, ghps, jax.ShapeDtypeStruct((1, hg_w), F32)],
        name="prep",
    )(s5_a_re[0], s5_a_im[0], s5_log_dt[0][:, None],
      jnp.swapaxes(s5_b_re[0], 1, 2), jnp.swapaxes(s5_b_im[0], 1, 2), s5_c_re[0], s5_c_im[0],
      hg_lb_logits)
    bk = bk.astype(BF16)
    gk = gk.astype(BF16)

    def bd_t(v):
        v = jnp.swapaxes(v, 1, 2)
        return _block_diag(v.reshape((n_tiles, GROUPS_PER_TILE) + v.shape[1:]))
    ck = jnp.concatenate([jnp.concatenate([bd_t(s5_c_re[0]), bd_t(-s5_c_im[0])], axis=1),
                          jnp.concatenate([bd_t(cl_re), bd_t(-cl_im)], axis=1)], axis=-1).astype(BF16)

    n_steps = grid[0] * grid[1]

    def row_slices(n_rows, n_cols):
        rows = CAST_ROWS_MIN * -(-n_rows // (n_steps * CAST_ROWS_MIN))
        while n_rows % rows:
            rows += CAST_ROWS_MIN
        return pl.BlockSpec((rows, n_cols),
                            lambda g, t: (jnp.minimum(g * grid[1] + t, n_rows // rows - 1), 0))
    up_slices = row_slices(d_model, 2 * d_ff)
    down_slices = row_slices(d_ff, d_model)
    x1, w_up_b, w_down_b = pl.pallas_call(
        functools.partial(_mixer_kernel, widths=(s5_w, hg_w, d_model)),
        grid=grid,
        in_specs=[
            _seq_spec(d_model, TT),
            _const_spec((1, d_model)),
            _const_spec((d_model, n_in)),
            _const_spec((n_tiles, SEQS, STATE_TILE)),
            _const_spec((n_tiles, SEQS, STATE_TILE)),
            _const_spec((n_tiles, 2 * LANES, 2 * STATE_TILE)),
            _const_spec((n_tiles, 2 * STATE_TILE, 2 * LANES)),
            _const_spec((n_tiles, LANES, LANES)),
            _const_spec((1, s5_w)),
            _const_spec((s5_w, s5_w)),
            _const_spec((1, s5_w)),
            _const_spec((1, hg_w)),
            _const_spec((1, hg_w)),
            _const_spec((s5_w, d_model)),
            _const_spec((hg_w, d_model)),
            _const_spec((d_model, d_model)),
            up_slices, down_slices],
        out_specs=[pl.BlockSpec((1, ROWS, d_model), lambda g, t: (g, t, 0)), up_slices, down_slices],
        out_shape=[jax.ShapeDtypeStruct((bsz // SEQS, seq * SEQS, d_model), F32),
                   jax.ShapeDtypeStruct((d_model, 2 * d_ff), BF16),
                   jax.ShapeDtypeStruct((d_ff, d_model), BF16)],
        scratch_shapes=[pltpu.VMEM((d_model // LANES, ROWS, LANES), F32),
                        pltpu.VMEM((ROWS // 2, n_tiles * 2 * STATE_TILE), F32),
                        pltpu.VMEM((n_tiles, 2, SEQS, STATE_TILE), F32),
                        pltpu.VMEM((n_tiles, SEQS, LANES), F32),
                        pltpu.VMEM((n_heads, HG_HEAD_DIM, SEQS * HG_HEAD_DIM), F32),
                        pltpu.VMEM((ROWS, hg_w), F32),
                        pltpu.VMEM((4, ROWS, hg_w), F32)],
        compiler_params=cparams,
        name="mixer",
    )(x, g_mix[0][None], w_in[0].astype(BF16), lam2_re_t, lam2_im_t, bk, ck, gk,
      s5_d[0][None], w_glu[0].astype(BF16), b_glu[0][None], lb, hg_norm_gain[0][None],
      w_pa[0].astype(BF16), w_pb[0].astype(BF16), w_out[0].astype(BF16), w_up[0], w_down[0])

    return pl.pallas_call(
        _ffn_kernel,
        grid=(bsz // SEQS, seq // FFN_TT),
        in_specs=[pl.BlockSpec((1, FFN_ROWS, d_model), lambda g, t: (g, t, 0)),
                  _const_spec((1, d_model)),
                  _const_spec((d_model, 2 * d_ff)),
                  _const_spec((CONV_W, 2 * d_ff)),
                  _const_spec((1, 2 * d_ff)),
                  _const_spec((d_ff, d_model)),
                  _const_spec((1, d_model))],
        out_specs=_seq_spec(d_model, FFN_TT),
        out_shape=jax.ShapeDtypeStruct((bsz, seq, d_model), F32),
        scratch_shapes=[pltpu.VMEM((2 * SEQS, 2 * d_ff), F32),
                        pltpu.VMEM((d_model // LANES, FFN_ROWS, LANES), F32),
                        pltpu.VMEM((FFN_ROWS, d_ff), BF16)],
        compiler_params=cparams,
        name="ffn",
    )(x1, g_ffn[0][None], w_up_b, w_conv[0], b_conv[0][None], w_down_b, g_final[None])
```

```python
import functools

import jax
import jax.numpy as jnp
from jax import lax
from jax.experimental import pallas as pl
from jax.experimental.pallas import tpu as pltpu

EPS = 1e-6
S5_GROUP = 16
S5_STATE = 64
HG_HEAD_DIM = 128
CONV_W = 3

LANES = 128
SEQS = 8
TT = 64
ROWS = SEQS * TT
GROUPS_PER_TILE = LANES // S5_GROUP
STATE_TILE = GROUPS_PER_TILE * S5_STATE
HG_FAST_DECAY_LIMIT = 60.0
CAST_ROWS_MIN = 16
FFN_TT = 128
FFN_ROWS = SEQS * FFN_TT
FFN_CHUNK = 256
FFN_OUT_SPLIT = 2
MIX_HEAD_CHUNKS = 4
V7X_VMEM_BYTES = 64 * 1024 * 1024
VMEM_LIMIT = V7X_VMEM_BYTES * 7 // 8

F32 = jnp.float32
BF16 = jnp.bfloat16


def _dot(a, b):
    return jnp.dot(a, b, preferred_element_type=F32)


def _rmsnorm(x, gain):
    return x * lax.rsqrt(jnp.mean(x * x, axis=-1, keepdims=True) + EPS) * gain


def _sigmoid(x):
    return jax.nn.sigmoid(x)


def _const_spec(shape):
    nd = len(shape)
    return pl.BlockSpec(shape, lambda *_: (0,) * nd, pipeline_mode=pl.Buffered(1))


def _seq_spec(width, tokens):
    return pl.BlockSpec((SEQS, tokens, width), lambda g, t: (g, t, 0))


def _interleave(seq_ref, slab_scr, t0, nt):
    n_slabs = slab_scr.shape[0]
    for b in range(SEQS):
        for s in range(n_slabs):
            slab_scr[s, pl.ds(t0 * SEQS + b, nt, stride=SEQS), :] = (
                seq_ref[b, t0:t0 + nt, s * LANES:(s + 1) * LANES])
    return jnp.concatenate([slab_scr[s, t0 * SEQS:(t0 + nt) * SEQS, :] for s in range(n_slabs)], axis=-1)


def _deinterleave(val, slab_scr, seq_ref, row0=0):
    n_slabs = slab_scr.shape[0]
    rows = val.shape[0]
    for s in range(n_slabs):
        slab_scr[s, row0:row0 + rows, :] = val[:, s * LANES:(s + 1) * LANES]
    for b in range(SEQS):
        seq_ref[b, row0 // SEQS:(row0 + rows) // SEQS, :] = jnp.concatenate(
            [slab_scr[s, pl.ds(row0 + b, rows // SEQS, stride=SEQS), :] for s in range(n_slabs)],
            axis=-1)


def _prep_kernel(a_re_ref, a_im_ref, log_dt_ref, bt_re_ref, bt_im_ref, c_re_ref, c_im_ref, lbl_ref,
                 lam2_re_ref, lam2_im_ref, bk_ref, gk_ref, cl_re_ref, cl_im_ref, lb_ref, *, layer):
    a_re = a_re_ref[...]
    a_im = a_im_ref[...]
    dt = jnp.exp(log_dt_ref[...])
    mag = jnp.exp(a_re * dt)
    ang = a_im * dt
    lb_re = mag * jnp.cos(ang)
    lb_im = mag * jnp.sin(ang)
    den = a_re * a_re + a_im * a_im
    n_re = lb_re - 1.0
    n_im = lb_im
    co_re = (n_re * a_re + n_im * a_im) / den
    co_im = (n_im * a_re - n_re * a_im) / den
    lam2_re = lb_re * lb_re - lb_im * lb_im
    lam2_im = 2.0 * lb_re * lb_im
    bt_re = bt_re_ref[...]
    bt_im = bt_im_ref[...]
    bbt_re = co_re[:, None, :] * bt_re - co_im[:, None, :] * bt_im
    bbt_im = co_re[:, None, :] * bt_im + co_im[:, None, :] * bt_re
    l_re = lb_re[:, None, :]
    l_im = lb_im[:, None, :]
    lbt_re = l_re * bbt_re - l_im * bbt_im
    lbt_im = l_re * bbt_im + l_im * bbt_re
    c_re = c_re_ref[...]
    c_im = c_im_ref[...]
    cl_re_ref[...] = c_re * l_re - c_im * l_im
    cl_im_ref[...] = c_re * l_im + c_im * l_re
    n_groups, n_h, n_p = bt_re.shape
    bk_ref[...] = jnp.zeros_like(bk_ref)
    gk_ref[...] = jnp.zeros_like(gk_ref)
    for g in range(n_groups):
        k, gl = divmod(g, GROUPS_PER_TILE)
        rows = slice(gl * n_h, (gl + 1) * n_h)
        for half, (m_re, m_im) in enumerate(((lbt_re, lbt_im), (bbt_re, bbt_im))):
            r = slice(half * LANES + gl * n_h, half * LANES + (gl + 1) * n_h)
            bk_ref[k, r, gl * n_p:(gl + 1) * n_p] = m_re[g]
            bk_ref[k, r, STATE_TILE + gl * n_p:STATE_TILE + (gl + 1) * n_p] = m_im[g]
        gk_ref[k, rows, rows] = jnp.sum(bbt_re[g][:, None, :] * c_re[g][None, :, :]
                                        - bbt_im[g][:, None, :] * c_im[g][None, :, :], axis=-1)
    for k in range(n_groups // GROUPS_PER_TILE):
        for ref, lam2 in ((lam2_re_ref, lam2_re), (lam2_im_ref, lam2_im)):
            row = jnp.concatenate([lam2[g:g + 1, :] for g in range(k * GROUPS_PER_TILE,
                                                                    (k + 1) * GROUPS_PER_TILE)], axis=-1)
            ref[k] = jnp.broadcast_to(row, (SEQS, STATE_TILE))
    logits = lbl_ref[...]
    m = jnp.max(logits, axis=0, keepdims=True)
    e = jnp.exp(logits - m)
    tot = jnp.sum(e, axis=0, keepdims=True)
    lb_ref[...] = jnp.sum(e[: layer + 1], axis=0, keepdims=True) / tot


def _s5_branch(za, lam2_re_ref, lam2_im_ref, bk_ref, ck_ref, gk_ref, d_ref, bu_scr, st_scr, ycar_scr):
    n_tiles = bk_ref.shape[0]
    width = za.shape[-1]
    pairs = TT // 2
    prows = pairs * SEQS
    za3 = za.reshape(pairs, 2 * SEQS, width)
    u_even = za3[:, :SEQS, :].reshape(prows, width)
    u_odd = za3[:, SEQS:, :].reshape(prows, width)
    ue_bf = u_even.astype(BF16)
    uo_bf = u_odd.astype(BF16)
    for k in range(n_tiles):
        lt = slice(k * LANES, (k + 1) * LANES)
        bu_scr[:, k * 2 * STATE_TILE:(k + 1) * 2 * STATE_TILE] = _dot(
            jnp.concatenate([ue_bf[:, lt], uo_bf[:, lt]], axis=-1), bk_ref[k])
    for k in range(n_tiles):
        lr = lam2_re_ref[k]
        li = lam2_im_ref[k]
        c_re = k * 2 * STATE_TILE
        c_im = c_re + STATE_TILE
        xr = st_scr[k, 0]
        xi = st_scr[k, 1]
        for p in range(pairs):
            r = slice(p * SEQS, (p + 1) * SEQS)
            nr = lr * xr - li * xi + bu_scr[r, c_re:c_re + STATE_TILE]
            ni = lr * xi + li * xr + bu_scr[r, c_im:c_im + STATE_TILE]
            bu_scr[r, c_re:c_re + STATE_TILE] = nr
            bu_scr[r, c_im:c_im + STATE_TILE] = ni
            xr, xi = nr, ni
        st_scr[k, 0] = xr
        st_scr[k, 1] = xi
    y_even, y_odd = [], []
    for k in range(n_tiles):
        lt = slice(k * LANES, (k + 1) * LANES)
        xk = bu_scr[:, k * 2 * STATE_TILE:(k + 1) * 2 * STATE_TILE].astype(BF16)
        yy = _dot(xk, ck_ref[k])
        y_odd.append(yy[:, :LANES])
        nxt = yy[:, LANES:]
        y_even.append(jnp.concatenate([ycar_scr[k], nxt[:prows - SEQS]], axis=0)
                      + _dot(ue_bf[:, lt], gk_ref[k]))
        ycar_scr[k] = nxt[prows - SEQS:]
    d = d_ref[...]
    y_even = jnp.concatenate(y_even, axis=-1) + d * u_even
    y_odd = jnp.concatenate(y_odd, axis=-1) + d * u_odd
    return jnp.concatenate([y_even.reshape(pairs, SEQS, width), y_odd.reshape(pairs, SEQS, width)],
                           axis=1).reshape(ROWS, width)


def _hgrn_branch(zq, zf, zi, zg, lb_ref, gain_ref, hst_scr):
    n_heads = hst_scr.shape[0]
    width = zq.shape[-1]
    lb = lb_ref[...]
    f = lb + (1.0 - lb) * _sigmoid(zf)
    lf = jnp.log(f)
    kk = 1.0 - f
    acc = lf[0:SEQS]
    parts = [acc]
    for t in range(1, TT):
        acc = acc + lf[t * SEQS:(t + 1) * SEQS]
        parts.append(acc)
    bcum = jnp.concatenate(parts, axis=0)
    b_last = parts[-1]
    qa = zq * _sigmoid(zq) * (HG_HEAD_DIM ** -0.5)
    qh = qa * jnp.exp(bcum)
    kh = (kk * jnp.exp(-bcum)).astype(BF16)
    ke = (kk.reshape(TT, SEQS, width) * jnp.exp(b_last[None] - bcum.reshape(TT, SEQS, width))
          ).reshape(ROWS, width)
    dec = jnp.exp(b_last)
    half = ROWS // 2
    dlt = (lax.broadcasted_iota(jnp.int32, (half, half), 1)
           - lax.broadcasted_iota(jnp.int32, (half, half), 0))
    same_seq = (dlt & (SEQS - 1)) == 0
    causal = same_seq & (dlt >= 0)
    seq_of_row = lax.broadcasted_iota(jnp.int32, (ROWS, HG_HEAD_DIM), 0) & (SEQS - 1)
    seq_of_col = lax.broadcasted_iota(jnp.int32, (HG_HEAD_DIM, ROWS), 1) & (SEQS - 1)
    intra, inter = [], []
    for h in range(n_heads):
        sl = slice(h * HG_HEAD_DIM, (h + 1) * HG_HEAD_DIM)
        q_t = qh[:, sl].T
        i_t = zi[:, sl].T.astype(BF16)
        q_tb = q_t.astype(BF16)
        k_e, k_l = kh[:half, sl], kh[half:, sl]
        sc_ee = jnp.where(causal, _dot(k_e, q_tb[:, :half]), 0.0).astype(BF16)
        sc_el = jnp.where(same_seq, _dot(k_e, q_tb[:, half:]), 0.0).astype(BF16)
        sc_ll = jnp.where(causal, _dot(k_l, q_tb[:, half:]), 0.0).astype(BF16)
        intra_t = jnp.concatenate(
            [_dot(i_t[:, :half], sc_ee),
             _dot(i_t, jnp.concatenate([sc_el, sc_ll], axis=0))], axis=1)
        q_aug_t = jnp.concatenate([jnp.where(seq_of_col == b, q_t, 0.0) for b in range(SEQS)],
                                  axis=0).astype(BF16)
        k_aug = jnp.concatenate([jnp.where(seq_of_row == b, ke[:, sl], 0.0) for b in range(SEQS)],
                                axis=-1).astype(BF16)
        st = hst_scr[h]
        inter_t = _dot(st.astype(BF16), q_aug_t)
        dec_row = jnp.concatenate([dec[b:b + 1, sl] for b in range(SEQS)], axis=-1)
        hst_scr[h] = st * dec_row + _dot(i_t, k_aug)
        intra.append(intra_t.T)
        inter.append(inter_t.T)
    o_inter = jnp.concatenate(inter, axis=-1)
    out_gate = zg * _sigmoid(zg)
    yb = _hgrn_head_norm(jnp.concatenate(intra, axis=-1) + o_inter, gain_ref) * out_gate
    return yb, (qa, kk, zi, bcum, b_last, o_inter, out_gate)


def _hgrn_head_norm(o, gain_ref):
    gain = gain_ref[...]
    heads = [slice(h * HG_HEAD_DIM, (h + 1) * HG_HEAD_DIM) for h in range(o.shape[-1] // HG_HEAD_DIM)]
    return jnp.concatenate([_rmsnorm(o[:, sl], gain[:, sl]) for sl in heads], axis=-1)


def _hgrn_intra_exact(qa, kk, zi, bcum, fb_scr, oi_scr):
    width = qa.shape[-1]
    n_heads = width // HG_HEAD_DIM
    fb_scr[0] = qa
    fb_scr[1] = kk
    fb_scr[2] = zi
    fb_scr[3] = bcum

    def rows(j, t):
        return fb_scr[j, pl.ds(pl.multiple_of(t * SEQS, SEQS), SEQS), :]

    def outer(t, carry):
        q_t = rows(0, t)
        b_t = rows(3, t)

        def inner(s, acc):
            decay = jnp.where(s <= t, jnp.exp(jnp.minimum(b_t - rows(3, s), 0.0)), 0.0)
            p = q_t * rows(1, s) * decay
            i_s = rows(2, s)
            return acc + jnp.concatenate(
                [jnp.sum(p[:, h * HG_HEAD_DIM:(h + 1) * HG_HEAD_DIM], axis=-1, keepdims=True)
                 * i_s[:, h * HG_HEAD_DIM:(h + 1) * HG_HEAD_DIM] for h in range(n_heads)], axis=-1)

        acc = lax.fori_loop(0, TT, inner, jnp.zeros((SEQS, width), F32))
        oi_scr[pl.ds(pl.multiple_of(t * SEQS, SEQS), SEQS), :] = acc
        return carry

    lax.fori_loop(0, TT, outer, 0)


def _mixer_kernel(x_ref, gmix_ref, win_ref, lam2_re_ref, lam2_im_ref, bk_ref, ck_ref, gk_ref, d_ref,
                  wglu_ref, bglu_ref, lb_ref, gain_ref, wpa_ref, wpb_ref, wout_ref, wup_ref, wdown_ref,
                  out_ref, wup_bf_ref, wdown_bf_ref,
                  slab_scr, bu_scr, sst_scr, ycar_scr, hst_scr, oi_scr, fb_scr, *, widths):
    s5_w, hg_w, d_model = widths
    wup_bf_ref[...] = wup_ref[...].astype(BF16)
    wdown_bf_ref[...] = wdown_ref[...].astype(BF16)

    @pl.when(pl.program_id(1) == 0)
    def _():
        sst_scr[...] = jnp.zeros_like(sst_scr)
        ycar_scr[...] = jnp.zeros_like(ycar_scr)
        hst_scr[...] = jnp.zeros_like(hst_scr)

    nt = TT // MIX_HEAD_CHUNKS
    o = s5_w
    us = [_rmsnorm(_interleave(x_ref, slab_scr, c * nt, nt), gmix_ref[...]).astype(BF16)
          for c in range(MIX_HEAD_CHUNKS)]
    zf = jnp.concatenate([_dot(uc, win_ref[:, o + hg_w:o + 2 * hg_w]) for uc in us], axis=0)
    zq = jnp.concatenate([_dot(uc, win_ref[:, o:o + hg_w]) for uc in us], axis=0)
    u = jnp.concatenate(us, axis=0)

    def proj(lo, width):
        return _dot(u, win_ref[:, lo:lo + width])

    yb, (qa, kk, zi, bcum, b_last, o_inter, out_gate) = _hgrn_branch(
        zq, zf, proj(o + 2 * hg_w, hg_w), proj(o + 3 * hg_w, hg_w),
        lb_ref, gain_ref, hst_scr)
    y_s5 = _s5_branch(proj(0, s5_w), lam2_re_ref, lam2_im_ref, bk_ref, ck_ref, gk_ref, d_ref, bu_scr,
                      sst_scr, ycar_scr)
    o += 4 * hg_w
    gate_a = _sigmoid(proj(o, d_model))
    gate_b = _sigmoid(proj(o + d_model, d_model))

    ya = jax.nn.gelu(y_s5)
    ya = ya * _sigmoid(_dot(ya.astype(BF16), wglu_ref[...]) + bglu_ref[...])
    m_a = gate_a * _dot(ya.astype(BF16), wpa_ref[...])

    def finish(yb):
        m = m_a + gate_b * _dot(yb.astype(BF16), wpb_ref[...])
        x_res = jnp.concatenate([slab_scr[s] for s in range(slab_scr.shape[0])], axis=-1)
        out_ref[0] = (x_res + _dot(m.astype(BF16), wout_ref[...])).astype(out_ref.dtype)

    finish(yb)

    @pl.when(jnp.min(b_last) < -HG_FAST_DECAY_LIMIT)
    def _():
        _hgrn_intra_exact(qa, kk, zi, bcum, fb_scr, oi_scr)
        finish(_hgrn_head_norm(oi_scr[...] + o_inter, gain_ref) * out_gate)


def _conv_taps(h, prev, w, b):
    rows = h.shape[0]
    h1 = jnp.concatenate([prev[SEQS:], h[:rows - SEQS]], axis=0)
    h2 = jnp.concatenate([prev, h[:rows - 2 * SEQS]], axis=0)
    return h2 * w[0:1] + h1 * w[1:2] + h * w[2:3] + b


def _ffn_kernel(x_ref, gf_ref, wup_ref, wconv_ref, bconv_ref, wdown_ref, gfin_ref, out_ref,
                halo_scr, slab_scr, act_scr):
    d_ff = wdown_ref.shape[0]
    n_chunks = d_ff // FFN_CHUNK

    @pl.when(pl.program_id(1) == 0)
    def _():
        halo_scr[...] = jnp.zeros_like(halo_scr)

    x1 = x_ref[0].astype(F32)
    u = _rmsnorm(x1, gf_ref[...]).astype(BF16)
    for c in range(n_chunks):
        cols = []
        for part in range(2):
            j = part * n_chunks + c
            cs = slice(j * FFN_CHUNK, (j + 1) * FFN_CHUNK)
            h = _dot(u, wup_ref[:, cs])
            cols.append(_conv_taps(h, halo_scr[:, cs], wconv_ref[:, cs], bconv_ref[:, cs]))
            halo_scr[:, cs] = h[FFN_ROWS - 2 * SEQS:]
        gate, val = cols
        act_scr[:, c * FFN_CHUNK:(c + 1) * FFN_CHUNK] = (gate * _sigmoid(gate) * val).astype(BF16)
    blk = FFN_ROWS // FFN_OUT_SPLIT
    for i in range(FFN_OUT_SPLIT):
        rs = slice(i * blk, (i + 1) * blk)
        y = x1[rs] + _dot(act_scr[rs, :], wdown_ref[...])
        _deinterleave(_rmsnorm(y, gfin_ref[...]), slab_scr, out_ref, i * blk)


def _block_diag(vals):
    nt, g, a, b = vals.shape
    same = jnp.arange(g)[:, None, None, None] == jnp.arange(g)[None, None, :, None]
    return jnp.where(same, vals[:, :, :, None, :], 0).reshape(nt, g * a, g * b)


def kernel(x, g_mix, w_in, s5_a_re, s5_a_im, s5_log_dt, s5_b_re, s5_b_im, s5_c_re, s5_c_im, s5_d,
           w_glu, b_glu, hg_lb_logits, hg_norm_gain, w_pa, w_pb, w_out, g_ffn, w_up, w_conv,
           b_conv, w_down, g_final):
    depth = w_in.shape[0]
    assert depth == 1, "kernel is written for a single layer"
    bsz, seq, d_model = x.shape
    s5_w = s5_d.shape[-1]
    hg_w = hg_norm_gain.shape[-1]
    d_ff = w_down.shape[1]
    n_in = w_in.shape[-1]
    n_groups = s5_a_re.shape[1]
    n_tiles = s5_w // LANES
    n_heads = hg_w // HG_HEAD_DIM
    assert bsz % SEQS == 0 and seq % TT == 0 and seq % FFN_TT == 0
    assert s5_w % LANES == 0 and d_model % LANES == 0 and d_ff % FFN_CHUNK == 0
    assert hg_w % HG_HEAD_DIM == 0 and n_in == s5_w + 4 * hg_w + 2 * d_model
    grid = (bsz // SEQS, seq // TT)
    cparams = pltpu.CompilerParams(vmem_limit_bytes=VMEM_LIMIT,
                                   dimension_semantics=("arbitrary", "arbitrary"))

    ghps = jax.ShapeDtypeStruct((n_groups, S5_GROUP, S5_STATE), F32)
    lam_t = jax.ShapeDtypeStruct((n_tiles, SEQS, STATE_TILE), F32)
    lam2_re_t, lam2_im_t, bk, gk, cl_re, cl_im, lb = pl.pallas_call(
        functools.partial(_prep_kernel, layer=0),
        out_shape=[lam_t, lam_t,
                   jax.ShapeDtypeStruct((n_tiles, 2 * LANES, 2 * STATE_TILE), F32),
                   jax.ShapeDtypeStruct((n_tiles, LANES, LANES), F32),
                   ghps, ghps, jax.ShapeDtypeStruct((1, hg_w), F32)],
        name="prep",
    )(s5_a_re[0], s5_a_im[0], s5_log_dt[0][:, None],
      jnp.swapaxes(s5_b_re[0], 1, 2), jnp.swapaxes(s5_b_im[0], 1, 2), s5_c_re[0], s5_c_im[0],
      hg_lb_logits)
    bk = bk.astype(BF16)
    gk = gk.astype(BF16)

    def bd_t(v):
        v = jnp.swapaxes(v, 1, 2)
        return _block_diag(v.reshape((n_tiles, GROUPS_PER_TILE) + v.shape[1:]))
    ck = jnp.concatenate([jnp.concatenate([bd_t(s5_c_re[0]), bd_t(-s5_c_im[0])], axis=1),
                          jnp.concatenate([bd_t(cl_re), bd_t(-cl_im)], axis=1)], axis=-1).astype(BF16)

    n_steps = grid[0] * grid[1]

    def row_slices(n_rows, n_cols):
        rows = CAST_ROWS_MIN * -(-n_rows // (n_steps * CAST_ROWS_MIN))
        while n_rows % rows:
            rows += CAST_ROWS_MIN
        return pl.BlockSpec((rows, n_cols),
                            lambda g, t: (jnp.minimum(g * grid[1] + t, n_rows // rows - 1), 0))
    up_slices = row_slices(d_model, 2 * d_ff)
    down_slices = row_slices(d_ff, d_model)
    x1, w_up_b, w_down_b = pl.pallas_call(
        functools.partial(_mixer_kernel, widths=(s5_w, hg_w, d_model)),
        grid=grid,
        in_specs=[
            _seq_spec(d_model, TT),
            _const_spec((1, d_model)),
            _const_spec((d_model, n_in)),
            _const_spec((n_tiles, SEQS, STATE_TILE)),
            _const_spec((n_tiles, SEQS, STATE_TILE)),
            _const_spec((n_tiles, 2 * LANES, 2 * STATE_TILE)),
            _const_spec((n_tiles, 2 * STATE_TILE, 2 * LANES)),
            _const_spec((n_tiles, LANES, LANES)),
            _const_spec((1, s5_w)),
            _const_spec((s5_w, s5_w)),
            _const_spec((1, s5_w)),
            _const_spec((1, hg_w)),
            _const_spec((1, hg_w)),
            _const_spec((s5_w, d_model)),
            _const_spec((hg_w, d_model)),
            _const_spec((d_model, d_model)),
            up_slices, down_slices],
        out_specs=[pl.BlockSpec((1, ROWS, d_model), lambda g, t: (g, t, 0)), up_slices, down_slices],
        out_shape=[jax.ShapeDtypeStruct((bsz // SEQS, seq * SEQS, d_model), BF16),
                   jax.ShapeDtypeStruct((d_model, 2 * d_ff), BF16),
                   jax.ShapeDtypeStruct((d_ff, d_model), BF16)],
        scratch_shapes=[pltpu.VMEM((d_model // LANES, ROWS, LANES), F32),
                        pltpu.VMEM((ROWS // 2, n_tiles * 2 * STATE_TILE), F32),
                        pltpu.VMEM((n_tiles, 2, SEQS, STATE_TILE), F32),
                        pltpu.VMEM((n_tiles, SEQS, LANES), F32),
                        pltpu.VMEM((n_heads, HG_HEAD_DIM, SEQS * HG_HEAD_DIM), F32),
                        pltpu.VMEM((ROWS, hg_w), F32),
                        pltpu.VMEM((4, ROWS, hg_w), F32)],
        compiler_params=cparams,
        name="mixer",
    )(x, g_mix[0][None], w_in[0].astype(BF16), lam2_re_t, lam2_im_t, bk, ck, gk,
      s5_d[0][None], w_glu[0].astype(BF16), b_glu[0][None], lb, hg_norm_gain[0][None],
      w_pa[0].astype(BF16), w_pb[0].astype(BF16), w_out[0].astype(BF16), w_up[0], w_down[0])

    return pl.pallas_call(
        _ffn_kernel,
        grid=(bsz // SEQS, seq // FFN_TT),
        in_specs=[pl.BlockSpec((1, FFN_ROWS, d_model), lambda g, t: (g, t, 0)),
                  _const_spec((1, d_model)),
                  _const_spec((d_model, 2 * d_ff)),
                  _const_spec((CONV_W, 2 * d_ff)),
                  _const_spec((1, 2 * d_ff)),
                  _const_spec((d_ff, d_model)),
                  _const_spec((1, d_model))],
        out_specs=_seq_spec(d_model, FFN_TT),
        out_shape=jax.ShapeDtypeStruct((bsz, seq, d_model), F32),
        scratch_shapes=[pltpu.VMEM((2 * SEQS, 2 * d_ff), F32),
                        pltpu.VMEM((d_model // LANES, FFN_ROWS, LANES), F32),
                        pltpu.VMEM((FFN_ROWS, d_ff), BF16)],
        compiler_params=cparams,
        name="ffn",
    )(x1, g_ffn[0][None], w_up_b, w_conv[0], b_conv[0][None], w_down_b, g_final[None])
```
